```python
import math
import jax, jax.numpy as jnp
from jax import lax
import numpy as np

D_MODEL = 1024
BATCH = 8
SEQ = 2048
DEPTH = 2

A_HEADS = 4
A_DH = 64
A_DV = 2 * A_DH
B_HEADS = 8
B_DQLAT = 128
B_DLAT = 128
B_DV = 64
IDX_HEADS = 8
IDX_DH = 32
TOPK_MAX = 256
AB_WIDTH = A_HEADS * A_DV + B_HEADS * B_DV
AB_SPLITS = (A_HEADS * 2 * A_DH, A_HEADS * 2 * A_DH, A_HEADS * A_DV,
             B_DQLAT, B_DLAT, IDX_DH, IDX_HEADS, AB_WIDTH)
C_HEADS = 8
C_KV_HEADS = 2
C_DH = 64
WINDOW = 128
D_CH = 512
CONV_W = 31
CD_WIDTH = C_HEADS * C_DH + D_CH
CD_SPLITS = (C_HEADS * C_DH, C_KV_HEADS * C_DH, C_KV_HEADS * C_DH, D_CH, D_CH, CD_WIDTH)

Q_BLOCK = 128
EPS = 1e-6
N_AB = (DEPTH + 1) // 2
N_CD = DEPTH // 2

kernel_name = "hybrid_diffattn_dsa_swa_conformer_adaln"


def rmsnorm(x, g):
    xf = x.astype(jnp.float32)
    y = xf * lax.rsqrt(jnp.mean(xf * xf, axis=-1, keepdims=True) + EPS)
    return (y * g.astype(jnp.float32)).astype(x.dtype)


def layernorm(x, g, b):
    xf = x.astype(jnp.float32)
    mu = jnp.mean(xf, axis=-1, keepdims=True)
    var = jnp.mean(jnp.square(xf - mu), axis=-1, keepdims=True)
    y = (xf - mu) * lax.rsqrt(var + EPS)
    return (y * g.astype(jnp.float32) + b.astype(jnp.float32)).astype(x.dtype)


def alibi_slopes(n):
    return jnp.asarray(np.power(2.0, -8.0 * np.arange(1, n + 1) / n).astype(np.float32))


def _split(t, sizes):
    offs, acc = [], 0
    for s_ in sizes[:-1]:
        acc += s_
        offs.append(acc)
    return jnp.split(t, offs, axis=-1)


def _to_blocks(t, nb):
    return t.reshape(t.shape[0], nb, Q_BLOCK, *t.shape[2:]).swapaxes(0, 1)


def _from_blocks(t):
    t = t.swapaxes(0, 1)
    return t.reshape(t.shape[0], t.shape[1] * t.shape[2], *t.shape[3:])


def diff_attention(q, k, v, lam, slopes):
    S = q.shape[1]
    nb = S // Q_BLOCK
    kpos = jnp.arange(S)
    scale = A_DH ** -0.5

    def one_block(args):
        qblk, i = args
        qpos = i * Q_BLOCK + jnp.arange(Q_BLOCK)
        s = jnp.einsum('bqhcd,bkhcd->bhcqk', qblk, k).astype(jnp.float32) * scale
        dist = (qpos[:, None] - kpos[None, :]).astype(jnp.float32)
        s = jnp.where(dist >= 0, s - slopes[:, None, None, None] * dist, -jnp.inf)
        p = jax.nn.softmax(s, axis=-1)
        w = p[:, :, 0] - lam * p[:, :, 1]
        return jnp.einsum('bhqk,bkhd->bqhd', w.astype(v.dtype), v)

    out = lax.map(one_block, (_to_blocks(q, nb), jnp.arange(nb)))
    return _from_blocks(out)


def dsa_attention(q_lat, kv_lat, q_idx, k_idx, w_idx, slopes):
    S = kv_lat.shape[1]
    topk = min(TOPK_MAX, S // 4)
    nb = S // Q_BLOCK
    kpos = jnp.arange(S)
    scale = B_DLAT ** -0.5

    def one_block(args):
        ql, qi, wi, i = args
        qpos = i * Q_BLOCK + jnp.arange(Q_BLOCK)
        causal = kpos[None, :] <= qpos[:, None]
        rel = jax.nn.relu(jnp.einsum('bqhd,bkd->bqhk', qi, k_idx))
        iscore = jnp.einsum('bqh,bqhk->bqk', wi, rel).astype(jnp.float32)
        iscore = jnp.where(causal[None], iscore, -jnp.inf)
        _, sel = lax.top_k(iscore, topk)
        kv_sel = jax.vmap(lambda lat, ix: lat[ix])(kv_lat, sel)
        s = jnp.einsum('bqhd,bqkd->bhqk', ql, kv_sel).astype(jnp.float32) * scale
        dist = (qpos[None, :, None] - sel).astype(jnp.float32)
        s = jnp.where((dist >= 0)[:, None], s - slopes[None, :, None, None] * dist[:, None], -jnp.inf)
        p = jax.nn.softmax(s, axis=-1)
        return jnp.einsum('bhqk,bqkd->bqhd', p.astype(kv_sel.dtype), kv_sel)

    out = lax.map(one_block, (_to_blocks(q_lat, nb), _to_blocks(q_idx, nb),
                              _to_blocks(w_idx, nb), jnp.arange(nb)))
    return _from_blocks(out)


def sliding_window_gqa(q, k, v, sinks, slopes):
    Bsz, S, H, dh = q.shape
    G = k.shape[2]
    R = H // G
    nb = S // WINDOW
    qb = q.reshape(Bsz, nb, WINDOW, G, R, dh)

    def band(t):
        tb = t.reshape(Bsz, nb, WINDOW, G, dh)
        prev = jnp.concatenate([jnp.zeros_like(tb[:, :1]), tb[:, :-1]], axis=1)
        return jnp.concatenate([prev, tb], axis=2)

    kb, vb = band(k), band(v)
    qpos = jnp.arange(WINDOW) + WINDOW
    kpos = jnp.arange(2 * WINDOW)
    dist = qpos[:, None] - kpos[None, :]
    valid = (dist >= 0) & (dist < WINDOW)
    valid = valid[None] & ((jnp.arange(nb)[:, None, None] > 0) | (kpos >= WINDOW)[None, None, :])
    s = jnp.einsum('bnqgrd,bnkgd->bngrqk', qb, kb).astype(jnp.float32) * (dh ** -0.5)
    s = s - slopes.reshape(G, R)[:, :, None, None] * dist.astype(jnp.float32)
    s = jnp.where(valid[None, :, None, None], s, -jnp.inf)
    sink = jnp.broadcast_to(sinks.astype(jnp.float32).reshape(G, R)[:, :, None, None], s.shape[:-1] + (1,))
    p = jax.nn.softmax(jnp.concatenate([s, sink], axis=-1), axis=-1)[..., :-1]
    o = jnp.einsum('bngrqk,bnkgd->bnqgrd', p.astype(v.dtype), vb)
    return o.reshape(Bsz, S, H, dh)


def conformer_conv(val, gate, dw_w, dw_b, ln_g, ln_b):
    h = val * jax.nn.sigmoid(gate)
    hp = jnp.pad(h, ((0, 0), (CONV_W - 1, 0), (0, 0)))
    h = lax.conv_general_dilated(hp, dw_w[:, None, :], window_strides=(1,), padding='VALID',
                                 dimension_numbers=('NWC', 'WIO', 'NWC'),
                                 feature_group_count=D_CH) + dw_b
    return jax.nn.silu(layernorm(h, ln_g, ln_b))


def mixer_ab(h, layer_idx, w_in, q_norm_g, kv_norm_g, w_uq, w_qidx, w_uv,
             lam_q1, lam_k1, lam_q2, lam_k2, subln_g, w_out):
    Bsz, S, _ = h.shape
    qa, ka, va, cq, ckv, kidx, widx, gate = _split(h @ w_in, AB_SPLITS)
    lam_init = 0.8 - 0.6 * math.exp(-0.3 * layer_idx)
    lam = (jnp.exp(jnp.sum(lam_q1.astype(jnp.float32) * lam_k1.astype(jnp.float32)))
           - jnp.exp(jnp.sum(lam_q2.astype(jnp.float32) * lam_k2.astype(jnp.float32))) + lam_init)
    oa = diff_attention(qa.reshape(Bsz, S, A_HEADS, 2, A_DH), ka.reshape(Bsz, S, A_HEADS, 2, A_DH),
                        va.reshape(Bsz, S, A_HEADS, A_DV), lam, alibi_slopes(A_HEADS))
    oa = rmsnorm(oa, subln_g) * (1.0 - lam_init)
    cq = rmsnorm(cq, q_norm_g)
    ckv = rmsnorm(ckv, kv_norm_g)
    q_lat = jnp.einsum('bsc,chd->bshd', cq, w_uq)
    q_idx = jnp.einsum('bsc,chd->bshd', cq, w_qidx)
    ob = dsa_attention(q_lat, ckv, q_idx, kidx, widx, alibi_slopes(B_HEADS))
    ob = jnp.einsum('bshd,hde->bshe', ob, w_uv)
    y = jnp.concatenate([oa.reshape(Bsz, S, -1), ob.reshape(Bsz, S, -1)], axis=-1) * jax.nn.silu(gate)
    return y @ w_out


def mixer_cd(h, w_in, sinks, dw_w, dw_b, ln_g, ln_b, w_out):
    Bsz, S, _ = h.shape
    qc, kc, vc, dval, dgate, gate = _split(h @ w_in, CD_SPLITS)
    oc = sliding_window_gqa(qc.reshape(Bsz, S, C_HEADS, C_DH), kc.reshape(Bsz, S, C_KV_HEADS, C_DH),
                            vc.reshape(Bsz, S, C_KV_HEADS, C_DH), sinks, alibi_slopes(C_HEADS))
    od = conformer_conv(dval, dgate, dw_w, dw_b, ln_g, ln_b)
    y = jnp.concatenate([oc.reshape(Bsz, S, -1), od], axis=-1) * jax.nn.silu(gate)
    return y @ w_out


def setup_inputs(seed: int = 0) -> dict:
    key = jax.random.key(seed)
    ks = jax.random.split(key, 32)
    f32 = jnp.float32
    nrm = lambda k, shape, s: jax.random.normal(k, shape, f32) * s
    ab_cols = sum(AB_SPLITS)
    cd_cols = sum(CD_SPLITS)
    return {
        'x': nrm(ks[0], (BATCH, SEQ, D_MODEL), 1.0),
        'c': nrm(ks[1], (BATCH, D_MODEL), 1.0),
        'ada_w': nrm(ks[2], (DEPTH, D_MODEL, 3 * D_MODEL), D_MODEL ** -0.5),
        'ada_b': nrm(ks[3], (DEPTH, 3 * D_MODEL), 0.02),
        'norm_g': 1.0 + nrm(ks[4], (DEPTH, D_MODEL), 0.02),
        'ab_w_in': nrm(ks[5], (N_AB, D_MODEL, ab_cols), D_MODEL ** -0.5),
        'ab_q_norm_g': 1.0 + nrm(ks[6], (N_AB, B_DQLAT), 0.02),
        'ab_kv_norm_g': 1.0 + nrm(ks[7], (N_AB, B_DLAT), 0.02),
        'ab_w_uq': nrm(ks[8], (N_AB, B_DQLAT, B_HEADS, B_DLAT), B_DQLAT ** -0.5),
        'ab_w_qidx': nrm(ks[9], (N_AB, B_DQLAT, IDX_HEADS, IDX_DH), B_DQLAT ** -0.5),
        'ab_w_uv': nrm(ks[10], (N_AB, B_HEADS, B_DLAT, B_DV), B_DLAT ** -0.5),
        'ab_lam_q1': nrm(ks[11], (N_AB, A_DH), 0.1),
        'ab_lam_k1': nrm(ks[12], (N_AB, A_DH), 0.1),
        'ab_lam_q2': nrm(ks[13], (N_AB, A_DH), 0.1),
        'ab_lam_k2': nrm(ks[14], (N_AB, A_DH), 0.1),
        'ab_subln_g': 1.0 + nrm(ks[15], (N_AB, A_DV), 0.02),
        'ab_w_out': nrm(ks[16], (N_AB, AB_WIDTH, D_MODEL), AB_WIDTH ** -0.5),
        'cd_w_in': nrm(ks[17], (N_CD, D_MODEL, cd_cols), D_MODEL ** -0.5),
        'cd_sinks': nrm(ks[18], (N_CD, C_HEADS), 0.5),
        'cd_dw_w': nrm(ks[19], (N_CD, CONV_W, D_CH), CONV_W ** -0.5),
        'cd_dw_b': nrm(ks[20], (N_CD, D_CH), 0.02),
        'cd_ln_g': 1.0 + nrm(ks[21], (N_CD, D_CH), 0.02),
        'cd_ln_b': nrm(ks[22], (N_CD, D_CH), 0.02),
        'cd_w_out': nrm(ks[23], (N_CD, CD_WIDTH, D_MODEL), CD_WIDTH ** -0.5),
        'final_g': 1.0 + nrm(ks[24], (D_MODEL,), 0.02),
    }


def reference(x, c, ada_w, ada_b, norm_g, ab_w_in, ab_q_norm_g, ab_kv_norm_g, ab_w_uq, ab_w_qidx,
              ab_w_uv, ab_lam_q1, ab_lam_k1, ab_lam_q2, ab_lam_k2, ab_subln_g, ab_w_out,
              cd_w_in, cd_sinks, cd_dw_w, cd_dw_b, cd_ln_g, cd_ln_b, cd_w_out, final_g):
    mods = jnp.einsum('bd,ldm->lbm', jax.nn.silu(c), ada_w) + ada_b[:, None]
    for l in range(DEPTH):
        shift, scale, gate = jnp.split(mods[l], 3, axis=-1)
        h = rmsnorm(x, norm_g[l]) * (1.0 + scale[:, None]) + shift[:, None]
        j = l // 2
        if l % 2 == 0:
            y = mixer_ab(h, l, ab_w_in[j], ab_q_norm_g[j], ab_kv_norm_g[j], ab_w_uq[j], ab_w_qidx[j],
                         ab_w_uv[j], ab_lam_q1[j], ab_lam_k1[j], ab_lam_q2[j], ab_lam_k2[j],
                         ab_subln_g[j], ab_w_out[j])
        else:
            y = mixer_cd(h, cd_w_in[j], cd_sinks[j], cd_dw_w[j], cd_dw_b[j], cd_ln_g[j], cd_ln_b[j],
                         cd_w_out[j])
        x = x + gate[:, None] * y
    return rmsnorm(x, final_g)
```

```python
import functools
import math

import numpy as np
import jax
import jax.numpy as jnp
from jax import lax
from jax.experimental import pallas as pl
from jax.experimental.pallas import tpu as pltpu

F32 = jnp.float32
BF16 = jnp.bfloat16
I32 = jnp.int32

D_MODEL = 1024
EPS = 1e-6
A_HEADS = 4
A_DH = 64
A_DV = 128
B_HEADS = 8
B_DQLAT = 128
B_DLAT = 128
B_DV = 64
IDX_HEADS = 8
IDX_DH = 32
TOPK_MAX = 256
C_HEADS = 8
C_KV_HEADS = 2
C_DH = 64
WINDOW = 128
D_CH = 512
CONV_W = 31

LANES = 128
SUBLANES = 8
INT_MIN = -2 ** 31
ORDER_OF_NEG_INF = INT_MIN + 0x7FFFFF
NEG_INF = float("-inf")
VMEM_LIMIT = 48 * 1024 * 1024

ROW_TILE = 512
A_TQ = 256
B_TQ = 128
B_KC = 256
CONV_TILE = 256
CONV_HALO = 32


def _alibi(n):
    return [float(2.0 ** (-8.0 * i / n)) for i in range(1, n + 1)]


def _params(sem):
    return pltpu.CompilerParams(dimension_semantics=sem, vmem_limit_bytes=VMEM_LIMIT)


def _nt_dot(a, b):
    return lax.dot_general(a, b, (((1,), (1,)), ((), ())), preferred_element_type=F32)


def _rms(x, g):
    return x * lax.rsqrt(jnp.mean(x * x, axis=-1, keepdims=True) + EPS) * g


def _ada_kernel(c_ref, w_ref, b_ref, o_ref):
    c = c_ref[...]
    sc = c * jax.nn.sigmoid(c)
    o_ref[0] = jnp.dot(sc.astype(BF16), w_ref[0].astype(BF16), preferred_element_type=F32) + b_ref[0]


def _ada_mods(c, ada_w, ada_b):
    depth, d, d3 = ada_w.shape
    bsz = c.shape[0]
    nt = d3 // d
    return pl.pallas_call(
        _ada_kernel,
        grid=(depth, nt),
        in_specs=[pl.BlockSpec((bsz, d), lambda l, j: (0, 0)),
                  pl.BlockSpec((1, d, d), lambda l, j: (l, 0, j)),
                  pl.BlockSpec((1, 1, d), lambda l, j: (l, 0, j))],
        out_specs=pl.BlockSpec((1, bsz, d), lambda l, j: (l, 0, j)),
        out_shape=jax.ShapeDtypeStruct((depth, bsz, d3), F32),
        compiler_params=_params(("arbitrary", "arbitrary")),
        name="ada_mods",
    )(c, ada_w, ada_b.reshape(depth, 1, d3))


def _proj0_kernel(x_ref, shift_ref, scale_ref, ng_ref, wmain_ref, wgate_ref, wkidx_ref, wwidx_ref,
                  qng_ref, kvng_ref, wuq_ref, wqidx_ref,
                  q1_ref, q2_ref, ka_ref, va_ref, qlat_ref, ckv_ref, qidx_ref, kidx_ref, widxt_ref, sg_ref):
    x = x_ref[0]
    h = _rms(x, ng_ref[...]) * (1.0 + scale_ref[0]) + shift_ref[0]
    hb = h.astype(BF16)
    main = jnp.dot(hb, wmain_ref[...], preferred_element_type=F32)
    qw = A_HEADS * 2 * A_DH
    qa = main[:, :qw] * (A_DH ** -0.5)
    first = (lax.broadcasted_iota(I32, qa.shape, 1) % (2 * A_DH)) < A_DH
    q1_ref[0] = jnp.where(first, qa, 0.0).astype(BF16)
    q2_ref[0] = jnp.where(first, 0.0, qa).astype(BF16)
    ka_ref[0] = main[:, qw:2 * qw].astype(BF16)
    va_ref[0] = main[:, 2 * qw:3 * qw].astype(BF16)
    cq = _rms(main[:, 3 * qw:3 * qw + B_DQLAT], qng_ref[...]).astype(BF16)
    ckv_ref[0] = _rms(main[:, 3 * qw + B_DQLAT:], kvng_ref[...]).astype(BF16)
    qlat = jnp.dot(cq, wuq_ref[...], preferred_element_type=F32) * (B_DLAT ** -0.5)
    qlat_ref[0] = qlat.astype(BF16)
    qidx_ref[0] = jnp.dot(cq, wqidx_ref[...], preferred_element_type=F32).astype(BF16)
    kidx_ref[0] = jnp.dot(hb, wkidx_ref[...], preferred_element_type=F32).astype(BF16)
    widxt_ref[0] = _nt_dot(wwidx_ref[...], hb)
    gate = jnp.dot(hb, wgate_ref[...], preferred_element_type=F32)
    sg_ref[0] = (gate * jax.nn.sigmoid(gate)).astype(BF16)


def _proj0(x, shift, scale, norm_g, w_in, q_norm_g, kv_norm_g, w_uq, w_qidx):
    bsz, seq, d = x.shape
    t = min(ROW_TILE, seq)
    qw = A_HEADS * 2 * A_DH
    o_main = 3 * qw + B_DQLAT + B_DLAT
    wmain = w_in[:, :o_main].astype(BF16)
    wkidx = jnp.tile(w_in[:, o_main:o_main + IDX_DH], (1, IDX_HEADS)).astype(BF16)
    wwidx = w_in[:, o_main + IDX_DH:o_main + IDX_DH + IDX_HEADS].T.astype(BF16)
    wgate = w_in[:, o_main + IDX_DH + IDX_HEADS:].astype(BF16)
    wuq = w_uq.reshape(B_DQLAT, B_HEADS * B_DLAT).astype(BF16)
    wqidx = w_qidx.reshape(B_DQLAT, IDX_HEADS * IDX_DH).astype(BF16)
    row = lambda w: pl.BlockSpec((1, t, w), lambda b, i: (b, i, 0))
    mod = pl.BlockSpec((1, 1, d), lambda b, i: (b, 0, 0))
    full = lambda a: pl.BlockSpec(a.shape, lambda b, i: (0,) * a.ndim)
    ng = norm_g.reshape(1, d)
    qng = q_norm_g.reshape(1, B_DQLAT)
    kvng = kv_norm_g.reshape(1, B_DLAT)
    iw = IDX_HEADS * IDX_DH
    outs = [(qw, BF16), (qw, BF16), (qw, BF16), (qw, BF16), (B_HEADS * B_DLAT, BF16), (B_DLAT, BF16),
            (iw, BF16), (iw, BF16)]
    out_shape = [jax.ShapeDtypeStruct((bsz, seq, w), dt) for w, dt in outs]
    out_specs = [row(w) for w, _ in outs]
    out_shape.append(jax.ShapeDtypeStruct((bsz, IDX_HEADS, seq), F32))
    out_specs.append(pl.BlockSpec((1, IDX_HEADS, t), lambda b, i: (b, 0, i)))
    out_shape.append(jax.ShapeDtypeStruct((bsz, seq, d), BF16))
    out_specs.append(row(d))
    return pl.pallas_call(
        _proj0_kernel,
        grid=(bsz, seq // t),
        in_specs=[row(d), mod, mod, full(ng), full(wmain), full(wgate), full(wkidx), full(wwidx),
                  full(qng), full(kvng), full(wuq), full(wqidx)],
        out_specs=out_specs,
        out_shape=out_shape,
        compiler_params=_params(("arbitrary", "arbitrary")),
        name="proj0",
    )(x, shift, scale, ng, wmain, wgate, wkidx, wwidx, qng, kvng, wuq, wqidx)


def _diff_attn_kernel(q1_ref, q2_ref, k_ref, v_ref, lq1_ref, lk1_ref, lq2_ref, lk2_ref, g_ref, o_ref,
                      m_ref, l_ref, acc_ref, *, lam_init):
    i = pl.program_id(1)
    tq = q1_ref.shape[1]
    tk = tq
    lam = (jnp.exp(jnp.sum(lq1_ref[...] * lk1_ref[...], keepdims=True))
           - jnp.exp(jnp.sum(lq2_ref[...] * lk2_ref[...], keepdims=True)) + lam_init)
    krel = lax.broadcasted_iota(I32, (1, tk), 1)
    row = lax.broadcasted_iota(I32, (2 * tq, tk), 0) % tq
    col = lax.broadcasted_iota(I32, (2 * tq, tk), 1)
    causal = col <= row
    slopes = _alibi(A_HEADS)
    for h in range(A_HEADS):
        hs = slice(h * A_DV, (h + 1) * A_DV)
        qs = jnp.concatenate([q1_ref[0, :, hs], q2_ref[0, :, hs]], axis=0)
        slope = slopes[h]

        def scores(j):
            kc = k_ref[0, pl.ds(j * tk, tk), hs]
            kb = ((krel + (j - i) * tk).astype(F32)) * slope
            return _nt_dot(qs, kc) + kb

        s = jnp.where(causal, scores(i), NEG_INF)
        m0 = jnp.max(s, axis=-1, keepdims=True)
        p = jnp.exp(s - m0)
        m_ref[...] = m0
        l_ref[...] = jnp.sum(p, axis=-1, keepdims=True)
        acc_ref[...] = jnp.dot(p.astype(BF16), v_ref[0, pl.ds(i * tk, tk), hs], preferred_element_type=F32)

        def body(j, carry):
            s = scores(j)
            m_prev = m_ref[...]
            m_new = jnp.maximum(m_prev, jnp.max(s, axis=-1, keepdims=True))
            alpha = jnp.exp(m_prev - m_new)
            p = jnp.exp(s - m_new)
            l_ref[...] = alpha * l_ref[...] + jnp.sum(p, axis=-1, keepdims=True)
            acc_ref[...] = alpha * acc_ref[...] + jnp.dot(
                p.astype(BF16), v_ref[0, pl.ds(j * tk, tk), hs], preferred_element_type=F32)
            m_ref[...] = m_new
            return carry

        lax.fori_loop(0, i, body, 0)
        o = acc_ref[...] * (1.0 / l_ref[...])
        od = o[:tq] - lam * o[tq:]
        o_ref[0, :, hs] = (_rms(od, g_ref[...]) * (1.0 - lam_init)).astype(o_ref.dtype)


def _diff_attn(q1, q2, ka, va, lam_q1, lam_k1, lam_q2, lam_k2, subln_g, layer_idx):
    bsz, seq, w = q1.shape
    tq = min(A_TQ, seq)
    lam_init = 0.8 - 0.6 * math.exp(-0.3 * layer_idx)
    vec = lambda a: a.reshape(1, -1)
    full = lambda a: pl.BlockSpec(a.shape, lambda b, i: (0,) * a.ndim)
    small = [vec(lam_q1), vec(lam_k1), vec(lam_q2), vec(lam_k2), vec(subln_g)]
    return pl.pallas_call(
        functools.partial(_diff_attn_kernel, lam_init=lam_init),
        grid=(bsz, seq // tq),
        in_specs=[pl.BlockSpec((1, tq, w), lambda b, i: (b, i, 0)),
                  pl.BlockSpec((1, tq, w), lambda b, i: (b, i, 0)),
                  pl.BlockSpec((1, seq, w), lambda b, i: (b, 0, 0)),
                  pl.BlockSpec((1, seq, w), lambda b, i: (b, 0, 0))] + [full(a) for a in small],
        out_specs=pl.BlockSpec((1, tq, w), lambda b, i: (b, i, 0)),
        out_shape=jax.ShapeDtypeStruct((bsz, seq, w), BF16),
        scratch_shapes=[pltpu.VMEM((2 * tq, 1), F32), pltpu.VMEM((2 * tq, 1), F32),
                        pltpu.VMEM((2 * tq, A_DV), F32)],
        compiler_params=_params(("arbitrary", "arbitrary")),
        name="diff_attn",
    )(q1, q2, ka, va, *small)


def _dsa_kernel(qlat_ref, qidx_ref, wt_ref, kv_ref, kidx_ref, wuv_ref, o_ref,
                key_ref, bias_ref, thr_ref, lim_ref, m_ref, l_ref, acc_ref, *, topk):
    qi = pl.program_id(1)
    tq = qlat_ref.shape[1]
    kc_rows = bias_ref.shape[2]
    nch = (qi * tq) // kc_rows + 1
    qpos = qi * tq + lax.broadcasted_iota(I32, (kc_rows, tq), 1)
    krow = lax.broadcasted_iota(I32, (kc_rows, tq), 0)

    qidx = qidx_ref[0].astype(F32)
    head_of_lane = lax.broadcasted_iota(I32, qidx.shape, 1) // IDX_DH
    qstack = jnp.concatenate([jnp.where(head_of_lane == h, qidx, 0.0) for h in range(IDX_HEADS)],
                             axis=0).astype(BF16)
    wt = wt_ref[0]

    def make_keys(c, carry):
        kx = kidx_ref[0, pl.ds(c * kc_rows, kc_rows), :]
        rel = _nt_dot(kx, qstack)
        isc = jnp.zeros((kc_rows, tq), F32)
        for h in range(IDX_HEADS):
            isc = isc + wt[h:h + 1, :] * jnp.maximum(rel[:, h * tq:(h + 1) * tq], 0.0)
        key_ref[pl.ds(c * kc_rows, kc_rows), :] = jnp.where(c * kc_rows + krow <= qpos, isc, NEG_INF)
        return carry

    lax.fori_loop(0, nch, make_keys, 0)

    def count(pred):
        def body(c, acc):
            blk = key_ref[pl.ds(c * kc_rows, kc_rows), :]
            ind = pred(blk, c * kc_rows + krow).astype(I32)
            return acc + jnp.sum(ind.reshape(kc_rows // SUBLANES, SUBLANES, tq), axis=0)
        acc = lax.fori_loop(0, nch, body, jnp.zeros((SUBLANES, tq), I32))
        return jnp.sum(acc, axis=0, keepdims=True)

    thr_ref[...] = jnp.full((1, tq), NEG_INF, F32)
    lim_ref[...] = jnp.full((1, tq), 2 ** 30, I32)

    def to_float(u):
        s = jnp.maximum(u ^ INT_MIN, ORDER_OF_NEG_INF)
        return pltpu.bitcast(s ^ ((s >> 31) & 0x7FFFFFFF), F32)

    @pl.when((qi + 1) * tq > topk)
    def _():
        def bit_step(t, cand):
            trial = cand | jnp.left_shift(jnp.int32(1), 31 - t)
            thr = to_float(trial)
            cnt = count(lambda x, p: x >= thr)
            return jnp.where(cnt >= topk, trial, cand)
        thr = to_float(lax.fori_loop(0, 32, bit_step, jnp.zeros((1, tq), I32)))
        thr_ref[...] = thr
        need = topk - count(lambda x, p: x > thr)
        n_eq = count(lambda x, p: x == thr)

        @pl.when(jnp.max((n_eq > need).astype(I32)) > 0)
        def _():
            def pos_step(t, lim):
                trial = lim | jnp.left_shift(jnp.int32(1), 11 - t)
                cnt = count(lambda x, p: (x == thr) & (p < trial))
                return jnp.where(cnt < need, trial, lim)
            lim_ref[...] = lax.fori_loop(0, 12, pos_step, jnp.zeros((1, tq), I32))

    thr = thr_ref[...]
    lim = lim_ref[...]

    def make_bias(c, carry):
        x = key_ref[pl.ds(c * kc_rows, kc_rows), :]
        kpos = c * kc_rows + krow
        sel = ((x > thr) | ((x == thr) & (kpos <= lim))) & (kpos <= qpos)
        bias_ref[c] = jnp.where(sel, 0.0, NEG_INF).T
        return carry

    lax.fori_loop(0, nch, make_bias, 0)

    qs = jnp.concatenate([qlat_ref[0, :, h * B_DLAT:(h + 1) * B_DLAT] for h in range(B_HEADS)], axis=0)
    m_ref[...] = jnp.full(m_ref.shape, NEG_INF, F32)
    l_ref[...] = jnp.zeros(l_ref.shape, F32)
    acc_ref[...] = jnp.zeros(acc_ref.shape, F32)
    slopes = _alibi(B_HEADS)
    kcol = lax.broadcasted_iota(I32, (1, kc_rows), 1)

    def attend(c, carry):
        kvc = kv_ref[0, pl.ds(c * kc_rows, kc_rows), :]
        s = _nt_dot(qs, kvc)
        bias = bias_ref[c]
        krel = (kcol + (c * kc_rows - qi * tq)).astype(F32)
        s = jnp.concatenate([s[h * tq:(h + 1) * tq] + (bias + krel * slopes[h]) for h in range(B_HEADS)], axis=0)
        m_prev = m_ref[...]
        m_new = jnp.maximum(m_prev, jnp.max(s, axis=-1, keepdims=True))
        m_safe = jnp.where(m_new == NEG_INF, 0.0, m_new)
        alpha = jnp.exp(m_prev - m_safe)
        p = jnp.exp(s - m_safe)
        l_ref[...] = alpha * l_ref[...] + jnp.sum(p, axis=-1, keepdims=True)
        acc_ref[...] = alpha * acc_ref[...] + jnp.dot(p.astype(BF16), kvc, preferred_element_type=F32)
        m_ref[...] = m_new
        return carry

    lax.fori_loop(0, nch, attend, 0)

    o = (acc_ref[...] * (1.0 / l_ref[...])).astype(BF16)
    o_all = jnp.concatenate([o[h * tq:(h + 1) * tq] for h in range(B_HEADS)], axis=1)
    o_ref[0] = jnp.dot(o_all, wuv_ref[...], preferred_element_type=F32).astype(o_ref.dtype)


def _dsa(qlat, qidx, widxt, ckv, kidx, w_uv):
    bsz, seq, _ = qlat.shape
    tq = min(B_TQ, seq)
    kc = min(B_KC, seq)
    topk = min(TOPK_MAX, seq // 4)
    eye = jnp.eye(B_HEADS, dtype=w_uv.dtype)
    wuv = jnp.einsum('hde,hg->hdge', w_uv, eye).reshape(B_HEADS * B_DLAT, B_HEADS * B_DV).astype(BF16)
    return pl.pallas_call(
        functools.partial(_dsa_kernel, topk=topk),
        grid=(bsz, seq // tq),
        in_specs=[pl.BlockSpec((1, tq, B_HEADS * B_DLAT), lambda b, i: (b, i, 0)),
                  pl.BlockSpec((1, tq, IDX_HEADS * IDX_DH), lambda b, i: (b, i, 0)),
                  pl.BlockSpec((1, IDX_HEADS, tq), lambda b, i: (b, 0, i)),
                  pl.BlockSpec((1, seq, B_DLAT), lambda b, i: (b, 0, 0)),
                  pl.BlockSpec((1, seq, IDX_HEADS * IDX_DH), lambda b, i: (b, 0, 0)),
                  pl.BlockSpec(wuv.shape, lambda b, i: (0, 0))],
        out_specs=pl.BlockSpec((1, tq, B_HEADS * B_DV), lambda b, i: (b, i, 0)),
        out_shape=jax.ShapeDtypeStruct((bsz, seq, B_HEADS * B_DV), BF16),
        scratch_shapes=[pltpu.VMEM((seq, tq), F32),
                        pltpu.VMEM((seq // kc, tq, kc), F32),
                        pltpu.VMEM((1, tq), F32), pltpu.VMEM((1, tq), I32),
                        pltpu.VMEM((B_HEADS * tq, 1), F32), pltpu.VMEM((B_HEADS * tq, 1), F32),
                        pltpu.VMEM((B_HEADS * tq, B_DLAT), F32)],
        compiler_params=_params(("arbitrary", "arbitrary")),
        name="dsa",
    )(qlat, qidx, widxt, ckv, kidx, wuv)


def _mid_kernel(oa_ref, ob_ref, sg_ref, x_ref, gmod_ref, wout_ref, shift_ref, scale_ref, ng_ref, win_ref,
                x1_ref, qc_ref, kc_ref, vc_ref, hglu_ref, sg1_ref):
    y = jnp.concatenate([oa_ref[0], ob_ref[0]], axis=1).astype(F32) * sg_ref[0].astype(F32)
    r = jnp.dot(y.astype(BF16), wout_ref[...], preferred_element_type=F32)
    x1 = x_ref[0] + gmod_ref[0] * r
    x1_ref[0] = x1
    h = _rms(x1, ng_ref[...]) * (1.0 + scale_ref[0]) + shift_ref[0]
    p = jnp.dot(h.astype(BF16), win_ref[...], preferred_element_type=F32)
    qw = C_HEADS * C_DH
    kw = C_KV_HEADS * C_DH
    qc_ref[0] = (p[:, :qw] * (C_DH ** -0.5)).astype(BF16)
    kc_ref[0] = p[:, qw:qw + kw].astype(BF16)
    vc_ref[0] = p[:, qw + kw:qw + 2 * kw].astype(BF16)
    o = qw + 2 * kw
    hglu_ref[0] = p[:, o:o + D_CH] * jax.nn.sigmoid(p[:, o + D_CH:o + 2 * D_CH])
    gate = p[:, o + 2 * D_CH:]
    sg1_ref[0] = (gate * jax.nn.sigmoid(gate)).astype(BF16)


def _mid(oa, ob, sg, x, gmod, w_out, shift, scale, norm_g, w_in):
    bsz, seq, d = x.shape
    t = min(ROW_TILE, seq)
    wout = w_out.astype(BF16)
    win = w_in.astype(BF16)
    ng = norm_g.reshape(1, d)
    row = lambda w: pl.BlockSpec((1, t, w), lambda b, i: (b, i, 0))
    mod = pl.BlockSpec((1, 1, d), lambda b, i: (b, 0, 0))
    full = lambda a: pl.BlockSpec(a.shape, lambda b, i: (0,) * a.ndim)
    qw = C_HEADS * C_DH
    kw = C_KV_HEADS * C_DH
    outs = [(d, F32), (qw, BF16), (kw, BF16), (kw, BF16), (D_CH, F32), (d, BF16)]
    return pl.pallas_call(
        _mid_kernel,
        grid=(bsz, seq // t),
        in_specs=[row(oa.shape[2]), row(ob.shape[2]), row(d), row(d), mod, full(wout), mod, mod, full(ng),
                  full(win)],
        out_specs=[row(w) for w, _ in outs],
        out_shape=[jax.ShapeDtypeStruct((bsz, seq, w), dt) for w, dt in outs],
        compiler_params=_params(("arbitrary", "arbitrary")),
        name="out0_proj1",
    )(oa, ob, sg, x, gmod, wout, shift, scale, ng, win)


def _swa_kernel(q_ref, kp_ref, kc_ref, vp_ref, vc_ref, sink_ref, o_ref, bias_ref):
    b = pl.program_id(0)
    n = pl.program_id(1)
    w = q_ref.shape[1]
    rep = C_HEADS // C_KV_HEADS
    half = LANES // 2

    @pl.when((b == 0) & (n == 0))
    def _():
        r = lax.broadcasted_iota(I32, (w, 2 * w), 0)
        c = lax.broadcasted_iota(I32, (w, 2 * w), 1)
        dist = w + r - c
        valid = (dist >= 0) & (dist < w)
        slopes = _alibi(C_HEADS)
        for h in range(C_HEADS):
            full = jnp.where(valid, -slopes[h] * dist.astype(F32), NEG_INF)
            bias_ref[1, h * w:(h + 1) * w, :] = full
            bias_ref[0, h * w:(h + 1) * w, :] = jnp.where(c >= w, full, NEG_INF)

    lane = lax.broadcasted_iota(I32, (w, LANES), 1)
    qs = []
    for h in range(C_HEADS):
        g = h // rep
        x = q_ref[0, :, (h // 2) * LANES:(h // 2 + 1) * LANES].astype(F32)
        if (h % 2) != g:
            x = pltpu.roll(x, half, axis=1)
        qs.append(jnp.where((lane // half) == g, x, 0.0))
    qs = jnp.concatenate(qs, axis=0).astype(BF16)
    kcat = jnp.concatenate([kp_ref[0], kc_ref[0]], axis=0)
    vcat = jnp.concatenate([vp_ref[0], vc_ref[0]], axis=0)
    s = _nt_dot(qs, kcat) + bias_ref[jnp.minimum(n, 1)]
    sink = sink_ref[...]
    m = jnp.maximum(jnp.max(s, axis=-1, keepdims=True), sink)
    p = jnp.exp(s - m)
    denom = jnp.sum(p, axis=-1, keepdims=True) + jnp.exp(sink - m)
    o = jnp.dot(p.astype(BF16), vcat, preferred_element_type=F32) * (1.0 / denom)
    for j in range(C_HEADS // 2):
        g = (2 * j) // rep
        a = o[(2 * j) * w:(2 * j + 1) * w]
        bb = o[(2 * j + 1) * w:(2 * j + 2) * w]
        if g == 0:
            bb = pltpu.roll(bb, half, axis=1)
        else:
            a = pltpu.roll(a, half, axis=1)
        o_ref[0, :, j * LANES:(j + 1) * LANES] = jnp.where(lane < half, a, bb).astype(o_ref.dtype)


def _swa(qc, kc, vc, sinks):
    bsz, seq, qw = qc.shape
    w = WINDOW
    kw = kc.shape[2]
    sink_rows = jnp.repeat(sinks.astype(F32), w).reshape(C_HEADS * w, 1)
    cur = lambda width: pl.BlockSpec((1, w, width), lambda b, n: (b, n, 0))
    prev = lambda width: pl.BlockSpec((1, w, width), lambda b, n: (b, jnp.maximum(n - 1, 0), 0))
    return pl.pallas_call(
        _swa_kernel,
        grid=(bsz, seq // w),
        in_specs=[cur(qw), prev(kw), cur(kw), prev(kw), cur(kw),
                  pl.BlockSpec(sink_rows.shape, lambda b, n: (0, 0))],
        out_specs=cur(qw),
        out_shape=jax.ShapeDtypeStruct((bsz, seq, qw), BF16),
        scratch_shapes=[pltpu.VMEM((2, C_HEADS * w, 2 * w), F32)],
        compiler_params=_params(("arbitrary", "arbitrary")),
        name="swa",
    )(qc, kc, kc, vc, vc, sink_rows)


def _conv_kernel(cur_ref, halo_ref, w_ref, b_ref, g_ref, beta_ref, o_ref, buf_ref):
    i = pl.program_id(1)
    t = cur_ref.shape[1]
    halo = halo_ref.shape[1]
    buf_ref[:halo, :] = jnp.where(i > 0, halo_ref[0], 0.0)
    buf_ref[halo:, :] = cur_ref[0]
    acc = jnp.zeros((t, cur_ref.shape[2]), F32) + b_ref[...]
    off = halo - (CONV_W - 1)
    for j in range(CONV_W):
        acc = acc + w_ref[j:j + 1, :] * buf_ref[off + j:off + j + t, :]
    mu = jnp.mean(acc, axis=-1, keepdims=True)
    xc = acc - mu
    var = jnp.mean(xc * xc, axis=-1, keepdims=True)
    y = xc * lax.rsqrt(var + EPS) * g_ref[...] + beta_ref[...]
    o_ref[0] = (y * jax.nn.sigmoid(y)).astype(o_ref.dtype)


def _conv(hglu, dw_w, dw_b, ln_g, ln_b):
    bsz, seq, ch = hglu.shape
    t = min(CONV_TILE, seq)
    per = t // CONV_HALO
    vec = lambda a: a.reshape(1, ch)
    full = lambda a: pl.BlockSpec(a.shape, lambda b, i: (0,) * a.ndim)
    small = [dw_w, vec(dw_b), vec(ln_g), vec(ln_b)]
    return pl.pallas_call(
        _conv_kernel,
        grid=(bsz, seq // t),
        in_specs=[pl.BlockSpec((1, t, ch), lambda b, i: (b, i, 0)),
                  pl.BlockSpec((1, CONV_HALO, ch), lambda b, i: (b, jnp.maximum(i * per - 1, 0), 0))]
                 + [full(a) for a in small],
        out_specs=pl.BlockSpec((1, t, ch), lambda b, i: (b, i, 0)),
        out_shape=jax.ShapeDtypeStruct((bsz, seq, ch), BF16),
        scratch_shapes=[pltpu.VMEM((CONV_HALO + t, ch), F32)],
        compiler_params=_params(("arbitrary", "arbitrary")),
        name="conformer_conv",
    )(hglu, hglu, *small)


def _final_kernel(oc_ref, od_ref, sg_ref, x_ref, gmod_ref, wout_ref, fg_ref, o_ref):
    y = jnp.concatenate([oc_ref[0], od_ref[0]], axis=1).astype(F32) * sg_ref[0].astype(F32)
    r = jnp.dot(y.astype(BF16), wout_ref[...], preferred_element_type=F32)
    x2 = x_ref[0] + gmod_ref[0] * r
    o_ref[0] = _rms(x2, fg_ref[...])


def _final(oc, od, sg, x1, gmod, w_out, final_g):
    bsz, seq, d = x1.shape
    t = min(ROW_TILE, seq)
    wout = w_out.astype(BF16)
    fg = final_g.reshape(1, d)
    row = lambda w: pl.BlockSpec((1, t, w), lambda b, i: (b, i, 0))
    mod = pl.BlockSpec((1, 1, d), lambda b, i: (b, 0, 0))
    full = lambda a: pl.BlockSpec(a.shape, lambda b, i: (0,) * a.ndim)
    return pl.pallas_call(
        _final_kernel,
        grid=(bsz, seq // t),
        in_specs=[row(oc.shape[2]), row(od.shape[2]), row(d), row(d), mod, full(wout), full(fg)],
        out_specs=row(d),
        out_shape=jax.ShapeDtypeStruct((bsz, seq, d), F32),
        compiler_params=_params(("arbitrary", "arbitrary")),
        name="out1_final",
    )(oc, od, sg, x1, gmod, wout, fg)


def kernel(x, c, ada_w, ada_b, norm_g, ab_w_in, ab_q_norm_g, ab_kv_norm_g, ab_w_uq, ab_w_qidx, ab_w_uv,
           ab_lam_q1, ab_lam_k1, ab_lam_q2, ab_lam_k2, ab_subln_g, ab_w_out,
           cd_w_in, cd_sinks, cd_dw_w, cd_dw_b, cd_ln_g, cd_ln_b, cd_w_out, final_g):
    bsz, seq, d = x.shape
    mods = _ada_mods(c, ada_w, ada_b)
    mod = lambda l, k: mods[l, :, k * d:(k + 1) * d].reshape(bsz, 1, d)
    q1, q2, ka, va, qlat, ckv, qidx, kidx, widxt, sg0 = _proj0(
        x, mod(0, 0), mod(0, 1), norm_g[0], ab_w_in[0], ab_q_norm_g[0], ab_kv_norm_g[0], ab_w_uq[0],
        ab_w_qidx[0])
    oa = _diff_attn(q1, q2, ka, va, ab_lam_q1[0], ab_lam_k1[0], ab_lam_q2[0], ab_lam_k2[0], ab_subln_g[0], 0)
    ob = _dsa(qlat, qidx, widxt, ckv, kidx, ab_w_uv[0])
    x1, qc, kc, vc, hglu, sg1 = _mid(oa, ob, sg0, x, mod(0, 2), ab_w_out[0], mod(1, 0), mod(1, 1), norm_g[1],
                                     cd_w_in[0])
    oc = _swa(qc, kc, vc, cd_sinks[0])
    od = _conv(hglu, cd_dw_w[0], cd_dw_b[0], cd_ln_g[0], cd_ln_b[0])
    return _final(oc, od, sg1, x1, mod(1, 2), cd_w_out[0], final_g)
```

```python
import functools
import math

import numpy as np
import jax
import jax.numpy as jnp
from jax import lax
from jax.experimental import pallas as pl
from jax.experimental.pallas import tpu as pltpu

F32 = jnp.float32
BF16 = jnp.bfloat16
I32 = jnp.int32

D_MODEL = 1024
EPS = 1e-6
A_HEADS = 4
A_DH = 64
A_DV = 128
B_HEADS = 8
B_DQLAT = 128
B_DLAT = 128
B_DV = 64
IDX_HEADS = 8
IDX_DH = 32
TOPK_MAX = 256
C_HEADS = 8
C_KV_HEADS = 2
C_DH = 64
WINDOW = 128
D_CH = 512
CONV_W = 31

LANES = 128
SUBLANES = 8
INT_MIN = -2 ** 31
ORDER_OF_NEG_INF = INT_MIN + 0x7FFFFF
NEG_INF = float("-inf")
VMEM_LIMIT = 48 * 1024 * 1024

ROW_TILE = 512
KEY_CHUNK = 256
A_TQ = 256
B_TQ = 128
POS_SPLIT = 16
CONV_TILE = 256
CONV_HALO = 32


def _alibi(n):
    return [float(2.0 ** (-8.0 * i / n)) for i in range(1, n + 1)]


def _params(sem):
    return pltpu.CompilerParams(dimension_semantics=sem, vmem_limit_bytes=VMEM_LIMIT)


def _nt_dot(a, b):
    return lax.dot_general(a, b, (((1,), (1,)), ((), ())), preferred_element_type=F32)


def _rms(x, g):
    return x * lax.rsqrt(jnp.mean(x * x, axis=-1, keepdims=True) + EPS) * g


def _alibi_coef(rows, slope):
    lane = lax.broadcasted_iota(I32, (rows, LANES), 1)
    return jnp.where(lane == 0, POS_SPLIT * slope, jnp.where(lane == 1, slope, 0.0)).astype(BF16)


def _key_positions(seq):
    pos = np.zeros((seq, LANES), np.float32)
    pos[:, 0] = np.arange(seq) // POS_SPLIT
    pos[:, 1] = np.arange(seq) % POS_SPLIT
    return jnp.asarray(pos, BF16)


def _ada_kernel(c_ref, w_ref, b_ref, o_ref):
    c = c_ref[...]
    sc = c * jax.nn.sigmoid(c)
    o_ref[0] = jnp.dot(sc.astype(BF16), w_ref[0].astype(BF16), preferred_element_type=F32) + b_ref[0]


def _ada_mods(c, ada_w, ada_b):
    depth, d, d3 = ada_w.shape
    bsz = c.shape[0]
    nt = d3 // d
    return pl.pallas_call(
        _ada_kernel,
        grid=(depth, nt),
        in_specs=[pl.BlockSpec((bsz, d), lambda l, j: (0, 0)),
                  pl.BlockSpec((1, d, d), lambda l, j: (l, 0, j)),
                  pl.BlockSpec((1, 1, d), lambda l, j: (l, 0, j))],
        out_specs=pl.BlockSpec((1, bsz, d), lambda l, j: (l, 0, j)),
        out_shape=jax.ShapeDtypeStruct((depth, bsz, d3), F32),
        compiler_params=_params(("arbitrary", "arbitrary")),
        name="ada_mods",
    )(c, ada_w, ada_b.reshape(depth, 1, d3))


def _proj0_kernel(x_ref, shift_ref, scale_ref, ng_ref, wmain_ref, wgate_ref, wkidx_ref, wwidx_ref,
                  qng_ref, kvng_ref, wuq_ref, wqidx_ref,
                  q1_ref, q2_ref, ka_ref, vat_ref, qlat_ref, ckv_ref, ckvt_ref, qidx_ref, kidx_ref, widxt_ref,
                  sg_ref):
    x = x_ref[0]
    h = _rms(x, ng_ref[...]) * (1.0 + scale_ref[0]) + shift_ref[0]
    hb = h.astype(BF16)
    main = jnp.dot(hb, wmain_ref[...], preferred_element_type=F32)
    qw = A_HEADS * 2 * A_DH
    qa = main[:, :qw] * (A_DH ** -0.5)
    first = (lax.broadcasted_iota(I32, qa.shape, 1) % (2 * A_DH)) < A_DH
    q1_ref[0] = jnp.where(first, qa, 0.0).astype(BF16)
    q2_ref[0] = jnp.where(first, 0.0, qa).astype(BF16)
    ka_ref[0] = main[:, qw:2 * qw].astype(BF16)
    va = main[:, 2 * qw:3 * qw]
    cq = _rms(main[:, 3 * qw:3 * qw + B_DQLAT], qng_ref[...]).astype(BF16)
    ckv = _rms(main[:, 3 * qw + B_DQLAT:], kvng_ref[...])
    ckv_ref[0] = ckv.astype(BF16)
    ch = vat_ref.shape[3]
    for c in range(vat_ref.shape[1]):
        vat_ref[0, c] = va[c * ch:(c + 1) * ch].T.astype(BF16)
        ckvt_ref[0, c] = ckv[c * ch:(c + 1) * ch].T.astype(BF16)
    qlat = jnp.dot(cq, wuq_ref[...], preferred_element_type=F32) * (B_DLAT ** -0.5)
    qlat_ref[0] = qlat.astype(BF16)
    qidx_ref[0] = jnp.dot(cq, wqidx_ref[...], preferred_element_type=F32).astype(BF16)
    kidx_ref[0] = jnp.dot(hb, wkidx_ref[...], preferred_element_type=F32).astype(BF16)
    widxt_ref[0] = _nt_dot(wwidx_ref[...], hb)
    gate = jnp.dot(hb, wgate_ref[...], preferred_element_type=F32)
    sg_ref[0] = (gate * jax.nn.sigmoid(gate)).astype(BF16)


def _proj0(x, shift, scale, norm_g, w_in, q_norm_g, kv_norm_g, w_uq, w_qidx):
    bsz, seq, d = x.shape
    t = min(ROW_TILE, seq)
    ch = min(KEY_CHUNK, seq)
    qw = A_HEADS * 2 * A_DH
    o_main = 3 * qw + B_DQLAT + B_DLAT
    wmain = w_in[:, :o_main].astype(BF16)
    wkidx = jnp.tile(w_in[:, o_main:o_main + IDX_DH], (1, IDX_HEADS)).astype(BF16)
    wwidx = w_in[:, o_main + IDX_DH:o_main + IDX_DH + IDX_HEADS].T.astype(BF16)
    wgate = w_in[:, o_main + IDX_DH + IDX_HEADS:].astype(BF16)
    wuq = w_uq.reshape(B_DQLAT, B_HEADS * B_DLAT).astype(BF16)
    wqidx = w_qidx.reshape(B_DQLAT, IDX_HEADS * IDX_DH).astype(BF16)
    row = lambda w: pl.BlockSpec((1, t, w), lambda b, i: (b, i, 0))
    rows = lambda w: (jax.ShapeDtypeStruct((bsz, seq, w), BF16), row(w))
    chunked = lambda w: (jax.ShapeDtypeStruct((bsz, seq // ch, w, ch), BF16),
                         pl.BlockSpec((1, t // ch, w, ch), lambda b, i: (b, i, 0, 0)))
    mod = pl.BlockSpec((1, 1, d), lambda b, i: (b, 0, 0))
    full = lambda a: pl.BlockSpec(a.shape, lambda b, i: (0,) * a.ndim)
    ng = norm_g.reshape(1, d)
    qng = q_norm_g.reshape(1, B_DQLAT)
    kvng = kv_norm_g.reshape(1, B_DLAT)
    iw = IDX_HEADS * IDX_DH
    outs = [rows(qw), rows(qw), rows(qw), chunked(qw), rows(B_HEADS * B_DLAT), rows(B_DLAT), chunked(B_DLAT),
            rows(iw), rows(iw),
            (jax.ShapeDtypeStruct((bsz, IDX_HEADS, seq), F32),
             pl.BlockSpec((1, IDX_HEADS, t), lambda b, i: (b, 0, i))),
            rows(d)]
    return pl.pallas_call(
        _proj0_kernel,
        grid=(bsz, seq // t),
        in_specs=[row(d), mod, mod, full(ng), full(wmain), full(wgate), full(wkidx), full(wwidx),
                  full(qng), full(kvng), full(wuq), full(wqidx)],
        out_specs=[o[1] for o in outs],
        out_shape=[o[0] for o in outs],
        compiler_params=_params(("arbitrary", "arbitrary")),
        name="proj0",
    )(x, shift, scale, ng, wmain, wgate, wkidx, wwidx, qng, kvng, wuq, wqidx)


def _diff_attn_kernel(q1_ref, q2_ref, k_ref, vt_ref, pos_ref, lq1_ref, lk1_ref, lq2_ref, lk2_ref, g_ref, o_ref,
                      m_ref, l_ref, acc_ref, s_ref, p_ref, a_ref, *, lam_init):
    i = pl.program_id(1)
    tq = q1_ref.shape[1]
    tk = vt_ref.shape[3]
    lam = (jnp.exp(jnp.sum(lq1_ref[...] * lk1_ref[...], keepdims=True))
           - jnp.exp(jnp.sum(lq2_ref[...] * lk2_ref[...], keepdims=True)) + lam_init)
    krow = lax.broadcasted_iota(I32, (tk, 2 * tq), 0)
    qcol = lax.broadcasted_iota(I32, (tk, 2 * tq), 1) % tq
    causal = krow <= qcol
    slopes = _alibi(A_HEADS)
    heads = [slice(h * A_DV, (h + 1) * A_DV) for h in range(A_HEADS)]
    qs = [jnp.concatenate([jnp.concatenate([q1_ref[0, :, hs], _alibi_coef(tq, slopes[h])], axis=1),
                           jnp.concatenate([q2_ref[0, :, hs], _alibi_coef(tq, slopes[h])], axis=1)], axis=0)
          for h, hs in enumerate(heads)]

    def scores(j, h):
        kaug = jnp.concatenate([k_ref[0, pl.ds(j * tk, tk), heads[h]], pos_ref[pl.ds(j * tk, tk), :]], axis=1)
        return _nt_dot(kaug, qs[h])

    for h, hs in enumerate(heads):
        s = jnp.where(causal, scores(i, h), NEG_INF)
        m0 = jnp.max(s, axis=0, keepdims=True)
        p = jnp.exp(s - m0)
        m_ref[h] = m0
        l_ref[h] = jnp.sum(p, axis=0, keepdims=True)
        acc_ref[h] = jnp.dot(vt_ref[0, i, hs, :], p.astype(BF16), preferred_element_type=F32)

    def body(j, carry):
        for h in range(A_HEADS):
            s_ref[h] = scores(j, h)
        for h, hs in enumerate(heads):
            for cb in range(2 * tq // LANES):
                cs = slice(cb * LANES, (cb + 1) * LANES)
                s = s_ref[h, :, cs]
                m_prev = m_ref[h, :, cs]
                m_new = jnp.maximum(m_prev, jnp.max(s, axis=0, keepdims=True))
                alpha = jnp.exp(m_prev - m_new)
                p = jnp.exp(s - m_new)
                l_ref[h, :, cs] = alpha * l_ref[h, :, cs] + jnp.sum(p, axis=0, keepdims=True)
                p_ref[h, :, cs] = p.astype(BF16)
                a_ref[h, :, cs] = alpha
                m_ref[h, :, cs] = m_new
            acc_ref[h] = a_ref[h] * acc_ref[h] + jnp.dot(vt_ref[0, j, hs, :], p_ref[h],
                                                         preferred_element_type=F32)
        return carry

    lax.fori_loop(0, i, body, 0)
    for h, hs in enumerate(heads):
        o = acc_ref[h] * (1.0 / l_ref[h])
        od = (o[:, :tq] - lam * o[:, tq:]).T
        o_ref[0, :, hs] = (_rms(od, g_ref[...]) * (1.0 - lam_init)).astype(o_ref.dtype)


def _diff_attn(q1, q2, ka, vat, pos, lam_q1, lam_k1, lam_q2, lam_k2, subln_g, layer_idx):
    bsz, seq, w = q1.shape
    tq = vat.shape[3]
    lam_init = 0.8 - 0.6 * math.exp(-0.3 * layer_idx)
    vec = lambda a: a.reshape(1, -1)
    full = lambda a: pl.BlockSpec(a.shape, lambda b, i: (0,) * a.ndim)
    small = [vec(lam_q1), vec(lam_k1), vec(lam_q2), vec(lam_k2), vec(subln_g)]
    return pl.pallas_call(
        functools.partial(_diff_attn_kernel, lam_init=lam_init),
        grid=(bsz, seq // tq),
        in_specs=[pl.BlockSpec((1, tq, w), lambda b, i: (b, i, 0)),
                  pl.BlockSpec((1, tq, w), lambda b, i: (b, i, 0)),
                  pl.BlockSpec((1, seq, w), lambda b, i: (b, 0, 0)),
                  pl.BlockSpec((1,) + vat.shape[1:], lambda b, i: (b, 0, 0, 0)),
                  full(pos)] + [full(a) for a in small],
        out_specs=pl.BlockSpec((1, tq, w), lambda b, i: (b, i, 0)),
        out_shape=jax.ShapeDtypeStruct((bsz, seq, w), BF16),
        scratch_shapes=[pltpu.VMEM((A_HEADS, 1, 2 * tq), F32), pltpu.VMEM((A_HEADS, 1, 2 * tq), F32),
                        pltpu.VMEM((A_HEADS, A_DV, 2 * tq), F32),
                        pltpu.VMEM((A_HEADS, tq, 2 * tq), F32), pltpu.VMEM((A_HEADS, tq, 2 * tq), BF16),
                        pltpu.VMEM((A_HEADS, 1, 2 * tq), F32)],
        compiler_params=_params(("arbitrary", "arbitrary")),
        name="diff_attn",
    )(q1, q2, ka, vat, pos, *small)


def _dsa_kernel(qlat_ref, qidx_ref, wt_ref, kv_ref, kvt_ref, kidx_ref, pos_ref, wuv_ref, o_ref,
                key_ref, thr_ref, lim_ref, m_ref, l_ref, acc_ref, s_ref, p_ref, a_ref, bias_ref, *, topk):
    qi = pl.program_id(1)
    tq = qlat_ref.shape[1]
    kc_rows = kvt_ref.shape[3]
    nch = (qi * tq) // kc_rows + 1
    qpos = qi * tq + lax.broadcasted_iota(I32, (kc_rows, tq), 1)
    krow = lax.broadcasted_iota(I32, (kc_rows, tq), 0)

    qidx = qidx_ref[0].astype(F32)
    head_of_lane = lax.broadcasted_iota(I32, qidx.shape, 1) // IDX_DH
    qstack = jnp.concatenate([jnp.where(head_of_lane == h, qidx, 0.0) for h in range(IDX_HEADS)],
                             axis=0).astype(BF16)
    wt = wt_ref[0]

    def make_keys(c, carry):
        kx = kidx_ref[0, pl.ds(c * kc_rows, kc_rows), :]
        rel = _nt_dot(kx, qstack)
        isc = jnp.zeros((kc_rows, tq), F32)
        for h in range(IDX_HEADS):
            isc = isc + wt[h:h + 1, :] * jnp.maximum(rel[:, h * tq:(h + 1) * tq], 0.0)
        key_ref[pl.ds(c * kc_rows, kc_rows), :] = jnp.where(c * kc_rows + krow <= qpos, isc, NEG_INF)
        return carry

    lax.fori_loop(0, nch, make_keys, 0)

    def count(pred):
        def body(c, acc):
            blk = key_ref[pl.ds(c * kc_rows, kc_rows), :]
            ind = pred(blk, c * kc_rows + krow).astype(I32)
            return acc + jnp.sum(ind.reshape(kc_rows // SUBLANES, SUBLANES, tq), axis=0)
        acc = lax.fori_loop(0, nch, body, jnp.zeros((SUBLANES, tq), I32))
        return jnp.sum(acc, axis=0, keepdims=True)

    thr_ref[...] = jnp.full((1, tq), NEG_INF, F32)
    lim_ref[...] = jnp.full((1, tq), 2 ** 30, I32)

    def to_float(u):
        s = jnp.maximum(u ^ INT_MIN, ORDER_OF_NEG_INF)
        return pltpu.bitcast(s ^ ((s >> 31) & 0x7FFFFFFF), F32)

    @pl.when((qi + 1) * tq > topk)
    def _():
        def bit_step(t, cand):
            trial = cand | jnp.left_shift(jnp.int32(1), 31 - t)
            thr = to_float(trial)
            cnt = count(lambda x, p: x >= thr)
            return jnp.where(cnt >= topk, trial, cand)
        thr = to_float(lax.fori_loop(0, 32, bit_step, jnp.zeros((1, tq), I32)))
        thr_ref[...] = thr
        need = topk - count(lambda x, p: x > thr)
        n_eq = count(lambda x, p: x == thr)

        @pl.when(jnp.max((n_eq > need).astype(I32)) > 0)
        def _():
            def pos_step(t, lim):
                trial = lim | jnp.left_shift(jnp.int32(1), 11 - t)
                cnt = count(lambda x, p: (x == thr) & (p < trial))
                return jnp.where(cnt < need, trial, lim)
            lim_ref[...] = lax.fori_loop(0, 12, pos_step, jnp.zeros((1, tq), I32))

    thr = thr_ref[...]
    lim = lim_ref[...]
    slopes = _alibi(B_HEADS)
    group = 2
    ngroups = B_HEADS // group
    qs = [jnp.concatenate(
        [jnp.concatenate([qlat_ref[0, :, h * B_DLAT:(h + 1) * B_DLAT], _alibi_coef(tq, slopes[h])], axis=1)
         for h in range(g * group, (g + 1) * group)], axis=0)
          for g in range(ngroups)]
    m_ref[...] = jnp.full(m_ref.shape, NEG_INF, F32)
    l_ref[...] = jnp.zeros(l_ref.shape, F32)
    acc_ref[...] = jnp.zeros(acc_ref.shape, F32)

    def attend(c, carry):
        x = key_ref[pl.ds(c * kc_rows, kc_rows), :]
        kpos = c * kc_rows + krow
        sel = ((x > thr) | ((x == thr) & (kpos <= lim))) & (kpos <= qpos)
        bias_ref[...] = jnp.where(sel, 0.0, NEG_INF)
        kaug = jnp.concatenate([kv_ref[0, pl.ds(c * kc_rows, kc_rows), :],
                                pos_ref[pl.ds(c * kc_rows, kc_rows), :]], axis=1)
        for g in range(ngroups):
            s_ref[g] = _nt_dot(kaug, qs[g])
        for g in range(ngroups):
            for cb in range(group):
                cs = slice(cb * tq, (cb + 1) * tq)
                s = s_ref[g, :, cs] + bias_ref[...]
                m_prev = m_ref[g, :, cs]
                m_new = jnp.maximum(m_prev, jnp.max(s, axis=0, keepdims=True))
                m_safe = jnp.where(m_new == NEG_INF, 0.0, m_new)
                alpha = jnp.exp(m_prev - m_safe)
                p = jnp.exp(s - m_safe)
                l_ref[g, :, cs] = alpha * l_ref[g, :, cs] + jnp.sum(p, axis=0, keepdims=True)
                p_ref[g, :, cs] = p.astype(BF16)
                a_ref[g, :, cs] = alpha
                m_ref[g, :, cs] = m_new
            acc_ref[g] = a_ref[g] * acc_ref[g] + jnp.dot(kvt_ref[0, c], p_ref[g], preferred_element_type=F32)
        return carry

    lax.fori_loop(0, nch, attend, 0)

    o = jnp.concatenate([acc_ref[g] * (1.0 / l_ref[g]) for g in range(ngroups)], axis=1)
    o_all = jnp.concatenate([o[:, h * tq:(h + 1) * tq].T for h in range(B_HEADS)], axis=1).astype(BF16)
    o_ref[0] = jnp.dot(o_all, wuv_ref[...], preferred_element_type=F32).astype(o_ref.dtype)


def _dsa(qlat, qidx, widxt, ckv, ckvt, kidx, pos, w_uv):
    bsz, seq, _ = qlat.shape
    tq = min(B_TQ, seq)
    topk = min(TOPK_MAX, seq // 4)
    eye = jnp.eye(B_HEADS, dtype=w_uv.dtype)
    wuv = jnp.einsum('hde,hg->hdge', w_uv, eye).reshape(B_HEADS * B_DLAT, B_HEADS * B_DV).astype(BF16)
    return pl.pallas_call(
        functools.partial(_dsa_kernel, topk=topk),
        grid=(bsz, seq // tq),
        in_specs=[pl.BlockSpec((1, tq, B_HEADS * B_DLAT), lambda b, i: (b, i, 0)),
                  pl.BlockSpec((1, tq, IDX_HEADS * IDX_DH), lambda b, i: (b, i, 0)),
                  pl.BlockSpec((1, IDX_HEADS, tq), lambda b, i: (b, 0, i)),
                  pl.BlockSpec((1, seq, B_DLAT), lambda b, i: (b, 0, 0)),
                  pl.BlockSpec((1,) + ckvt.shape[1:], lambda b, i: (b, 0, 0, 0)),
                  pl.BlockSpec((1, seq, IDX_HEADS * IDX_DH), lambda b, i: (b, 0, 0)),
                  pl.BlockSpec(pos.shape, lambda b, i: (0, 0)),
                  pl.BlockSpec(wuv.shape, lambda b, i: (0, 0))],
        out_specs=pl.BlockSpec((1, tq, B_HEADS * B_DV), lambda b, i: (b, i, 0)),
        out_shape=jax.ShapeDtypeStruct((bsz, seq, B_HEADS * B_DV), BF16),
        scratch_shapes=[pltpu.VMEM((seq, tq), F32),
                        pltpu.VMEM((1, tq), F32), pltpu.VMEM((1, tq), I32),
                        pltpu.VMEM((B_HEADS // 2, 1, 2 * tq), F32), pltpu.VMEM((B_HEADS // 2, 1, 2 * tq), F32),
                        pltpu.VMEM((B_HEADS // 2, B_DLAT, 2 * tq), F32),
                        pltpu.VMEM((B_HEADS // 2, ckvt.shape[3], 2 * tq), F32),
                        pltpu.VMEM((B_HEADS // 2, ckvt.shape[3], 2 * tq), BF16),
                        pltpu.VMEM((B_HEADS // 2, 1, 2 * tq), F32),
                        pltpu.VMEM((ckvt.shape[3], tq), F32)],
        compiler_params=_params(("arbitrary", "arbitrary")),
        name="dsa",
    )(qlat, qidx, widxt, ckv, ckvt, kidx, pos, wuv)


def _mid_kernel(oa_ref, ob_ref, sg_ref, x_ref, gmod_ref, wout_ref, shift_ref, scale_ref, ng_ref, win_ref,
                x1_ref, qc_ref, kc_ref, vc_ref, hglu_ref, sg1_ref):
    y = jnp.concatenate([oa_ref[0], ob_ref[0]], axis=1).astype(F32) * sg_ref[0].astype(F32)
    r = jnp.dot(y.astype(BF16), wout_ref[...], preferred_element_type=F32)
    x1 = x_ref[0] + gmod_ref[0] * r
    x1_ref[0] = x1
    h = _rms(x1, ng_ref[...]) * (1.0 + scale_ref[0]) + shift_ref[0]
    p = jnp.dot(h.astype(BF16), win_ref[...], preferred_element_type=F32)
    qw = C_HEADS * C_DH
    kw = C_KV_HEADS * C_DH
    qc_ref[0] = (p[:, :qw] * (C_DH ** -0.5)).astype(BF16)
    kc_ref[0] = p[:, qw:qw + kw].astype(BF16)
    vc_ref[0] = p[:, qw + kw:qw + 2 * kw].astype(BF16)
    o = qw + 2 * kw
    hglu_ref[0] = p[:, o:o + D_CH] * jax.nn.sigmoid(p[:, o + D_CH:o + 2 * D_CH])
    gate = p[:, o + 2 * D_CH:]
    sg1_ref[0] = (gate * jax.nn.sigmoid(gate)).astype(BF16)


def _mid(oa, ob, sg, x, gmod, w_out, shift, scale, norm_g, w_in):
    bsz, seq, d = x.shape
    t = min(ROW_TILE, seq)
    wout = w_out.astype(BF16)
    win = w_in.astype(BF16)
    ng = norm_g.reshape(1, d)
    row = lambda w: pl.BlockSpec((1, t, w), lambda b, i: (b, i, 0))
    mod = pl.BlockSpec((1, 1, d), lambda b, i: (b, 0, 0))
    full = lambda a: pl.BlockSpec(a.shape, lambda b, i: (0,) * a.ndim)
    qw = C_HEADS * C_DH
    kw = C_KV_HEADS * C_DH
    outs = [(d, F32), (qw, BF16), (kw, BF16), (kw, BF16), (D_CH, F32), (d, BF16)]
    return pl.pallas_call(
        _mid_kernel,
        grid=(bsz, seq // t),
        in_specs=[row(oa.shape[2]), row(ob.shape[2]), row(d), row(d), mod, full(wout), mod, mod, full(ng),
                  full(win)],
        out_specs=[row(w) for w, _ in outs],
        out_shape=[jax.ShapeDtypeStruct((bsz, seq, w), dt) for w, dt in outs],
        compiler_params=_params(("arbitrary", "arbitrary")),
        name="out0_proj1",
    )(oa, ob, sg, x, gmod, wout, shift, scale, ng, win)


def _swa_kernel(q_ref, kp_ref, kc_ref, vp_ref, vc_ref, sink_ref, o_ref, bias_ref):
    b = pl.program_id(0)
    n = pl.program_id(1)
    w = q_ref.shape[1]
    rep = C_HEADS // C_KV_HEADS
    half = LANES // 2

    @pl.when((b == 0) & (n == 0))
    def _():
        r = lax.broadcasted_iota(I32, (w, 2 * w), 0)
        c = lax.broadcasted_iota(I32, (w, 2 * w), 1)
        dist = w + r - c
        valid = (dist >= 0) & (dist < w)
        slopes = _alibi(C_HEADS)
        for h in range(C_HEADS):
            full = jnp.where(valid, -slopes[h] * dist.astype(F32), NEG_INF)
            bias_ref[1, h * w:(h + 1) * w, :] = full
            bias_ref[0, h * w:(h + 1) * w, :] = jnp.where(c >= w, full, NEG_INF)

    lane = lax.broadcasted_iota(I32, (w, LANES), 1)
    qs = []
    for h in range(C_HEADS):
        g = h // rep
        x = q_ref[0, :, (h // 2) * LANES:(h // 2 + 1) * LANES].astype(F32)
        if (h % 2) != g:
            x = pltpu.roll(x, half, axis=1)
        qs.append(jnp.where((lane // half) == g, x, 0.0))
    qs = jnp.concatenate(qs, axis=0).astype(BF16)
    kcat = jnp.concatenate([kp_ref[0], kc_ref[0]], axis=0)
    vcat = jnp.concatenate([vp_ref[0], vc_ref[0]], axis=0)
    s = _nt_dot(qs, kcat) + bias_ref[jnp.minimum(n, 1)]
    sink = sink_ref[...]
    m = jnp.maximum(jnp.max(s, axis=-1, keepdims=True), sink)
    p = jnp.exp(s - m)
    denom = jnp.sum(p, axis=-1, keepdims=True) + jnp.exp(sink - m)
    o = jnp.dot(p.astype(BF16), vcat, preferred_element_type=F32) * (1.0 / denom)
    for j in range(C_HEADS // 2):
        g = (2 * j) // rep
        a = o[(2 * j) * w:(2 * j + 1) * w]
        bb = o[(2 * j + 1) * w:(2 * j + 2) * w]
        if g == 0:
            bb = pltpu.roll(bb, half, axis=1)
        else:
            a = pltpu.roll(a, half, axis=1)
        o_ref[0, :, j * LANES:(j + 1) * LANES] = jnp.where(lane < half, a, bb).astype(o_ref.dtype)


def _swa(qc, kc, vc, sinks):
    bsz, seq, qw = qc.shape
    w = WINDOW
    kw = kc.shape[2]
    sink_rows = jnp.repeat(sinks.astype(F32), w).reshape(C_HEADS * w, 1)
    cur = lambda width: pl.BlockSpec((1, w, width), lambda b, n: (b, n, 0))
    prev = lambda width: pl.BlockSpec((1, w, width), lambda b, n: (b, jnp.maximum(n - 1, 0), 0))
    return pl.pallas_call(
        _swa_kernel,
        grid=(bsz, seq // w),
        in_specs=[cur(qw), prev(kw), cur(kw), prev(kw), cur(kw),
                  pl.BlockSpec(sink_rows.shape, lambda b, n: (0, 0))],
        out_specs=cur(qw),
        out_shape=jax.ShapeDtypeStruct((bsz, seq, qw), BF16),
        scratch_shapes=[pltpu.VMEM((2, C_HEADS * w, 2 * w), F32)],
        compiler_params=_params(("arbitrary", "arbitrary")),
        name="swa",
    )(qc, kc, kc, vc, vc, sink_rows)


def _conv_kernel(cur_ref, halo_ref, w_ref, b_ref, g_ref, beta_ref, o_ref, buf_ref):
    i = pl.program_id(1)
    t = cur_ref.shape[1]
    halo = halo_ref.shape[1]
    buf_ref[:halo, :] = jnp.where(i > 0, halo_ref[0], 0.0)
    buf_ref[halo:, :] = cur_ref[0]
    acc = jnp.zeros((t, cur_ref.shape[2]), F32) + b_ref[...]
    off = halo - (CONV_W - 1)
    for j in range(CONV_W):
        acc = acc + w_ref[j:j + 1, :] * buf_ref[off + j:off + j + t, :]
    mu = jnp.mean(acc, axis=-1, keepdims=True)
    xc = acc - mu
    var = jnp.mean(xc * xc, axis=-1, keepdims=True)
    y = xc * lax.rsqrt(var + EPS) * g_ref[...] + beta_ref[...]
    o_ref[0] = (y * jax.nn.sigmoid(y)).astype(o_ref.dtype)


def _conv(hglu, dw_w, dw_b, ln_g, ln_b):
    bsz, seq, ch = hglu.shape
    t = min(CONV_TILE, seq)
    per = t // CONV_HALO
    vec = lambda a: a.reshape(1, ch)
    full = lambda a: pl.BlockSpec(a.shape, lambda b, i: (0,) * a.ndim)
    small = [dw_w, vec(dw_b), vec(ln_g), vec(ln_b)]
    return pl.pallas_call(
        _conv_kernel,
        grid=(bsz, seq // t),
        in_specs=[pl.BlockSpec((1, t, ch), lambda b, i: (b, i, 0)),
                  pl.BlockSpec((1, CONV_HALO, ch), lambda b, i: (b, jnp.maximum(i * per - 1, 0), 0))]
                 + [full(a) for a in small],
        out_specs=pl.BlockSpec((1, t, ch), lambda b, i: (b, i, 0)),
        out_shape=jax.ShapeDtypeStruct((bsz, seq, ch), BF16),
        scratch_shapes=[pltpu.VMEM((CONV_HALO + t, ch), F32)],
        compiler_params=_params(("arbitrary", "arbitrary")),
        name="conformer_conv",
    )(hglu, hglu, *small)


def _final_kernel(oc_ref, od_ref, sg_ref, x_ref, gmod_ref, wout_ref, fg_ref, o_ref):
    y = jnp.concatenate([oc_ref[0], od_ref[0]], axis=1).astype(F32) * sg_ref[0].astype(F32)
    r = jnp.dot(y.astype(BF16), wout_ref[...], preferred_element_type=F32)
    x2 = x_ref[0] + gmod_ref[0] * r
    o_ref[0] = _rms(x2, fg_ref[...])


def _final(oc, od, sg, x1, gmod, w_out, final_g):
    bsz, seq, d = x1.shape
    t = min(ROW_TILE, seq)
    wout = w_out.astype(BF16)
    fg = final_g.reshape(1, d)
    row = lambda w: pl.BlockSpec((1, t, w), lambda b, i: (b, i, 0))
    mod = pl.BlockSpec((1, 1, d), lambda b, i: (b, 0, 0))
    full = lambda a: pl.BlockSpec(a.shape, lambda b, i: (0,) * a.ndim)
    return pl.pallas_call(
        _final_kernel,
        grid=(bsz, seq // t),
        in_specs=[row(oc.shape[2]), row(od.shape[2]), row(d), row(d), mod, full(wout), full(fg)],
        out_specs=row(d),
        out_shape=jax.ShapeDtypeStruct((bsz, seq, d), F32),
        compiler_params=_params(("arbitrary", "arbitrary")),
        name="out1_final",
    )(oc, od, sg, x1, gmod, wout, fg)


def kernel(x, c, ada_w, ada_b, norm_g, ab_w_in, ab_q_norm_g, ab_kv_norm_g, ab_w_uq, ab_w_qidx, ab_w_uv,
           ab_lam_q1, ab_lam_k1, ab_lam_q2, ab_lam_k2, ab_subln_g, ab_w_out,
           cd_w_in, cd_sinks, cd_dw_w, cd_dw_b, cd_ln_g, cd_ln_b, cd_w_out, final_g):
    bsz, seq, d = x.shape
    mods = _ada_mods(c, ada_w, ada_b)
    mod = lambda l, k: mods[l, :, k * d:(k + 1) * d].reshape(bsz, 1, d)
    pos = _key_positions(seq)
    q1, q2, ka, vat, qlat, ckv, ckvt, qidx, kidx, widxt, sg0 = _proj0(
        x, mod(0, 0), mod(0, 1), norm_g[0], ab_w_in[0], ab_q_norm_g[0], ab_kv_norm_g[0], ab_w_uq[0],
        ab_w_qidx[0])
    oa = _diff_attn(q1, q2, ka, vat, pos, ab_lam_q1[0], ab_lam_k1[0], ab_lam_q2[0], ab_lam_k2[0],
                    ab_subln_g[0], 0)
    ob = _dsa(qlat, qidx, widxt, ckv, ckvt, kidx, pos, ab_w_uv[0])
    x1, qc, kc, vc, hglu, sg1 = _mid(oa, ob, sg0, x, mod(0, 2), ab_w_out[0], mod(1, 0), mod(1, 1), norm_g[1],
                                     cd_w_in[0])
    oc = _swa(qc, kc, vc, cd_sinks[0])
    od = _conv(hglu, cd_dw_w[0], cd_dw_b[0], cd_ln_g[0], cd_ln_b[0])
    return _final(oc, od, sg1, x1, mod(1, 2), cd_w_out[0], final_g)
```

```python
import functools
import math

import numpy as np
import jax
import jax.numpy as jnp
from jax import lax
from jax.experimental import pallas as pl
from jax.experimental.pallas import tpu as pltpu

F32 = jnp.float32
BF16 = jnp.bfloat16
I32 = jnp.int32

D_MODEL = 1024
EPS = 1e-6
A_HEADS = 4
A_DH = 64
A_DV = 128
B_HEADS = 8
B_DQLAT = 128
B_DLAT = 128
B_DV = 64
IDX_HEADS = 8
IDX_DH = 32
TOPK_MAX = 256
C_HEADS = 8
C_KV_HEADS = 2
C_DH = 64
WINDOW = 128
D_CH = 512
CONV_W = 31

LANES = 128
SUBLANES = 8
INT_MIN = -2 ** 31
ORDER_OF_NEG_INF = INT_MIN + 0x7FFFFF
NEG_INF = float("-inf")
VMEM_LIMIT = 48 * 1024 * 1024

ROW_TILE = 512
KEY_CHUNK = 256
A_TQ = 256
B_TQ = 128
POS_SPLIT = 16
ONES_ROWS = 16
CONV_TILE = 256
CONV_HALO = 32


def _alibi(n):
    return [float(2.0 ** (-8.0 * i / n)) for i in range(1, n + 1)]


def _params(sem):
    return pltpu.CompilerParams(dimension_semantics=sem, vmem_limit_bytes=VMEM_LIMIT)


def _nt_dot(a, b):
    return lax.dot_general(a, b, (((1,), (1,)), ((), ())), preferred_element_type=F32)


def _rms(x, g):
    return x * lax.rsqrt(jnp.mean(x * x, axis=-1, keepdims=True) + EPS) * g


def _alibi_coef(rows, slope):
    lane = lax.broadcasted_iota(I32, (rows, LANES), 1)
    return jnp.where(lane == 0, POS_SPLIT * slope, jnp.where(lane == 1, slope, 0.0)).astype(BF16)


def _key_positions(seq):
    pos = np.zeros((seq, LANES), np.float32)
    pos[:, 0] = np.arange(seq) // POS_SPLIT
    pos[:, 1] = np.arange(seq) % POS_SPLIT
    return jnp.asarray(pos, BF16)


def _ada_kernel(c_ref, w_ref, b_ref, o_ref):
    c = c_ref[...]
    sc = c * jax.nn.sigmoid(c)
    o_ref[0] = jnp.dot(sc.astype(BF16), w_ref[0].astype(BF16), preferred_element_type=F32) + b_ref[0]


def _ada_mods(c, ada_w, ada_b):
    depth, d, d3 = ada_w.shape
    bsz = c.shape[0]
    nt = d3 // d
    return pl.pallas_call(
        _ada_kernel,
        grid=(depth, nt),
        in_specs=[pl.BlockSpec((bsz, d), lambda l, j: (0, 0)),
                  pl.BlockSpec((1, d, d), lambda l, j: (l, 0, j)),
                  pl.BlockSpec((1, 1, d), lambda l, j: (l, 0, j))],
        out_specs=pl.BlockSpec((1, bsz, d), lambda l, j: (l, 0, j)),
        out_shape=jax.ShapeDtypeStruct((depth, bsz, d3), F32),
        compiler_params=_params(("arbitrary", "arbitrary")),
        name="ada_mods",
    )(c, ada_w, ada_b.reshape(depth, 1, d3))


def _proj0_kernel(x_ref, shift_ref, scale_ref, ng_ref, wmain_ref, wgate_ref, wkidx_ref, wwidx_ref,
                  qng_ref, kvng_ref, wuq_ref, wqidx_ref,
                  q1_ref, q2_ref, ka_ref, vat_ref, qlat_ref, ckv_ref, ckvt_ref, qidx_ref, kidx_ref, widxt_ref,
                  sg_ref):
    x = x_ref[0]
    h = _rms(x, ng_ref[...]) * (1.0 + scale_ref[0]) + shift_ref[0]
    hb = h.astype(BF16)
    main = jnp.dot(hb, wmain_ref[...], preferred_element_type=F32)
    qw = A_HEADS * 2 * A_DH
    qa = main[:, :qw] * (A_DH ** -0.5)
    first = (lax.broadcasted_iota(I32, qa.shape, 1) % (2 * A_DH)) < A_DH
    q1_ref[0] = jnp.where(first, qa, 0.0).astype(BF16)
    q2_ref[0] = jnp.where(first, 0.0, qa).astype(BF16)
    ka_ref[0] = main[:, qw:2 * qw].astype(BF16)
    va = main[:, 2 * qw:3 * qw]
    cq = _rms(main[:, 3 * qw:3 * qw + B_DQLAT], qng_ref[...]).astype(BF16)
    ckv = _rms(main[:, 3 * qw + B_DQLAT:], kvng_ref[...])
    ckv_ref[0] = ckv.astype(BF16)
    ch = vat_ref.shape[3]
    ones = jnp.ones((ONES_ROWS, ch), F32)
    for c in range(vat_ref.shape[1]):
        vt = va[c * ch:(c + 1) * ch].T
        vat_ref[0, c] = jnp.concatenate(
            [blk for h in range(A_HEADS) for blk in (vt[h * A_DV:(h + 1) * A_DV], ones)], axis=0).astype(BF16)
        ckvt_ref[0, c] = jnp.concatenate([ckv[c * ch:(c + 1) * ch].T, ones], axis=0).astype(BF16)
    qlat = jnp.dot(cq, wuq_ref[...], preferred_element_type=F32) * (B_DLAT ** -0.5)
    qlat_ref[0] = qlat.astype(BF16)
    qidx_ref[0] = jnp.dot(cq, wqidx_ref[...], preferred_element_type=F32).astype(BF16)
    kidx_ref[0] = jnp.dot(hb, wkidx_ref[...], preferred_element_type=F32).astype(BF16)
    widxt_ref[0] = _nt_dot(wwidx_ref[...], hb)
    gate = jnp.dot(hb, wgate_ref[...], preferred_element_type=F32)
    sg_ref[0] = (gate * jax.nn.sigmoid(gate)).astype(BF16)


def _proj0(x, shift, scale, norm_g, w_in, q_norm_g, kv_norm_g, w_uq, w_qidx):
    bsz, seq, d = x.shape
    t = min(ROW_TILE, seq)
    ch = min(KEY_CHUNK, seq)
    qw = A_HEADS * 2 * A_DH
    o_main = 3 * qw + B_DQLAT + B_DLAT
    wmain = w_in[:, :o_main].astype(BF16)
    wkidx = jnp.tile(w_in[:, o_main:o_main + IDX_DH], (1, IDX_HEADS)).astype(BF16)
    wwidx = w_in[:, o_main + IDX_DH:o_main + IDX_DH + IDX_HEADS].T.astype(BF16)
    wgate = w_in[:, o_main + IDX_DH + IDX_HEADS:].astype(BF16)
    wuq = w_uq.reshape(B_DQLAT, B_HEADS * B_DLAT).astype(BF16)
    wqidx = w_qidx.reshape(B_DQLAT, IDX_HEADS * IDX_DH).astype(BF16)
    row = lambda w: pl.BlockSpec((1, t, w), lambda b, i: (b, i, 0))
    rows = lambda w: (jax.ShapeDtypeStruct((bsz, seq, w), BF16), row(w))
    chunked = lambda w: (jax.ShapeDtypeStruct((bsz, seq // ch, w, ch), BF16),
                         pl.BlockSpec((1, t // ch, w, ch), lambda b, i: (b, i, 0, 0)))
    mod = pl.BlockSpec((1, 1, d), lambda b, i: (b, 0, 0))
    full = lambda a: pl.BlockSpec(a.shape, lambda b, i: (0,) * a.ndim)
    ng = norm_g.reshape(1, d)
    qng = q_norm_g.reshape(1, B_DQLAT)
    kvng = kv_norm_g.reshape(1, B_DLAT)
    iw = IDX_HEADS * IDX_DH
    outs = [rows(qw), rows(qw), rows(qw), chunked(A_HEADS * (A_DV + ONES_ROWS)), rows(B_HEADS * B_DLAT), rows(B_DLAT),
            chunked(B_DLAT + ONES_ROWS),
            rows(iw), rows(iw),
            (jax.ShapeDtypeStruct((bsz, IDX_HEADS, seq), F32),
             pl.BlockSpec((1, IDX_HEADS, t), lambda b, i: (b, 0, i))),
            rows(d)]
    return pl.pallas_call(
        _proj0_kernel,
        grid=(bsz, seq // t),
        in_specs=[row(d), mod, mod, full(ng), full(wmain), full(wgate), full(wkidx), full(wwidx),
                  full(qng), full(kvng), full(wuq), full(wqidx)],
        out_specs=[o[1] for o in outs],
        out_shape=[o[0] for o in outs],
        compiler_params=_params(("arbitrary", "arbitrary")),
        name="proj0",
    )(x, shift, scale, ng, wmain, wgate, wkidx, wwidx, qng, kvng, wuq, wqidx)


def _diff_attn_kernel(q1_ref, q2_ref, k_ref, vt_ref, pos_ref, lq1_ref, lk1_ref, lq2_ref, lk2_ref, g_ref, o_ref,
                      m_ref, a_ref, acc_ref, s0_ref, s1_ref, p_ref, *, lam_init):
    i = pl.program_id(1)
    tq = q1_ref.shape[1]
    tk = vt_ref.shape[3]
    lam = (jnp.exp(jnp.sum(lq1_ref[...] * lk1_ref[...], keepdims=True))
           - jnp.exp(jnp.sum(lq2_ref[...] * lk2_ref[...], keepdims=True)) + lam_init)
    krow = lax.broadcasted_iota(I32, (tk, 2 * tq), 0)
    qcol = lax.broadcasted_iota(I32, (tk, 2 * tq), 1) % tq
    causal = krow <= qcol
    slopes = _alibi(A_HEADS)
    heads = [slice(h * A_DV, (h + 1) * A_DV) for h in range(A_HEADS)]
    qs = [jnp.concatenate([jnp.concatenate([q1_ref[0, :, hs], _alibi_coef(tq, slopes[h])], axis=1),
                           jnp.concatenate([q2_ref[0, :, hs], _alibi_coef(tq, slopes[h])], axis=1)], axis=0)
          for h, hs in enumerate(heads)]

    rows_v = vt_ref.shape[2] // A_HEADS
    vrows = [slice(h * rows_v, (h + 1) * rows_v) for h in range(A_HEADS)]

    def stage(j, dst_ref):
        for h in range(A_HEADS):
            kaug = jnp.concatenate([k_ref[0, pl.ds(j * tk, tk), heads[h]], pos_ref[pl.ds(j * tk, tk), :]], axis=1)
            dst_ref[h] = _nt_dot(kaug, qs[h])

    def consume(j, src_ref, first):
        for h in range(A_HEADS):
            for cb in range(2 * tq // LANES):
                cs = slice(cb * LANES, (cb + 1) * LANES)
                s = src_ref[h, :, cs]
                if first:
                    s = jnp.where(causal[:, cs], s, NEG_INF)
                    m_new = jnp.max(s, axis=0, keepdims=True)
                else:
                    m_prev = m_ref[h, :, cs]
                    m_new = jnp.maximum(m_prev, jnp.max(s, axis=0, keepdims=True))
                    a_ref[h, :, cs] = jnp.exp(m_prev - m_new)
                p_ref[h, :, cs] = jnp.exp(s - m_new).astype(BF16)
                m_ref[h, :, cs] = m_new
            pv = jnp.dot(vt_ref[0, j, vrows[h], :], p_ref[h], preferred_element_type=F32)
            acc_ref[h] = pv if first else a_ref[h] * acc_ref[h] + pv

    chunk_at = lambda t: jnp.where(t == 0, i, jnp.maximum(jnp.minimum(t, i) - 1, 0))
    stage(i, s0_ref)
    stage(chunk_at(1), s1_ref)
    consume(i, s0_ref, True)

    def body(cc, carry):
        t = 1 + 2 * cc
        stage(chunk_at(t + 1), s0_ref)
        consume(t - 1, s1_ref, False)

        @pl.when(t + 1 <= i)
        def _():
            stage(chunk_at(t + 2), s1_ref)
            consume(t, s0_ref, False)
        return carry

    lax.fori_loop(0, (i + 1) // 2, body, 0)
    for h, hs in enumerate(heads):
        acc = acc_ref[h]
        o = acc[:A_DV] * (1.0 / acc[A_DV:A_DV + 1])
        od = (o[:, :tq] - lam * o[:, tq:]).T
        o_ref[0, :, hs] = (_rms(od, g_ref[...]) * (1.0 - lam_init)).astype(o_ref.dtype)


def _diff_attn(q1, q2, ka, vat, pos, lam_q1, lam_k1, lam_q2, lam_k2, subln_g, layer_idx):
    bsz, seq, w = q1.shape
    tq = vat.shape[3]
    lam_init = 0.8 - 0.6 * math.exp(-0.3 * layer_idx)
    vec = lambda a: a.reshape(1, -1)
    full = lambda a: pl.BlockSpec(a.shape, lambda b, i: (0,) * a.ndim)
    small = [vec(lam_q1), vec(lam_k1), vec(lam_q2), vec(lam_k2), vec(subln_g)]
    return pl.pallas_call(
        functools.partial(_diff_attn_kernel, lam_init=lam_init),
        grid=(bsz, seq // tq),
        in_specs=[pl.BlockSpec((1, tq, w), lambda b, i: (b, i, 0)),
                  pl.BlockSpec((1, tq, w), lambda b, i: (b, i, 0)),
                  pl.BlockSpec((1, seq, w), lambda b, i: (b, 0, 0)),
                  pl.BlockSpec((1,) + vat.shape[1:], lambda b, i: (b, 0, 0, 0)),
                  full(pos)] + [full(a) for a in small],
        out_specs=pl.BlockSpec((1, tq, w), lambda b, i: (b, i, 0)),
        out_shape=jax.ShapeDtypeStruct((bsz, seq, w), BF16),
        scratch_shapes=[pltpu.VMEM((A_HEADS, 1, 2 * tq), F32), pltpu.VMEM((A_HEADS, 1, 2 * tq), F32),
                        pltpu.VMEM((A_HEADS, vat.shape[2] // A_HEADS, 2 * tq), F32),
                        pltpu.VMEM((A_HEADS, tq, 2 * tq), F32), pltpu.VMEM((A_HEADS, tq, 2 * tq), F32),
                        pltpu.VMEM((A_HEADS, tq, 2 * tq), BF16)],
        compiler_params=_params(("arbitrary", "arbitrary")),
        name="diff_attn",
    )(q1, q2, ka, vat, pos, *small)


def _dsa_kernel(qlat_ref, qidx_ref, wt_ref, kv_ref, kvt_ref, kidx_ref, pos_ref, wuv_ref, o_ref,
                key_ref, thr_ref, lim_ref, m_ref, a_ref, acc_ref, s0_ref, s1_ref, p_ref, bias_ref, *, topk):
    qi = pl.program_id(1)
    tq = qlat_ref.shape[1]
    kc_rows = kvt_ref.shape[3]
    nch = (qi * tq) // kc_rows + 1
    qpos = qi * tq + lax.broadcasted_iota(I32, (kc_rows, tq), 1)
    krow = lax.broadcasted_iota(I32, (kc_rows, tq), 0)

    qidx = qidx_ref[0].astype(F32)
    head_of_lane = lax.broadcasted_iota(I32, qidx.shape, 1) // IDX_DH
    qstack = jnp.concatenate([jnp.where(head_of_lane == h, qidx, 0.0) for h in range(IDX_HEADS)],
                             axis=0).astype(BF16)
    wt = wt_ref[0]

    def make_keys(c, carry):
        kx = kidx_ref[0, pl.ds(c * kc_rows, kc_rows), :]
        rel = _nt_dot(kx, qstack)
        isc = jnp.zeros((kc_rows, tq), F32)
        for h in range(IDX_HEADS):
            isc = isc + wt[h:h + 1, :] * jnp.maximum(rel[:, h * tq:(h + 1) * tq], 0.0)
        key_ref[pl.ds(c * kc_rows, kc_rows), :] = jnp.where(c * kc_rows + krow <= qpos, isc, NEG_INF)
        return carry

    lax.fori_loop(0, nch, make_keys, 0)

    def count(pred, n):
        accs = [jnp.zeros((SUBLANES, tq), I32) for _ in range(4)]
        for c in range(n):
            for r in range(kc_rows // SUBLANES):
                rows = slice(r * SUBLANES, (r + 1) * SUBLANES)
                ind = pred(key_ref[c * kc_rows + r * SUBLANES:c * kc_rows + (r + 1) * SUBLANES, :],
                           c * kc_rows + krow[rows])
                accs[r % 4] = accs[r % 4] + ind.astype(I32)
        return jnp.sum((accs[0] + accs[1]) + (accs[2] + accs[3]), axis=0, keepdims=True)

    thr_ref[...] = jnp.full((1, tq), NEG_INF, F32)
    lim_ref[...] = jnp.full((1, tq), 2 ** 30, I32)

    def to_float(u):
        s = jnp.maximum(u ^ INT_MIN, ORDER_OF_NEG_INF)
        return pltpu.bitcast(s ^ ((s >> 31) & 0x7FFFFFFF), F32)

    def search(n):
        def bit_step(t, cand):
            trial = cand | jnp.left_shift(jnp.int32(1), 31 - t)
            thr = to_float(trial)
            cnt = count(lambda x, p: x >= thr, n)
            return jnp.where(cnt >= topk, trial, cand)
        thr = to_float(lax.fori_loop(0, 32, bit_step, jnp.zeros((1, tq), I32)))
        thr_ref[...] = thr
        need = topk - count(lambda x, p: x > thr, n)
        n_eq = count(lambda x, p: x == thr, n)

        @pl.when(jnp.max((n_eq > need).astype(I32)) > 0)
        def _():
            def pos_step(t, lim):
                trial = lim | jnp.left_shift(jnp.int32(1), 11 - t)
                cnt = count(lambda x, p: (x == thr) & (p < trial), n)
                return jnp.where(cnt < need, trial, lim)
            lim_ref[...] = lax.fori_loop(0, 12, pos_step, jnp.zeros((1, tq), I32))

    for n in range(1, key_ref.shape[0] // kc_rows + 1):
        if n * kc_rows > topk:
            pl.when(((qi + 1) * tq > topk) & (nch == n))(functools.partial(search, n))

    thr = thr_ref[...]
    lim = lim_ref[...]
    slopes = _alibi(B_HEADS)
    group = 2
    ngroups = B_HEADS // group
    qs = [jnp.concatenate(
        [jnp.concatenate([qlat_ref[0, :, h * B_DLAT:(h + 1) * B_DLAT], _alibi_coef(tq, slopes[h])], axis=1)
         for h in range(g * group, (g + 1) * group)], axis=0)
          for g in range(ngroups)]
    m_ref[...] = jnp.full(m_ref.shape, NEG_INF, F32)
    acc_ref[...] = jnp.zeros(acc_ref.shape, F32)

    def stage(c, dst_ref):
        kaug = jnp.concatenate([kv_ref[0, pl.ds(c * kc_rows, kc_rows), :],
                                pos_ref[pl.ds(c * kc_rows, kc_rows), :]], axis=1)
        for g in range(ngroups):
            dst_ref[g] = _nt_dot(kaug, qs[g])

    def consume(c, src_ref):
        x = key_ref[pl.ds(c * kc_rows, kc_rows), :]
        kpos = c * kc_rows + krow
        sel = ((x > thr) | ((x == thr) & (kpos <= lim))) & (kpos <= qpos)
        bias_ref[...] = jnp.where(sel, 0.0, NEG_INF)
        for g in range(ngroups):
            for cb in range(group):
                cs = slice(cb * tq, (cb + 1) * tq)
                s = src_ref[g, :, cs] + bias_ref[...]
                m_prev = m_ref[g, :, cs]
                m_new = jnp.maximum(m_prev, jnp.max(s, axis=0, keepdims=True))
                m_safe = jnp.where(m_new == NEG_INF, 0.0, m_new)
                a_ref[g, :, cs] = jnp.exp(m_prev - m_safe)
                p_ref[g, :, cs] = jnp.exp(s - m_safe).astype(BF16)
                m_ref[g, :, cs] = m_new
            acc_ref[g] = a_ref[g] * acc_ref[g] + jnp.dot(kvt_ref[0, c], p_ref[g], preferred_element_type=F32)

    stage(0, s0_ref)

    def attend(cc, carry):
        c = 2 * cc
        stage(jnp.minimum(c + 1, nch - 1), s1_ref)
        consume(c, s0_ref)

        @pl.when(c + 1 < nch)
        def _():
            stage(jnp.minimum(c + 2, nch - 1), s0_ref)
            consume(c + 1, s1_ref)
        return carry

    lax.fori_loop(0, (nch + 1) // 2, attend, 0)

    o = jnp.concatenate([acc_ref[g, :B_DLAT] * (1.0 / acc_ref[g, B_DLAT:B_DLAT + 1]) for g in range(ngroups)],
                        axis=1)
    o_all = jnp.concatenate([o[:, h * tq:(h + 1) * tq].T for h in range(B_HEADS)], axis=1).astype(BF16)
    o_ref[0] = jnp.dot(o_all, wuv_ref[...], preferred_element_type=F32).astype(o_ref.dtype)


def _dsa(qlat, qidx, widxt, ckv, ckvt, kidx, pos, w_uv):
    bsz, seq, _ = qlat.shape
    tq = min(B_TQ, seq)
    topk = min(TOPK_MAX, seq // 4)
    eye = jnp.eye(B_HEADS, dtype=w_uv.dtype)
    wuv = jnp.einsum('hde,hg->hdge', w_uv, eye).reshape(B_HEADS * B_DLAT, B_HEADS * B_DV).astype(BF16)
    return pl.pallas_call(
        functools.partial(_dsa_kernel, topk=topk),
        grid=(bsz, seq // tq),
        in_specs=[pl.BlockSpec((1, tq, B_HEADS * B_DLAT), lambda b, i: (b, i, 0)),
                  pl.BlockSpec((1, tq, IDX_HEADS * IDX_DH), lambda b, i: (b, i, 0)),
                  pl.BlockSpec((1, IDX_HEADS, tq), lambda b, i: (b, 0, i)),
                  pl.BlockSpec((1, seq, B_DLAT), lambda b, i: (b, 0, 0)),
                  pl.BlockSpec((1,) + ckvt.shape[1:], lambda b, i: (b, 0, 0, 0)),
                  pl.BlockSpec((1, seq, IDX_HEADS * IDX_DH), lambda b, i: (b, 0, 0)),
                  pl.BlockSpec(pos.shape, lambda b, i: (0, 0)),
                  pl.BlockSpec(wuv.shape, lambda b, i: (0, 0))],
        out_specs=pl.BlockSpec((1, tq, B_HEADS * B_DV), lambda b, i: (b, i, 0)),
        out_shape=jax.ShapeDtypeStruct((bsz, seq, B_HEADS * B_DV), BF16),
        scratch_shapes=[pltpu.VMEM((seq, tq), F32),
                        pltpu.VMEM((1, tq), F32), pltpu.VMEM((1, tq), I32),
                        pltpu.VMEM((B_HEADS // 2, 1, 2 * tq), F32), pltpu.VMEM((B_HEADS // 2, 1, 2 * tq), F32),
                        pltpu.VMEM((B_HEADS // 2, ckvt.shape[2], 2 * tq), F32),
                        pltpu.VMEM((B_HEADS // 2, ckvt.shape[3], 2 * tq), F32),
                        pltpu.VMEM((B_HEADS // 2, ckvt.shape[3], 2 * tq), F32),
                        pltpu.VMEM((B_HEADS // 2, ckvt.shape[3], 2 * tq), BF16),
                        pltpu.VMEM((ckvt.shape[3], tq), F32)],
        compiler_params=_params(("arbitrary", "arbitrary")),
        name="dsa",
    )(qlat, qidx, widxt, ckv, ckvt, kidx, pos, wuv)


def _mid_kernel(oa_ref, ob_ref, sg_ref, x_ref, gmod_ref, wout_ref, shift_ref, scale_ref, ng_ref, win_ref,
                x1_ref, qc_ref, kc_ref, vct_ref, hglu_ref, sg1_ref):
    y = jnp.concatenate([oa_ref[0], ob_ref[0]], axis=1).astype(F32) * sg_ref[0].astype(F32)
    r = jnp.dot(y.astype(BF16), wout_ref[...], preferred_element_type=F32)
    x1 = x_ref[0] + gmod_ref[0] * r
    x1_ref[0] = x1
    h = _rms(x1, ng_ref[...]) * (1.0 + scale_ref[0]) + shift_ref[0]
    p = jnp.dot(h.astype(BF16), win_ref[...], preferred_element_type=F32)
    qw = C_HEADS * C_DH
    kw = C_KV_HEADS * C_DH
    qc_ref[0] = (p[:, :qw] * (C_DH ** -0.5)).astype(BF16)
    kc_ref[0] = p[:, qw:qw + kw].astype(BF16)
    vct_ref[0] = p[:, qw + kw:qw + 2 * kw].T.astype(BF16)
    o = qw + 2 * kw
    hglu_ref[0] = p[:, o:o + D_CH] * jax.nn.sigmoid(p[:, o + D_CH:o + 2 * D_CH])
    gate = p[:, o + 2 * D_CH:]
    sg1_ref[0] = (gate * jax.nn.sigmoid(gate)).astype(BF16)


def _mid(oa, ob, sg, x, gmod, w_out, shift, scale, norm_g, w_in):
    bsz, seq, d = x.shape
    t = min(ROW_TILE, seq)
    wout = w_out.astype(BF16)
    win = w_in.astype(BF16)
    ng = norm_g.reshape(1, d)
    row = lambda w: pl.BlockSpec((1, t, w), lambda b, i: (b, i, 0))
    mod = pl.BlockSpec((1, 1, d), lambda b, i: (b, 0, 0))
    full = lambda a: pl.BlockSpec(a.shape, lambda b, i: (0,) * a.ndim)
    qw = C_HEADS * C_DH
    kw = C_KV_HEADS * C_DH
    rows = lambda w, dt: (jax.ShapeDtypeStruct((bsz, seq, w), dt), row(w))
    outs = [rows(d, F32), rows(qw, BF16), rows(kw, BF16),
            (jax.ShapeDtypeStruct((bsz, kw, seq), BF16), pl.BlockSpec((1, kw, t), lambda b, i: (b, 0, i))),
            rows(D_CH, F32), rows(d, BF16)]
    return pl.pallas_call(
        _mid_kernel,
        grid=(bsz, seq // t),
        in_specs=[row(oa.shape[2]), row(ob.shape[2]), row(d), row(d), mod, full(wout), mod, mod, full(ng),
                  full(win)],
        out_specs=[o[1] for o in outs],
        out_shape=[o[0] for o in outs],
        compiler_params=_params(("arbitrary", "arbitrary")),
        name="out0_proj1",
    )(oa, ob, sg, x, gmod, wout, shift, scale, ng, win)


def _swa_kernel(q_ref, kp_ref, kc_ref, vtp_ref, vtc_ref, pos_ref, sink_ref, o_ref, mask_ref, s_ref, p_ref, r_ref):
    b = pl.program_id(0)
    n = pl.program_id(1)
    w = q_ref.shape[1]
    rep = C_HEADS // C_KV_HEADS
    half = LANES // 2
    group = 2

    @pl.when((b == 0) & (n == 0))
    def _():
        k = lax.broadcasted_iota(I32, (2 * w, w), 0)
        q = lax.broadcasted_iota(I32, (2 * w, w), 1)
        dist = w + q - k
        valid = (dist >= 0) & (dist < w)
        mask_ref[1] = jnp.where(valid, 0.0, NEG_INF)
        mask_ref[0] = jnp.where(valid & (k >= w), 0.0, NEG_INF)

    lane = lax.broadcasted_iota(I32, (w, LANES), 1)
    slopes = _alibi(C_HEADS)
    qs = []
    for h in range(C_HEADS):
        g = h // rep
        x = q_ref[0, :, (h // 2) * LANES:(h // 2 + 1) * LANES].astype(F32)
        if (h % 2) != g:
            x = pltpu.roll(x, half, axis=1)
        x = jnp.where((lane // half) == g, x, 0.0).astype(BF16)
        qs.append(jnp.concatenate([x, _alibi_coef(w, slopes[h])], axis=1))
    kaug = jnp.concatenate([jnp.concatenate([kp_ref[0], kc_ref[0]], axis=0), pos_ref[...]], axis=1)
    vt = jnp.concatenate([vtp_ref[0], vtc_ref[0]], axis=1)
    mask = mask_ref[jnp.minimum(n, 1)]
    for g in range(C_HEADS // group):
        s_ref[g] = _nt_dot(kaug, jnp.concatenate(qs[g * group:(g + 1) * group], axis=0))
    ot = []
    for g in range(C_HEADS // group):
        for cb in range(group):
            h = g * group + cb
            cs = slice(cb * w, (cb + 1) * w)
            s = s_ref[g, :, cs] + mask
            sink = sink_ref[h:h + 1, :] + slopes[h] * (w + lane[:1, :]).astype(F32)
            m = jnp.maximum(jnp.max(s, axis=0, keepdims=True), sink)
            p = jnp.exp(s - m)
            r_ref[g, :, cs] = 1.0 / (jnp.sum(p, axis=0, keepdims=True) + jnp.exp(sink - m))
            p_ref[g, :, cs] = p.astype(BF16)
        o = jnp.dot(vt, p_ref[g], preferred_element_type=F32) * r_ref[g]
        for cb in range(group):
            kvg = (g * group + cb) // rep
            ot.append(o[kvg * C_DH:(kvg + 1) * C_DH, cb * w:(cb + 1) * w])
    o_ref[0] = jnp.concatenate(ot, axis=0).T.astype(o_ref.dtype)


def _swa(qc, kc, vct, pos, sinks):
    bsz, seq, qw = qc.shape
    w = WINDOW
    kw = kc.shape[2]
    sink_rows = jnp.broadcast_to(sinks.astype(F32)[:, None], (C_HEADS, w))
    cur = lambda width: pl.BlockSpec((1, w, width), lambda b, n: (b, n, 0))
    prev = lambda width: pl.BlockSpec((1, w, width), lambda b, n: (b, jnp.maximum(n - 1, 0), 0))
    return pl.pallas_call(
        _swa_kernel,
        grid=(bsz, seq // w),
        in_specs=[cur(qw), prev(kw), cur(kw),
                  pl.BlockSpec((1, kw, w), lambda b, n: (b, 0, jnp.maximum(n - 1, 0))),
                  pl.BlockSpec((1, kw, w), lambda b, n: (b, 0, n)),
                  pl.BlockSpec((2 * w, LANES), lambda b, n: (0, 0)),
                  pl.BlockSpec(sink_rows.shape, lambda b, n: (0, 0))],
        out_specs=cur(qw),
        out_shape=jax.ShapeDtypeStruct((bsz, seq, qw), BF16),
        scratch_shapes=[pltpu.VMEM((2, 2 * w, w), F32),
                        pltpu.VMEM((C_HEADS // 2, 2 * w, 2 * w), F32),
                        pltpu.VMEM((C_HEADS // 2, 2 * w, 2 * w), BF16),
                        pltpu.VMEM((C_HEADS // 2, 1, 2 * w), F32)],
        compiler_params=_params(("arbitrary", "arbitrary")),
        name="swa",
    )(qc, kc, kc, vct, vct, pos, sink_rows)


def _conv_kernel(cur_ref, halo_ref, w_ref, b_ref, g_ref, beta_ref, o_ref, buf_ref):
    i = pl.program_id(1)
    t = cur_ref.shape[1]
    halo = halo_ref.shape[1]
    buf_ref[:halo, :] = jnp.where(i > 0, halo_ref[0], 0.0)
    buf_ref[halo:, :] = cur_ref[0]
    acc = jnp.zeros((t, cur_ref.shape[2]), F32) + b_ref[...]
    off = halo - (CONV_W - 1)
    for j in range(CONV_W):
        acc = acc + w_ref[j:j + 1, :] * buf_ref[off + j:off + j + t, :]
    mu = jnp.mean(acc, axis=-1, keepdims=True)
    xc = acc - mu
    var = jnp.mean(xc * xc, axis=-1, keepdims=True)
    y = xc * lax.rsqrt(var + EPS) * g_ref[...] + beta_ref[...]
    o_ref[0] = (y * jax.nn.sigmoid(y)).astype(o_ref.dtype)


def _conv(hglu, dw_w, dw_b, ln_g, ln_b):
    bsz, seq, ch = hglu.shape
    t = min(CONV_TILE, seq)
    per = t // CONV_HALO
    vec = lambda a: a.reshape(1, ch)
    full = lambda a: pl.BlockSpec(a.shape, lambda b, i: (0,) * a.ndim)
    small = [dw_w, vec(dw_b), vec(ln_g), vec(ln_b)]
    return pl.pallas_call(
        _conv_kernel,
        grid=(bsz, seq // t),
        in_specs=[pl.BlockSpec((1, t, ch), lambda b, i: (b, i, 0)),
                  pl.BlockSpec((1, CONV_HALO, ch), lambda b, i: (b, jnp.maximum(i * per - 1, 0), 0))]
                 + [full(a) for a in small],
        out_specs=pl.BlockSpec((1, t, ch), lambda b, i: (b, i, 0)),
        out_shape=jax.ShapeDtypeStruct((bsz, seq, ch), BF16),
        scratch_shapes=[pltpu.VMEM((CONV_HALO + t, ch), F32)],
        compiler_params=_params(("arbitrary", "arbitrary")),
        name="conformer_conv",
    )(hglu, hglu, *small)


def _final_kernel(oc_ref, od_ref, sg_ref, x_ref, gmod_ref, wout_ref, fg_ref, o_ref):
    y = jnp.concatenate([oc_ref[0], od_ref[0]], axis=1).astype(F32) * sg_ref[0].astype(F32)
    r = jnp.dot(y.astype(BF16), wout_ref[...], preferred_element_type=F32)
    x2 = x_ref[0] + gmod_ref[0] * r
    o_ref[0] = _rms(x2, fg_ref[...])


def _final(oc, od, sg, x1, gmod, w_out, final_g):
    bsz, seq, d = x1.shape
    t = min(ROW_TILE, seq)
    wout = w_out.astype(BF16)
    fg = final_g.reshape(1, d)
    row = lambda w: pl.BlockSpec((1, t, w), lambda b, i: (b, i, 0))
    mod = pl.BlockSpec((1, 1, d), lambda b, i: (b, 0, 0))
    full = lambda a: pl.BlockSpec(a.shape, lambda b, i: (0,) * a.ndim)
    return pl.pallas_call(
        _final_kernel,
        grid=(bsz, seq // t),
        in_specs=[row(oc.shape[2]), row(od.shape[2]), row(d), row(d), mod, full(wout), full(fg)],
        out_specs=row(d),
        out_shape=jax.ShapeDtypeStruct((bsz, seq, d), F32),
        compiler_params=_params(("arbitrary", "arbitrary")),
        name="out1_final",
    )(oc, od, sg, x1, gmod, wout, fg)


def kernel(x, c, ada_w, ada_b, norm_g, ab_w_in, ab_q_norm_g, ab_kv_norm_g, ab_w_uq, ab_w_qidx, ab_w_uv,
           ab_lam_q1, ab_lam_k1, ab_lam_q2, ab_lam_k2, ab_subln_g, ab_w_out,
           cd_w_in, cd_sinks, cd_dw_w, cd_dw_b, cd_ln_g, cd_ln_b, cd_w_out, final_g):
    bsz, seq, d = x.shape
    mods = _ada_mods(c, ada_w, ada_b)
    mod = lambda l, k: mods[l, :, k * d:(k + 1) * d].reshape(bsz, 1, d)
    pos = _key_positions(seq)
    q1, q2, ka, vat, qlat, ckv, ckvt, qidx, kidx, widxt, sg0 = _proj0(
        x, mod(0, 0), mod(0, 1), norm_g[0], ab_w_in[0], ab_q_norm_g[0], ab_kv_norm_g[0], ab_w_uq[0],
        ab_w_qidx[0])
    oa = _diff_attn(q1, q2, ka, vat, pos, ab_lam_q1[0], ab_lam_k1[0], ab_lam_q2[0], ab_lam_k2[0],
                    ab_subln_g[0], 0)
    ob = _dsa(qlat, qidx, widxt, ckv, ckvt, kidx, pos, ab_w_uv[0])
    x1, qc, kc, vct, hglu, sg1 = _mid(oa, ob, sg0, x, mod(0, 2), ab_w_out[0], mod(1, 0), mod(1, 1), norm_g[1],
                                     cd_w_in[0])
    oc = _swa(qc, kc, vct, pos, cd_sinks[0])
    od = _conv(hglu, cd_dw_w[0], cd_dw_b[0], cd_ln_g[0], cd_ln_b[0])
    return _final(oc, od, sg1, x1, mod(1, 2), cd_w_out[0], final_g)
```

```python
import functools
import math

import numpy as np
import jax
import jax.numpy as jnp
from jax import lax
from jax.experimental import pallas as pl
from jax.experimental.pallas import tpu as pltpu

F32 = jnp.float32
BF16 = jnp.bfloat16
I32 = jnp.int32

D_MODEL = 1024
EPS = 1e-6
A_HEADS = 4
A_DH = 64
A_DV = 128
B_HEADS = 8
B_DQLAT = 128
B_DLAT = 128
B_DV = 64
IDX_HEADS = 8
IDX_DH = 32
TOPK_MAX = 256
C_HEADS = 8
C_KV_HEADS = 2
C_DH = 64
WINDOW = 128
D_CH = 512
CONV_W = 31

LANES = 128
SUBLANES = 8
INT_MIN = -2 ** 31
ORDER_OF_NEG_INF = INT_MIN + 0x7FFFFF
NEG_INF = float("-inf")
VMEM_LIMIT = 48 * 1024 * 1024

ROW_TILE = 512
KEY_CHUNK = 256
A_TQ = 256
B_TQ = 128
POS_SPLIT = 16
ONES_ROWS = 16
LOG2E = math.log2(math.e)
COEF_TERMS = 4
CONV_TILE = 256
CONV_HALO = 32


def _alibi(n):
    return [float(2.0 ** (-8.0 * i / n)) for i in range(1, n + 1)]


def _params(sem):
    return pltpu.CompilerParams(dimension_semantics=sem, vmem_limit_bytes=VMEM_LIMIT)


def _nt_dot(a, b):
    return lax.dot_general(a, b, (((1,), (1,)), ((), ())), preferred_element_type=F32)


def _dot(a, b):
    return jnp.dot(a, b, preferred_element_type=F32)


def _transposed(x):
    return x.astype(F32).T.astype(BF16)


def _rms(x, g):
    return x * lax.rsqrt(jnp.mean(x * x, axis=-1, keepdims=True) + EPS) * g


def _bf16_terms(x, n):
    terms = []
    for _ in range(n):
        t = float(np.asarray(x, np.float32).astype(jnp.bfloat16).astype(np.float32))
        terms.append(t)
        x -= t
    return terms


def _alibi_coef(rows, slope):
    lane = lax.broadcasted_iota(I32, (rows, LANES), 1)
    out = jnp.zeros((rows, LANES), F32)
    for i, t in enumerate(_bf16_terms(LOG2E * slope, COEF_TERMS)):
        out = jnp.where(lane == 2 * i, POS_SPLIT * t, jnp.where(lane == 2 * i + 1, t, out))
    return out.astype(BF16)


def _key_positions(seq):
    pos = np.zeros((seq, LANES), np.float32)
    for i in range(COEF_TERMS):
        pos[:, 2 * i] = np.arange(seq) // POS_SPLIT
        pos[:, 2 * i + 1] = np.arange(seq) % POS_SPLIT
    return jnp.asarray(pos, BF16)


def _ada_kernel(c_ref, w_ref, b_ref, o_ref):
    c = c_ref[...]
    sc = c * jax.nn.sigmoid(c)
    o_ref[0] = jnp.dot(sc.astype(BF16), w_ref[0].astype(BF16), preferred_element_type=F32) + b_ref[0]


def _ada_mods(c, ada_w, ada_b):
    depth, d, d3 = ada_w.shape
    bsz = c.shape[0]
    nt = d3 // d
    return pl.pallas_call(
        _ada_kernel,
        grid=(depth, nt),
        in_specs=[pl.BlockSpec((bsz, d), lambda l, j: (0, 0)),
                  pl.BlockSpec((1, d, d), lambda l, j: (l, 0, j)),
                  pl.BlockSpec((1, 1, d), lambda l, j: (l, 0, j))],
        out_specs=pl.BlockSpec((1, bsz, d), lambda l, j: (l, 0, j)),
        out_shape=jax.ShapeDtypeStruct((depth, bsz, d3), F32),
        compiler_params=_params(("arbitrary", "arbitrary")),
        name="ada_mods",
    )(c, ada_w, ada_b.reshape(depth, 1, d3))


def _proj0_kernel(x_ref, shift_ref, scale_ref, ng_ref, wmain_ref, wgate_ref, wkidx_ref, wwidx_ref,
                  qng_ref, kvng_ref, wuq_ref, wqidx_ref,
                  q1_ref, q2_ref, ka_ref, vat_ref, qlat_ref, ckv_ref, ckvt_ref, qidx_ref, kidx_ref, widxt_ref,
                  sg_ref):
    x = x_ref[0]
    h = _rms(x, ng_ref[...]) * (1.0 + scale_ref[0]) + shift_ref[0]
    hb = h.astype(BF16)
    main = jnp.dot(hb, wmain_ref[...], preferred_element_type=F32)
    qw = A_HEADS * 2 * A_DH
    qa = main[:, :qw] * (A_DH ** -0.5 * LOG2E)
    first = (lax.broadcasted_iota(I32, qa.shape, 1) % (2 * A_DH)) < A_DH
    q1_ref[0] = jnp.where(first, qa, 0.0).astype(BF16)
    q2_ref[0] = jnp.where(first, 0.0, qa).astype(BF16)
    ka_ref[0] = main[:, qw:2 * qw].astype(BF16)
    va = main[:, 2 * qw:3 * qw]
    cq = _rms(main[:, 3 * qw:3 * qw + B_DQLAT], qng_ref[...]).astype(BF16)
    ckv = _rms(main[:, 3 * qw + B_DQLAT:], kvng_ref[...])
    ckv_ref[0] = ckv.astype(BF16)
    ch = vat_ref.shape[3]
    ones = jnp.ones((ONES_ROWS, ch), F32)
    for c in range(vat_ref.shape[1]):
        vt = va[c * ch:(c + 1) * ch].T
        vat_ref[0, c] = jnp.concatenate(
            [blk for h in range(A_HEADS) for blk in (vt[h * A_DV:(h + 1) * A_DV], ones)], axis=0).astype(BF16)
        ckvt_ref[0, c] = jnp.concatenate([ckv[c * ch:(c + 1) * ch].T, ones], axis=0).astype(BF16)
    qlat = jnp.dot(cq, wuq_ref[...], preferred_element_type=F32) * (B_DLAT ** -0.5 * LOG2E)
    qlat_ref[0] = qlat.astype(BF16)
    qidx_ref[0] = jnp.dot(cq, wqidx_ref[...], preferred_element_type=F32).astype(BF16)
    kidx_ref[0] = jnp.dot(hb, wkidx_ref[...], preferred_element_type=F32).astype(BF16)
    widxt_ref[0] = _nt_dot(wwidx_ref[...], hb)
    gate = jnp.dot(hb, wgate_ref[...], preferred_element_type=F32)
    sg_ref[0] = (gate * jax.nn.sigmoid(gate)).astype(BF16)


def _proj0(x, shift, scale, norm_g, w_in, q_norm_g, kv_norm_g, w_uq, w_qidx):
    bsz, seq, d = x.shape
    t = min(ROW_TILE, seq)
    ch = min(KEY_CHUNK, seq)
    qw = A_HEADS * 2 * A_DH
    o_main = 3 * qw + B_DQLAT + B_DLAT
    wmain = w_in[:, :o_main].astype(BF16)
    wkidx = jnp.tile(w_in[:, o_main:o_main + IDX_DH], (1, IDX_HEADS)).astype(BF16)
    wwidx = w_in[:, o_main + IDX_DH:o_main + IDX_DH + IDX_HEADS].T.astype(BF16)
    wgate = w_in[:, o_main + IDX_DH + IDX_HEADS:].astype(BF16)
    wuq = w_uq.reshape(B_DQLAT, B_HEADS * B_DLAT).astype(BF16)
    wqidx = w_qidx.reshape(B_DQLAT, IDX_HEADS * IDX_DH).astype(BF16)
    row = lambda w: pl.BlockSpec((1, t, w), lambda b, i: (b, i, 0))
    rows = lambda w: (jax.ShapeDtypeStruct((bsz, seq, w), BF16), row(w))
    chunked = lambda w: (jax.ShapeDtypeStruct((bsz, seq // ch, w, ch), BF16),
                         pl.BlockSpec((1, t // ch, w, ch), lambda b, i: (b, i, 0, 0)))
    mod = pl.BlockSpec((1, 1, d), lambda b, i: (b, 0, 0))
    full = lambda a: pl.BlockSpec(a.shape, lambda b, i: (0,) * a.ndim)
    ng = norm_g.reshape(1, d)
    qng = q_norm_g.reshape(1, B_DQLAT)
    kvng = kv_norm_g.reshape(1, B_DLAT)
    iw = IDX_HEADS * IDX_DH
    outs = [rows(qw), rows(qw), rows(qw), chunked(A_HEADS * (A_DV + ONES_ROWS)), rows(B_HEADS * B_DLAT), rows(B_DLAT),
            chunked(B_DLAT + ONES_ROWS),
            rows(iw), rows(iw),
            (jax.ShapeDtypeStruct((bsz, IDX_HEADS, seq), F32),
             pl.BlockSpec((1, IDX_HEADS, t), lambda b, i: (b, 0, i))),
            rows(d)]
    return pl.pallas_call(
        _proj0_kernel,
        grid=(bsz, seq // t),
        in_specs=[row(d), mod, mod, full(ng), full(wmain), full(wgate), full(wkidx), full(wwidx),
                  full(qng), full(kvng), full(wuq), full(wqidx)],
        out_specs=[o[1] for o in outs],
        out_shape=[o[0] for o in outs],
        compiler_params=_params(("arbitrary", "arbitrary")),
        name="proj0",
    )(x, shift, scale, ng, wmain, wgate, wkidx, wwidx, qng, kvng, wuq, wqidx)


def _diff_attn_kernel(q1_ref, q2_ref, k_ref, vt_ref, pos_ref, lq1_ref, lk1_ref, lq2_ref, lk2_ref, g_ref, o_ref,
                      *, lam_init):
    i = pl.program_id(1)
    tq = q1_ref.shape[1]
    tk = vt_ref.shape[3]
    lam = (jnp.exp(jnp.sum(lq1_ref[...] * lk1_ref[...], keepdims=True))
           - jnp.exp(jnp.sum(lq2_ref[...] * lk2_ref[...], keepdims=True)) + lam_init)
    krow = lax.broadcasted_iota(I32, (tk, 2 * tq), 0)
    qcol = lax.broadcasted_iota(I32, (tk, 2 * tq), 1) % tq
    causal = krow <= qcol
    slopes = _alibi(A_HEADS)
    heads = [slice(h * A_DV, (h + 1) * A_DV) for h in range(A_HEADS)]
    qs = [_transposed(jnp.concatenate(
        [jnp.concatenate([q1_ref[0, :, hs], _alibi_coef(tq, slopes[h])], axis=1),
         jnp.concatenate([q2_ref[0, :, hs], _alibi_coef(tq, slopes[h])], axis=1)], axis=0))
          for h, hs in enumerate(heads)]

    rows_v = vt_ref.shape[2] // A_HEADS
    vrows = [slice(h * rows_v, (h + 1) * rows_v) for h in range(A_HEADS)]

    def variant(n):
        rows = n * tk
        pos = pos_ref[:rows, :]
        for h, hs in enumerate(heads):
            s = _dot(jnp.concatenate([k_ref[0, :rows, hs], pos], axis=1), qs[h])
            last = jnp.where(causal, s[rows - tk:], NEG_INF)
            s = last if n == 1 else jnp.concatenate([s[:rows - tk], last], axis=0)
            m = jnp.max(s, axis=0, keepdims=True)
            vt = jnp.concatenate([vt_ref[0, c, vrows[h], :] for c in range(n)], axis=1)
            acc = _dot(vt, jnp.exp2(s - m).astype(BF16))
            o = acc[:A_DV] * (1.0 / acc[A_DV:A_DV + 1])
            od = (o[:, :tq] - lam * o[:, tq:]).T
            o_ref[0, :, hs] = (_rms(od, g_ref[...]) * (1.0 - lam_init)).astype(o_ref.dtype)

    for n in range(1, vt_ref.shape[1] + 1):
        pl.when(i + 1 == n)(functools.partial(variant, n))


def _diff_attn(q1, q2, ka, vat, pos, lam_q1, lam_k1, lam_q2, lam_k2, subln_g, layer_idx):
    bsz, seq, w = q1.shape
    tq = vat.shape[3]
    lam_init = 0.8 - 0.6 * math.exp(-0.3 * layer_idx)
    vec = lambda a: a.reshape(1, -1)
    full = lambda a: pl.BlockSpec(a.shape, lambda b, i: (0,) * a.ndim)
    small = [vec(lam_q1), vec(lam_k1), vec(lam_q2), vec(lam_k2), vec(subln_g)]
    return pl.pallas_call(
        functools.partial(_diff_attn_kernel, lam_init=lam_init),
        grid=(bsz, seq // tq),
        in_specs=[pl.BlockSpec((1, tq, w), lambda b, i: (b, i, 0)),
                  pl.BlockSpec((1, tq, w), lambda b, i: (b, i, 0)),
                  pl.BlockSpec((1, seq, w), lambda b, i: (b, 0, 0)),
                  pl.BlockSpec((1,) + vat.shape[1:], lambda b, i: (b, 0, 0, 0)),
                  full(pos)] + [full(a) for a in small],
        out_specs=pl.BlockSpec((1, tq, w), lambda b, i: (b, i, 0)),
        out_shape=jax.ShapeDtypeStruct((bsz, seq, w), BF16),
        compiler_params=_params(("arbitrary", "arbitrary")),
        name="diff_attn",
    )(q1, q2, ka, vat, pos, *small)


def _dsa_kernel(qlat_ref, qidx_ref, wt_ref, kv_ref, kvt_ref, kidx_ref, pos_ref, wuv_ref, o_ref,
                key_ref, thr_ref, lim_ref, *, topk):
    qi = pl.program_id(1)
    tq = qlat_ref.shape[1]
    kc_rows = kvt_ref.shape[3]
    nch = (qi * tq) // kc_rows + 1

    qidx = qidx_ref[0].astype(F32)
    head_of_lane = lax.broadcasted_iota(I32, qidx.shape, 1) // IDX_DH
    qstack = jnp.concatenate([jnp.where(head_of_lane == h, qidx, 0.0) for h in range(IDX_HEADS)],
                             axis=0).T.astype(BF16)
    wt = wt_ref[0]
    slopes = _alibi(B_HEADS)
    group = 2
    ngroups = B_HEADS // group
    qs = [_transposed(jnp.concatenate(
        [jnp.concatenate([qlat_ref[0, :, h * B_DLAT:(h + 1) * B_DLAT], _alibi_coef(tq, slopes[h])], axis=1)
         for h in range(g * group, (g + 1) * group)], axis=0))
          for g in range(ngroups)]

    def to_float(u):
        s = jnp.maximum(u ^ INT_MIN, ORDER_OF_NEG_INF)
        return pltpu.bitcast(s ^ ((s >> 31) & 0x7FFFFFFF), F32)

    def variant(n):
        rows = n * kc_rows
        kpos = lax.broadcasted_iota(I32, (rows, tq), 0)
        causal = kpos <= qi * tq + lax.broadcasted_iota(I32, (rows, tq), 1)
        krow = kpos[:SUBLANES]

        rel = _dot(kidx_ref[0, :rows, :], qstack)
        isc = jnp.zeros((rows, tq), F32)
        for h in range(IDX_HEADS):
            isc = isc + wt[h:h + 1, :] * jnp.maximum(rel[:, h * tq:(h + 1) * tq], 0.0)
        key_ref[:rows, :] = jnp.where(causal, isc, NEG_INF)

        def count(pred):
            accs = [jnp.zeros((SUBLANES, tq), I32) for _ in range(4)]
            for r in range(rows // SUBLANES):
                ind = pred(key_ref[r * SUBLANES:(r + 1) * SUBLANES, :], r * SUBLANES + krow)
                accs[r % 4] = accs[r % 4] + ind.astype(I32)
            return jnp.sum((accs[0] + accs[1]) + (accs[2] + accs[3]), axis=0, keepdims=True)

        thr_ref[...] = jnp.full((1, tq), NEG_INF, F32)
        lim_ref[...] = jnp.full((1, tq), 2 ** 30, I32)
        if rows > topk:
            def bit_step(t, cand):
                trial = cand | jnp.left_shift(jnp.int32(1), 31 - t)
                thr = to_float(trial)
                cnt = count(lambda x, p: x >= thr)
                return jnp.where(cnt >= topk, trial, cand)
            thr = to_float(lax.fori_loop(0, 32, bit_step, jnp.zeros((1, tq), I32)))
            thr_ref[...] = thr
            need = topk - count(lambda x, p: x > thr)
            n_eq = count(lambda x, p: x == thr)

            @pl.when(jnp.max((n_eq > need).astype(I32)) > 0)
            def _():
                def pos_step(t, lim):
                    trial = lim | jnp.left_shift(jnp.int32(1), 11 - t)
                    cnt = count(lambda x, p: (x == thr) & (p < trial))
                    return jnp.where(cnt < need, trial, lim)
                lim_ref[...] = lax.fori_loop(0, 12, pos_step, jnp.zeros((1, tq), I32))

        thr = thr_ref[...]
        lim = lim_ref[...]
        x = key_ref[:rows, :]
        sel = ((x > thr) | ((x == thr) & (kpos <= lim))) & causal
        bias = jnp.where(sel, 0.0, NEG_INF)
        bias = jnp.concatenate([bias] * group, axis=1)
        kaug = jnp.concatenate([kv_ref[0, :rows, :], pos_ref[:rows, :]], axis=1)
        kvt = jnp.concatenate([kvt_ref[0, c] for c in range(n)], axis=1)
        s_all = _dot(kaug, jnp.concatenate(qs, axis=1))
        ps = []
        for g in range(ngroups):
            s = s_all[:, g * group * tq:(g + 1) * group * tq] + bias
            m = jnp.max(s, axis=0, keepdims=True)
            ps.append(jnp.exp2(s - m).astype(BF16))
        acc = _dot(kvt, jnp.concatenate(ps, axis=1))

        o = acc[:B_DLAT] * (1.0 / acc[B_DLAT:B_DLAT + 1])
        o_all = jnp.concatenate([o[:, h * tq:(h + 1) * tq].T for h in range(B_HEADS)], axis=1).astype(BF16)
        o_ref[0] = jnp.dot(o_all, wuv_ref[...], preferred_element_type=F32).astype(o_ref.dtype)

    for n in range(1, key_ref.shape[0] // kc_rows + 1):
        pl.when(nch == n)(functools.partial(variant, n))


def _dsa(qlat, qidx, widxt, ckv, ckvt, kidx, pos, w_uv):
    bsz, seq, _ = qlat.shape
    tq = min(B_TQ, seq)
    topk = min(TOPK_MAX, seq // 4)
    eye = jnp.eye(B_HEADS, dtype=w_uv.dtype)
    wuv = jnp.einsum('hde,hg->hdge', w_uv, eye).reshape(B_HEADS * B_DLAT, B_HEADS * B_DV).astype(BF16)
    return pl.pallas_call(
        functools.partial(_dsa_kernel, topk=topk),
        grid=(bsz, seq // tq),
        in_specs=[pl.BlockSpec((1, tq, B_HEADS * B_DLAT), lambda b, i: (b, i, 0)),
                  pl.BlockSpec((1, tq, IDX_HEADS * IDX_DH), lambda b, i: (b, i, 0)),
                  pl.BlockSpec((1, IDX_HEADS, tq), lambda b, i: (b, 0, i)),
                  pl.BlockSpec((1, seq, B_DLAT), lambda b, i: (b, 0, 0)),
                  pl.BlockSpec((1,) + ckvt.shape[1:], lambda b, i: (b, 0, 0, 0)),
                  pl.BlockSpec((1, seq, IDX_HEADS * IDX_DH), lambda b, i: (b, 0, 0)),
                  pl.BlockSpec(pos.shape, lambda b, i: (0, 0)),
                  pl.BlockSpec(wuv.shape, lambda b, i: (0, 0))],
        out_specs=pl.BlockSpec((1, tq, B_HEADS * B_DV), lambda b, i: (b, i, 0)),
        out_shape=jax.ShapeDtypeStruct((bsz, seq, B_HEADS * B_DV), BF16),
        scratch_shapes=[pltpu.VMEM((seq, tq), F32),
                        pltpu.VMEM((1, tq), F32), pltpu.VMEM((1, tq), I32)],
        compiler_params=_params(("arbitrary", "arbitrary")),
        name="dsa",
    )(qlat, qidx, widxt, ckv, ckvt, kidx, pos, wuv)


def _mid_kernel(oa_ref, ob_ref, sg_ref, x_ref, gmod_ref, wout_ref, shift_ref, scale_ref, ng_ref, win_ref,
                x1_ref, qc_ref, kc_ref, vct_ref, hglu_ref, sg1_ref):
    y = jnp.concatenate([oa_ref[0], ob_ref[0]], axis=1).astype(F32) * sg_ref[0].astype(F32)
    r = jnp.dot(y.astype(BF16), wout_ref[...], preferred_element_type=F32)
    x1 = x_ref[0] + gmod_ref[0] * r
    x1_ref[0] = x1
    h = _rms(x1, ng_ref[...]) * (1.0 + scale_ref[0]) + shift_ref[0]
    p = jnp.dot(h.astype(BF16), win_ref[...], preferred_element_type=F32)
    qw = C_HEADS * C_DH
    kw = C_KV_HEADS * C_DH
    qc_ref[0] = (p[:, :qw] * (C_DH ** -0.5 * LOG2E)).astype(BF16)
    kc_ref[0] = p[:, qw:qw + kw].astype(BF16)
    vct_ref[0] = p[:, qw + kw:qw + 2 * kw].T.astype(BF16)
    o = qw + 2 * kw
    hglu_ref[0] = p[:, o:o + D_CH] * jax.nn.sigmoid(p[:, o + D_CH:o + 2 * D_CH])
    gate = p[:, o + 2 * D_CH:]
    sg1_ref[0] = (gate * jax.nn.sigmoid(gate)).astype(BF16)


def _mid(oa, ob, sg, x, gmod, w_out, shift, scale, norm_g, w_in):
    bsz, seq, d = x.shape
    t = min(ROW_TILE, seq)
    wout = w_out.astype(BF16)
    win = w_in.astype(BF16)
    ng = norm_g.reshape(1, d)
    row = lambda w: pl.BlockSpec((1, t, w), lambda b, i: (b, i, 0))
    mod = pl.BlockSpec((1, 1, d), lambda b, i: (b, 0, 0))
    full = lambda a: pl.BlockSpec(a.shape, lambda b, i: (0,) * a.ndim)
    qw = C_HEADS * C_DH
    kw = C_KV_HEADS * C_DH
    rows = lambda w, dt: (jax.ShapeDtypeStruct((bsz, seq, w), dt), row(w))
    outs = [rows(d, F32), rows(qw, BF16), rows(kw, BF16),
            (jax.ShapeDtypeStruct((bsz, kw, seq), BF16), pl.BlockSpec((1, kw, t), lambda b, i: (b, 0, i))),
            rows(D_CH, F32), rows(d, BF16)]
    return pl.pallas_call(
        _mid_kernel,
        grid=(bsz, seq // t),
        in_specs=[row(oa.shape[2]), row(ob.shape[2]), row(d), row(d), mod, full(wout), mod, mod, full(ng),
                  full(win)],
        out_specs=[o[1] for o in outs],
        out_shape=[o[0] for o in outs],
        compiler_params=_params(("arbitrary", "arbitrary")),
        name="out0_proj1",
    )(oa, ob, sg, x, gmod, wout, shift, scale, ng, win)


def _swa_kernel(q_ref, kp_ref, kc_ref, vtp_ref, vtc_ref, pos_ref, sink_ref, o_ref, mask_ref, s_ref, p_ref, r_ref):
    b = pl.program_id(0)
    n = pl.program_id(1)
    w = q_ref.shape[1]
    rep = C_HEADS // C_KV_HEADS
    half = LANES // 2
    group = 2

    @pl.when((b == 0) & (n == 0))
    def _():
        k = lax.broadcasted_iota(I32, (2 * w, w), 0)
        q = lax.broadcasted_iota(I32, (2 * w, w), 1)
        dist = w + q - k
        valid = (dist >= 0) & (dist < w)
        mask_ref[1] = jnp.where(valid, 0.0, NEG_INF)
        mask_ref[0] = jnp.where(valid & (k >= w), 0.0, NEG_INF)

    lane = lax.broadcasted_iota(I32, (w, LANES), 1)
    slopes = _alibi(C_HEADS)
    qs = []
    for h in range(C_HEADS):
        g = h // rep
        x = q_ref[0, :, (h // 2) * LANES:(h // 2 + 1) * LANES].astype(F32)
        if (h % 2) != g:
            x = pltpu.roll(x, half, axis=1)
        x = jnp.where((lane // half) == g, x, 0.0).astype(BF16)
        qs.append(jnp.concatenate([x, _alibi_coef(w, slopes[h])], axis=1))
    kaug = jnp.concatenate([jnp.concatenate([kp_ref[0], kc_ref[0]], axis=0), pos_ref[...]], axis=1)
    vt = jnp.concatenate([vtp_ref[0], vtc_ref[0]], axis=1)
    mask = mask_ref[jnp.minimum(n, 1)]
    for g in range(C_HEADS // group):
        s_ref[g] = _dot(kaug, _transposed(jnp.concatenate(qs[g * group:(g + 1) * group], axis=0)))
    ot = []
    for g in range(C_HEADS // group):
        for cb in range(group):
            h = g * group + cb
            cs = slice(cb * w, (cb + 1) * w)
            s = s_ref[g, :, cs] + mask
            sink = LOG2E * (sink_ref[h:h + 1, :] + slopes[h] * (w + lane[:1, :]).astype(F32))
            m = jnp.maximum(jnp.max(s, axis=0, keepdims=True), sink)
            p = jnp.exp2(s - m)
            r_ref[g, :, cs] = 1.0 / (jnp.sum(p, axis=0, keepdims=True) + jnp.exp2(sink - m))
            p_ref[g, :, cs] = p.astype(BF16)
        o = jnp.dot(vt, p_ref[g], preferred_element_type=F32) * r_ref[g]
        for cb in range(group):
            kvg = (g * group + cb) // rep
            ot.append(o[kvg * C_DH:(kvg + 1) * C_DH, cb * w:(cb + 1) * w])
    o_ref[0] = jnp.concatenate(ot, axis=0).T.astype(o_ref.dtype)


def _swa(qc, kc, vct, pos, sinks):
    bsz, seq, qw = qc.shape
    w = WINDOW
    kw = kc.shape[2]
    sink_rows = jnp.broadcast_to(sinks.astype(F32)[:, None], (C_HEADS, w))
    cur = lambda width: pl.BlockSpec((1, w, width), lambda b, n: (b, n, 0))
    prev = lambda width: pl.BlockSpec((1, w, width), lambda b, n: (b, jnp.maximum(n - 1, 0), 0))
    return pl.pallas_call(
        _swa_kernel,
        grid=(bsz, seq // w),
        in_specs=[cur(qw), prev(kw), cur(kw),
                  pl.BlockSpec((1, kw, w), lambda b, n: (b, 0, jnp.maximum(n - 1, 0))),
                  pl.BlockSpec((1, kw, w), lambda b, n: (b, 0, n)),
                  pl.BlockSpec((2 * w, LANES), lambda b, n: (0, 0)),
                  pl.BlockSpec(sink_rows.shape, lambda b, n: (0, 0))],
        out_specs=cur(qw),
        out_shape=jax.ShapeDtypeStruct((bsz, seq, qw), BF16),
        scratch_shapes=[pltpu.VMEM((2, 2 * w, w), F32),
                        pltpu.VMEM((C_HEADS // 2, 2 * w, 2 * w), F32),
                        pltpu.VMEM((C_HEADS // 2, 2 * w, 2 * w), BF16),
                        pltpu.VMEM((C_HEADS // 2, 1, 2 * w), F32)],
        compiler_params=_params(("arbitrary", "arbitrary")),
        name="swa",
    )(qc, kc, kc, vct, vct, pos, sink_rows)


def _conv_kernel(cur_ref, halo_ref, w_ref, b_ref, g_ref, beta_ref, o_ref, buf_ref):
    i = pl.program_id(1)
    t = cur_ref.shape[1]
    halo = halo_ref.shape[1]
    buf_ref[:halo, :] = jnp.where(i > 0, halo_ref[0], 0.0)
    buf_ref[halo:, :] = cur_ref[0]
    acc = jnp.zeros((t, cur_ref.shape[2]), F32) + b_ref[...]
    off = halo - (CONV_W - 1)
    for j in range(CONV_W):
        acc = acc + w_ref[j:j + 1, :] * buf_ref[off + j:off + j + t, :]
    mu = jnp.mean(acc, axis=-1, keepdims=True)
    xc = acc - mu
    var = jnp.mean(xc * xc, axis=-1, keepdims=True)
    y = xc * lax.rsqrt(var + EPS) * g_ref[...] + beta_ref[...]
    o_ref[0] = (y * jax.nn.sigmoid(y)).astype(o_ref.dtype)


def _conv(hglu, dw_w, dw_b, ln_g, ln_b):
    bsz, seq, ch = hglu.shape
    t = min(CONV_TILE, seq)
    per = t // CONV_HALO
    vec = lambda a: a.reshape(1, ch)
    full = lambda a: pl.BlockSpec(a.shape, lambda b, i: (0,) * a.ndim)
    small = [dw_w, vec(dw_b), vec(ln_g), vec(ln_b)]
    return pl.pallas_call(
        _conv_kernel,
        grid=(bsz, seq // t),
        in_specs=[pl.BlockSpec((1, t, ch), lambda b, i: (b, i, 0)),
                  pl.BlockSpec((1, CONV_HALO, ch), lambda b, i: (b, jnp.maximum(i * per - 1, 0), 0))]
                 + [full(a) for a in small],
        out_specs=pl.BlockSpec((1, t, ch), lambda b, i: (b, i, 0)),
        out_shape=jax.ShapeDtypeStruct((bsz, seq, ch), BF16),
        scratch_shapes=[pltpu.VMEM((CONV_HALO + t, ch), F32)],
        compiler_params=_params(("arbitrary", "arbitrary")),
        name="conformer_conv",
    )(hglu, hglu, *small)


def _final_kernel(oc_ref, od_ref, sg_ref, x_ref, gmod_ref, wout_ref, fg_ref, o_ref):
    y = jnp.concatenate([oc_ref[0], od_ref[0]], axis=1).astype(F32) * sg_ref[0].astype(F32)
    r = jnp.dot(y.astype(BF16), wout_ref[...], preferred_element_type=F32)
    x2 = x_ref[0] + gmod_ref[0] * r
    o_ref[0] = _rms(x2, fg_ref[...])


def _final(oc, od, sg, x1, gmod, w_out, final_g):
    bsz, seq, d = x1.shape
    t = min(ROW_TILE, seq)
    wout = w_out.astype(BF16)
    fg = final_g.reshape(1, d)
    row = lambda w: pl.BlockSpec((1, t, w), lambda b, i: (b, i, 0))
    mod = pl.BlockSpec((1, 1, d), lambda b, i: (b, 0, 0))
    full = lambda a: pl.BlockSpec(a.shape, lambda b, i: (0,) * a.ndim)
    return pl.pallas_call(
        _final_kernel,
        grid=(bsz, seq // t),
        in_specs=[row(oc.shape[2]), row(od.shape[2]), row(d), row(d), mod, full(wout), full(fg)],
        out_specs=row(d),
        out_shape=jax.ShapeDtypeStruct((bsz, seq, d), F32),
        compiler_params=_params(("arbitrary", "arbitrary")),
        name="out1_final",
    )(oc, od, sg, x1, gmod, wout, fg)


def kernel(x, c, ada_w, ada_b, norm_g, ab_w_in, ab_q_norm_g, ab_kv_norm_g, ab_w_uq, ab_w_qidx, ab_w_uv,
           ab_lam_q1, ab_lam_k1, ab_lam_q2, ab_lam_k2, ab_subln_g, ab_w_out,
           cd_w_in, cd_sinks, cd_dw_w, cd_dw_b, cd_ln_g, cd_ln_b, cd_w_out, final_g):
    bsz, seq, d = x.shape
    mods = _ada_mods(c, ada_w, ada_b)
    mod = lambda l, k: mods[l, :, k * d:(k + 1) * d].reshape(bsz, 1, d)
    pos = _key_positions(seq)
    q1, q2, ka, vat, qlat, ckv, ckvt, qidx, kidx, widxt, sg0 = _proj0(
        x, mod(0, 0), mod(0, 1), norm_g[0], ab_w_in[0], ab_q_norm_g[0], ab_kv_norm_g[0], ab_w_uq[0],
        ab_w_qidx[0])
    oa = _diff_attn(q1, q2, ka, vat, pos, ab_lam_q1[0], ab_lam_k1[0], ab_lam_q2[0], ab_lam_k2[0],
                    ab_subln_g[0], 0)
    ob = _dsa(qlat, qidx, widxt, ckv, ckvt, kidx, pos, ab_w_uv[0])
    x1, qc, kc, vct, hglu, sg1 = _mid(oa, ob, sg0, x, mod(0, 2), ab_w_out[0], mod(1, 0), mod(1, 1), norm_g[1],
                                     cd_w_in[0])
    oc = _swa(qc, kc, vct, pos, cd_sinks[0])
    od = _conv(hglu, cd_dw_w[0], cd_dw_b[0], cd_ln_g[0], cd_ln_b[0])
    return _final(oc, od, sg1, x1, mod(1, 2), cd_w_out[0], final_g)
```

```python
import functools
import math

import numpy as np
import jax
import jax.numpy as jnp
from jax import lax
from jax.experimental import pallas as pl
from jax.experimental.pallas import tpu as pltpu

F32 = jnp.float32
BF16 = jnp.bfloat16
I32 = jnp.int32

D_MODEL = 1024
EPS = 1e-6
A_HEADS = 4
A_DH = 64
A_DV = 128
B_HEADS = 8
B_DQLAT = 128
B_DLAT = 128
B_DV = 64
IDX_HEADS = 8
IDX_DH = 32
TOPK_MAX = 256
C_HEADS = 8
C_KV_HEADS = 2
C_DH = 64
WINDOW = 128
D_CH = 512
CONV_W = 31

LANES = 128
SUBLANES = 8
INT_MIN = -2 ** 31
ORDER_OF_NEG_INF = INT_MIN + 0x7FFFFF
NEG_INF = float("-inf")
VMEM_LIMIT = 48 * 1024 * 1024

ROW_TILE = 1024
KEY_CHUNK = 256
A_TQ = 256
B_TQ = 128
POS_SPLIT = 16
ONES_ROWS = 16
LOG2E = math.log2(math.e)
COEF_TERMS = 4
CONV_TILE = 256
CONV_HALO = 32


def _alibi(n):
    return [float(2.0 ** (-8.0 * i / n)) for i in range(1, n + 1)]


def _params(sem):
    return pltpu.CompilerParams(dimension_semantics=sem, vmem_limit_bytes=VMEM_LIMIT)


def _nt_dot(a, b):
    return lax.dot_general(a, b, (((1,), (1,)), ((), ())), preferred_element_type=F32)


def _dot(a, b):
    return jnp.dot(a, b, preferred_element_type=F32)


def _rms(x, g):
    return x * lax.rsqrt(jnp.mean(x * x, axis=-1, keepdims=True) + EPS) * g


def _bf16_terms(x, n):
    terms = []
    for _ in range(n):
        t = float(np.asarray(x, np.float32).astype(jnp.bfloat16).astype(np.float32))
        terms.append(t)
        x -= t
    return terms


def _alibi_coef(rows, slope):
    lane = lax.broadcasted_iota(I32, (rows, LANES), 1)
    out = jnp.zeros((rows, LANES), F32)
    for i, t in enumerate(_bf16_terms(LOG2E * slope, COEF_TERMS)):
        out = jnp.where(lane == 2 * i, POS_SPLIT * t, jnp.where(lane == 2 * i + 1, t, out))
    return out.astype(BF16)


def _key_positions(seq):
    pos = np.zeros((seq, LANES), np.float32)
    for i in range(COEF_TERMS):
        pos[:, 2 * i] = np.arange(seq) // POS_SPLIT
        pos[:, 2 * i + 1] = np.arange(seq) % POS_SPLIT
    return jnp.asarray(pos, BF16)


def _ada_kernel(c_ref, w_ref, b_ref, o_ref):
    c = c_ref[...]
    sc = c * jax.nn.sigmoid(c)
    o_ref[0] = jnp.dot(sc.astype(BF16), w_ref[0].astype(BF16), preferred_element_type=F32) + b_ref[0]


def _ada_mods(c, ada_w, ada_b):
    depth, d, d3 = ada_w.shape
    bsz = c.shape[0]
    nt = d3 // d
    return pl.pallas_call(
        _ada_kernel,
        grid=(depth, nt),
        in_specs=[pl.BlockSpec((bsz, d), lambda l, j: (0, 0)),
                  pl.BlockSpec((1, d, d), lambda l, j: (l, 0, j)),
                  pl.BlockSpec((1, 1, d), lambda l, j: (l, 0, j))],
        out_specs=pl.BlockSpec((1, bsz, d), lambda l, j: (l, 0, j)),
        out_shape=jax.ShapeDtypeStruct((depth, bsz, d3), F32),
        compiler_params=_params(("arbitrary", "arbitrary")),
        name="ada_mods",
    )(c, ada_w, ada_b.reshape(depth, 1, d3))


def _proj0_kernel(x_ref, shift_ref, scale_ref, ng_ref, wmain_ref, wgate_ref, wkidx_ref, wwidx_ref,
                  qng_ref, kvng_ref,
                  qa_ref, ka_ref, vat_ref, cq_ref, ckv_ref, ckvt_ref, kidx_ref, widxt_ref, sg_ref):
    x = x_ref[0]
    h = _rms(x, ng_ref[...]) * (1.0 + scale_ref[0]) + shift_ref[0]
    hb = h.astype(BF16)
    main = jnp.dot(hb, wmain_ref[...], preferred_element_type=F32)
    qw = A_HEADS * 2 * A_DH
    qa_ref[0] = (main[:, :qw] * (A_DH ** -0.5 * LOG2E)).astype(BF16)
    ka_ref[0] = main[:, qw:2 * qw].astype(BF16)
    va = main[:, 2 * qw:3 * qw]
    cq_ref[0] = _rms(main[:, 3 * qw:3 * qw + B_DQLAT], qng_ref[...]).astype(BF16)
    ckv = _rms(main[:, 3 * qw + B_DQLAT:], kvng_ref[...])
    ckv_ref[0] = ckv.astype(BF16)
    ch = vat_ref.shape[3]
    ones = jnp.ones((ONES_ROWS, ch), F32)
    for c in range(vat_ref.shape[1]):
        vt = va[c * ch:(c + 1) * ch].T
        vat_ref[0, c] = jnp.concatenate(
            [blk for h in range(A_HEADS) for blk in (vt[h * A_DV:(h + 1) * A_DV], ones)], axis=0).astype(BF16)
        ckvt_ref[0, c] = jnp.concatenate([ckv[c * ch:(c + 1) * ch].T, ones], axis=0).astype(BF16)
    kidx_ref[0] = jnp.dot(hb, wkidx_ref[...], preferred_element_type=F32).astype(BF16)
    widxt_ref[0] = _nt_dot(wwidx_ref[...], hb)
    gate = jnp.dot(hb, wgate_ref[...], preferred_element_type=F32)
    sg_ref[0] = (gate * jax.nn.sigmoid(gate)).astype(BF16)


def _proj0(x, shift, scale, norm_g, w_in, q_norm_g, kv_norm_g):
    bsz, seq, d = x.shape
    t = min(ROW_TILE, seq)
    ch = min(KEY_CHUNK, seq)
    qw = A_HEADS * 2 * A_DH
    o_main = 3 * qw + B_DQLAT + B_DLAT
    wmain = w_in[:, :o_main].astype(BF16)
    wkidx = jnp.tile(w_in[:, o_main:o_main + IDX_DH], (1, IDX_HEADS)).astype(BF16)
    wwidx = w_in[:, o_main + IDX_DH:o_main + IDX_DH + IDX_HEADS].T.astype(BF16)
    wgate = w_in[:, o_main + IDX_DH + IDX_HEADS:].astype(BF16)
    row = lambda w: pl.BlockSpec((1, t, w), lambda b, i: (b, i, 0))
    rows = lambda w: (jax.ShapeDtypeStruct((bsz, seq, w), BF16), row(w))
    chunked = lambda w: (jax.ShapeDtypeStruct((bsz, seq // ch, w, ch), BF16),
                         pl.BlockSpec((1, t // ch, w, ch), lambda b, i: (b, i, 0, 0)))
    mod = pl.BlockSpec((1, 1, d), lambda b, i: (b, 0, 0))
    full = lambda a: pl.BlockSpec(a.shape, lambda b, i: (0,) * a.ndim)
    ng = norm_g.reshape(1, d)
    qng = q_norm_g.reshape(1, B_DQLAT)
    kvng = kv_norm_g.reshape(1, B_DLAT)
    iw = IDX_HEADS * IDX_DH
    outs = [rows(qw), rows(qw), chunked(A_HEADS * (A_DV + ONES_ROWS)), rows(B_DQLAT), rows(B_DLAT),
            chunked(B_DLAT + ONES_ROWS), rows(iw),
            (jax.ShapeDtypeStruct((bsz, IDX_HEADS, seq), F32),
             pl.BlockSpec((1, IDX_HEADS, t), lambda b, i: (b, 0, i))),
            rows(d)]
    return pl.pallas_call(
        _proj0_kernel,
        grid=(bsz, seq // t),
        in_specs=[row(d), mod, mod, full(ng), full(wmain), full(wgate), full(wkidx), full(wwidx),
                  full(qng), full(kvng)],
        out_specs=[o[1] for o in outs],
        out_shape=[o[0] for o in outs],
        compiler_params=_params(("arbitrary", "arbitrary")),
        name="proj0",
    )(x, shift, scale, ng, wmain, wgate, wkidx, wwidx, qng, kvng)


def _diff_attn_kernel(q_ref, k_ref, vt_ref, pos_ref, lq1_ref, lk1_ref, lq2_ref, lk2_ref, g_ref, o_ref,
                      *, lam_init):
    i = pl.program_id(1)
    tq = q_ref.shape[1]
    tk = vt_ref.shape[3]
    lam = (jnp.exp(jnp.sum(lq1_ref[...] * lk1_ref[...], keepdims=True))
           - jnp.exp(jnp.sum(lq2_ref[...] * lk2_ref[...], keepdims=True)) + lam_init)
    krow = lax.broadcasted_iota(I32, (tk, 2 * tq), 0)
    qcol = lax.broadcasted_iota(I32, (tk, 2 * tq), 1) % tq
    causal = krow <= qcol
    slopes = _alibi(A_HEADS)
    heads = [slice(h * A_DV, (h + 1) * A_DV) for h in range(A_HEADS)]
    first = lax.broadcasted_iota(I32, (tq, A_DV), 1) < A_DH
    zero = jnp.zeros((tq, A_DV), BF16)
    qs = [jnp.concatenate(
        [jnp.concatenate([jnp.where(first, q_ref[0, :, hs], zero), _alibi_coef(tq, slopes[h])], axis=1),
         jnp.concatenate([jnp.where(first, zero, q_ref[0, :, hs]), _alibi_coef(tq, slopes[h])], axis=1)], axis=0)
          for h, hs in enumerate(heads)]

    rows_v = vt_ref.shape[2] // A_HEADS
    vrows = [slice(h * rows_v, (h + 1) * rows_v) for h in range(A_HEADS)]

    def variant(n):
        rows = n * tk
        pos = pos_ref[:rows, :]
        for h, hs in enumerate(heads):
            s = _nt_dot(jnp.concatenate([k_ref[0, :rows, hs], pos], axis=1), qs[h])
            last = jnp.where(causal, s[rows - tk:], NEG_INF)
            s = last if n == 1 else jnp.concatenate([s[:rows - tk], last], axis=0)
            m = jnp.max(s, axis=0, keepdims=True)
            vt = jnp.concatenate([vt_ref[0, c, vrows[h], :] for c in range(n)], axis=1)
            acc = _dot(vt, jnp.exp2(s - m).astype(BF16))
            o = acc[:A_DV] * (1.0 / acc[A_DV:A_DV + 1])
            od = (o[:, :tq] - lam * o[:, tq:]).T
            o_ref[0, :, hs] = (_rms(od, g_ref[...]) * (1.0 - lam_init)).astype(o_ref.dtype)

    for n in range(1, vt_ref.shape[1] + 1):
        pl.when(i + 1 == n)(functools.partial(variant, n))


def _diff_attn(qa, ka, vat, pos, lam_q1, lam_k1, lam_q2, lam_k2, subln_g, layer_idx):
    bsz, seq, w = qa.shape
    tq = vat.shape[3]
    lam_init = 0.8 - 0.6 * math.exp(-0.3 * layer_idx)
    vec = lambda a: a.reshape(1, -1)
    full = lambda a: pl.BlockSpec(a.shape, lambda b, i: (0,) * a.ndim)
    small = [vec(lam_q1), vec(lam_k1), vec(lam_q2), vec(lam_k2), vec(subln_g)]
    return pl.pallas_call(
        functools.partial(_diff_attn_kernel, lam_init=lam_init),
        grid=(bsz, seq // tq),
        in_specs=[pl.BlockSpec((1, tq, w), lambda b, i: (b, i, 0)),
                  pl.BlockSpec((1, seq, w), lambda b, i: (b, 0, 0)),
                  pl.BlockSpec((1,) + vat.shape[1:], lambda b, i: (b, 0, 0, 0)),
                  full(pos)] + [full(a) for a in small],
        out_specs=pl.BlockSpec((1, tq, w), lambda b, i: (b, i, 0)),
        out_shape=jax.ShapeDtypeStruct((bsz, seq, w), BF16),
        compiler_params=_params(("arbitrary", "arbitrary")),
        name="diff_attn",
    )(qa, ka, vat, pos, *small)


def _dsa_kernel(cq_ref, wuq_ref, wqidx_ref, wt_ref, kv_ref, kvt_ref, kidx_ref, pos_ref, wuv_ref, o_ref,
                key_ref, thr_ref, lim_ref, *, topk):
    qi = pl.program_id(1)
    tq = cq_ref.shape[1]
    kc_rows = kvt_ref.shape[3]
    nch = (qi * tq) // kc_rows + 1

    cq = cq_ref[0]
    qidx = _dot(cq, wqidx_ref[...])
    qlat = (_dot(cq, wuq_ref[...]) * (B_DLAT ** -0.5 * LOG2E)).astype(BF16)
    head_of_lane = lax.broadcasted_iota(I32, qidx.shape, 1) // IDX_DH
    qstack = jnp.concatenate([jnp.where(head_of_lane == h, qidx, 0.0) for h in range(IDX_HEADS)],
                             axis=0).astype(BF16)
    wt = wt_ref[0]
    slopes = _alibi(B_HEADS)
    group = 2
    ngroups = B_HEADS // group
    qs = jnp.concatenate(
        [jnp.concatenate([qlat[:, h * B_DLAT:(h + 1) * B_DLAT], _alibi_coef(tq, slopes[h])], axis=1)
         for h in range(B_HEADS)], axis=0)

    def to_float(u):
        s = jnp.maximum(u ^ INT_MIN, ORDER_OF_NEG_INF)
        return pltpu.bitcast(s ^ ((s >> 31) & 0x7FFFFFFF), F32)

    def variant(n):
        rows = n * kc_rows
        kpos = lax.broadcasted_iota(I32, (rows, tq), 0)
        causal = kpos <= qi * tq + lax.broadcasted_iota(I32, (rows, tq), 1)
        krow = kpos[:SUBLANES]

        rel = _nt_dot(kidx_ref[0, :rows, :], qstack)
        isc = jnp.zeros((rows, tq), F32)
        for h in range(IDX_HEADS):
            isc = isc + wt[h:h + 1, :] * jnp.maximum(rel[:, h * tq:(h + 1) * tq], 0.0)
        key_ref[:rows, :] = jnp.where(causal, isc, NEG_INF)

        def count(pred):
            accs = [jnp.zeros((SUBLANES, tq), I32) for _ in range(4)]
            for r in range(rows // SUBLANES):
                ind = pred(key_ref[r * SUBLANES:(r + 1) * SUBLANES, :], r * SUBLANES + krow)
                accs[r % 4] = accs[r % 4] + ind.astype(I32)
            return jnp.sum((accs[0] + accs[1]) + (accs[2] + accs[3]), axis=0, keepdims=True)

        thr_ref[...] = jnp.full((1, tq), NEG_INF, F32)
        lim_ref[...] = jnp.full((1, tq), 2 ** 30, I32)
        if rows > topk:
            def bit_step(t, cand):
                trial = cand | jnp.left_shift(jnp.int32(1), 31 - t)
                thr = to_float(trial)
                cnt = count(lambda x, p: x >= thr)
                return jnp.where(cnt >= topk, trial, cand)
            thr = to_float(lax.fori_loop(0, 32, bit_step, jnp.zeros((1, tq), I32)))
            thr_ref[...] = thr
            need = topk - count(lambda x, p: x > thr)
            n_eq = count(lambda x, p: x == thr)

            @pl.when(jnp.max((n_eq > need).astype(I32)) > 0)
            def _():
                def pos_step(t, lim):
                    trial = lim | jnp.left_shift(jnp.int32(1), 11 - t)
                    cnt = count(lambda x, p: (x == thr) & (p < trial))
                    return jnp.where(cnt < need, trial, lim)
                lim_ref[...] = lax.fori_loop(0, 12, pos_step, jnp.zeros((1, tq), I32))

        thr = thr_ref[...]
        lim = lim_ref[...]
        x = key_ref[:rows, :]
        sel = ((x > thr) | ((x == thr) & (kpos <= lim))) & causal
        bias = jnp.where(sel, 0.0, NEG_INF)
        bias = jnp.concatenate([bias] * group, axis=1)
        kaug = jnp.concatenate([kv_ref[0, :rows, :], pos_ref[:rows, :]], axis=1)
        kvt = jnp.concatenate([kvt_ref[0, c] for c in range(n)], axis=1)
        s_all = _nt_dot(kaug, qs)
        ps = []
        for g in range(ngroups):
            s = s_all[:, g * group * tq:(g + 1) * group * tq] + bias
            m = jnp.max(s, axis=0, keepdims=True)
            ps.append(jnp.exp2(s - m).astype(BF16))
        acc = _dot(kvt, jnp.concatenate(ps, axis=1))

        o = acc[:B_DLAT] * (1.0 / acc[B_DLAT:B_DLAT + 1])
        o_all = jnp.concatenate([o[:, h * tq:(h + 1) * tq].T for h in range(B_HEADS)], axis=1).astype(BF16)
        o_ref[0] = jnp.dot(o_all, wuv_ref[...], preferred_element_type=F32).astype(o_ref.dtype)

    for n in range(1, key_ref.shape[0] // kc_rows + 1):
        pl.when(nch == n)(functools.partial(variant, n))


def _dsa(cq, w_uq, w_qidx, widxt, ckv, ckvt, kidx, pos, w_uv):
    bsz, seq, _ = cq.shape
    wuq = w_uq.reshape(B_DQLAT, B_HEADS * B_DLAT).astype(BF16)
    wqidx = w_qidx.reshape(B_DQLAT, IDX_HEADS * IDX_DH).astype(BF16)
    tq = min(B_TQ, seq)
    topk = min(TOPK_MAX, seq // 4)
    eye = jnp.eye(B_HEADS, dtype=w_uv.dtype)
    wuv = jnp.einsum('hde,hg->hdge', w_uv, eye).reshape(B_HEADS * B_DLAT, B_HEADS * B_DV).astype(BF16)
    return pl.pallas_call(
        functools.partial(_dsa_kernel, topk=topk),
        grid=(bsz, seq // tq),
        in_specs=[pl.BlockSpec((1, tq, B_DQLAT), lambda b, i: (b, i, 0)),
                  pl.BlockSpec(wuq.shape, lambda b, i: (0, 0)),
                  pl.BlockSpec(wqidx.shape, lambda b, i: (0, 0)),
                  pl.BlockSpec((1, IDX_HEADS, tq), lambda b, i: (b, 0, i)),
                  pl.BlockSpec((1, seq, B_DLAT), lambda b, i: (b, 0, 0)),
                  pl.BlockSpec((1,) + ckvt.shape[1:], lambda b, i: (b, 0, 0, 0)),
                  pl.BlockSpec((1, seq, IDX_HEADS * IDX_DH), lambda b, i: (b, 0, 0)),
                  pl.BlockSpec(pos.shape, lambda b, i: (0, 0)),
                  pl.BlockSpec(wuv.shape, lambda b, i: (0, 0))],
        out_specs=pl.BlockSpec((1, tq, B_HEADS * B_DV), lambda b, i: (b, i, 0)),
        out_shape=jax.ShapeDtypeStruct((bsz, seq, B_HEADS * B_DV), BF16),
        scratch_shapes=[pltpu.VMEM((seq, tq), F32),
                        pltpu.VMEM((1, tq), F32), pltpu.VMEM((1, tq), I32)],
        compiler_params=_params(("arbitrary", "arbitrary")),
        name="dsa",
    )(cq, wuq, wqidx, widxt, ckv, ckvt, kidx, pos, wuv)


def _mid_kernel(oa_ref, ob_ref, sg_ref, x_ref, gmod_ref, wout_ref, shift_ref, scale_ref, ng_ref, win_ref,
                x1_ref, qc_ref, kc_ref, vct_ref, hglu_ref, sg1_ref):
    y = jnp.concatenate([oa_ref[0], ob_ref[0]], axis=1).astype(F32) * sg_ref[0].astype(F32)
    r = jnp.dot(y.astype(BF16), wout_ref[...], preferred_element_type=F32)
    x1 = x_ref[0] + gmod_ref[0] * r
    x1_ref[0] = x1
    h = _rms(x1, ng_ref[...]) * (1.0 + scale_ref[0]) + shift_ref[0]
    p = jnp.dot(h.astype(BF16), win_ref[...], preferred_element_type=F32)
    qw = C_HEADS * C_DH
    kw = C_KV_HEADS * C_DH
    qc_ref[0] = (p[:, :qw] * (C_DH ** -0.5 * LOG2E)).astype(BF16)
    kc_ref[0] = p[:, qw:qw + kw].astype(BF16)
    vct_ref[0] = p[:, qw + kw:qw + 2 * kw].T.astype(BF16)
    o = qw + 2 * kw
    hglu_ref[0] = p[:, o:o + D_CH] * jax.nn.sigmoid(p[:, o + D_CH:o + 2 * D_CH])
    gate = p[:, o + 2 * D_CH:]
    sg1_ref[0] = (gate * jax.nn.sigmoid(gate)).astype(BF16)


def _mid(oa, ob, sg, x, gmod, w_out, shift, scale, norm_g, w_in):
    bsz, seq, d = x.shape
    t = min(ROW_TILE, seq)
    wout = w_out.astype(BF16)
    win = w_in.astype(BF16)
    ng = norm_g.reshape(1, d)
    row = lambda w: pl.BlockSpec((1, t, w), lambda b, i: (b, i, 0))
    mod = pl.BlockSpec((1, 1, d), lambda b, i: (b, 0, 0))
    full = lambda a: pl.BlockSpec(a.shape, lambda b, i: (0,) * a.ndim)
    qw = C_HEADS * C_DH
    kw = C_KV_HEADS * C_DH
    rows = lambda w, dt: (jax.ShapeDtypeStruct((bsz, seq, w), dt), row(w))
    outs = [rows(d, F32), rows(qw, BF16), rows(kw, BF16),
            (jax.ShapeDtypeStruct((bsz, kw, seq), BF16), pl.BlockSpec((1, kw, t), lambda b, i: (b, 0, i))),
            rows(D_CH, F32), rows(d, BF16)]
    return pl.pallas_call(
        _mid_kernel,
        grid=(bsz, seq // t),
        in_specs=[row(oa.shape[2]), row(ob.shape[2]), row(d), row(d), mod, full(wout), mod, mod, full(ng),
                  full(win)],
        out_specs=[o[1] for o in outs],
        out_shape=[o[0] for o in outs],
        compiler_params=_params(("arbitrary", "arbitrary")),
        name="out0_proj1",
    )(oa, ob, sg, x, gmod, wout, shift, scale, ng, win)


def _swa_kernel(q_ref, kp_ref, kc_ref, vtp_ref, vtc_ref, pos_ref, sink_ref, o_ref, mask_ref):
    b = pl.program_id(0)
    n = pl.program_id(1)
    w = q_ref.shape[1]
    rep = C_HEADS // C_KV_HEADS
    half = LANES // 2

    @pl.when((b == 0) & (n == 0))
    def _():
        k = lax.broadcasted_iota(I32, (2 * w, w), 0)
        q = lax.broadcasted_iota(I32, (2 * w, w), 1)
        dist = w + q - k
        valid = (dist >= 0) & (dist < w)
        mask_ref[1] = jnp.where(valid, 0.0, NEG_INF)
        mask_ref[0] = jnp.where(valid & (k >= w), 0.0, NEG_INF)

    lane = lax.broadcasted_iota(I32, (w, LANES), 1)
    slopes = _alibi(C_HEADS)
    qs = []
    for h in range(C_HEADS):
        g = h // rep
        x = q_ref[0, :, (h // 2) * LANES:(h // 2 + 1) * LANES].astype(F32)
        if (h % 2) != g:
            x = pltpu.roll(x, half, axis=1)
        x = jnp.where((lane // half) == g, x, 0.0).astype(BF16)
        qs.append(jnp.concatenate([x, _alibi_coef(w, slopes[h])], axis=1))
    kaug = jnp.concatenate([jnp.concatenate([kp_ref[0], kc_ref[0]], axis=0), pos_ref[...]], axis=1)
    vt = jnp.concatenate([vtp_ref[0], vtc_ref[0]], axis=1)
    mask = mask_ref[jnp.minimum(n, 1)]
    s_all = _nt_dot(kaug, jnp.concatenate(qs, axis=0))
    ps, rs = [], []
    for h in range(C_HEADS):
        s = s_all[:, h * w:(h + 1) * w] + mask
        sink = LOG2E * (sink_ref[h:h + 1, :] + slopes[h] * (w + lane[:1, :]).astype(F32))
        m = jnp.maximum(jnp.max(s, axis=0, keepdims=True), sink)
        p = jnp.exp2(s - m)
        rs.append(1.0 / (jnp.sum(p, axis=0, keepdims=True) + jnp.exp2(sink - m)))
        ps.append(p.astype(BF16))
    o = _dot(vt, jnp.concatenate(ps, axis=1)) * jnp.concatenate(rs, axis=1)
    ot = [o[(h // rep) * C_DH:(h // rep + 1) * C_DH, h * w:(h + 1) * w] for h in range(C_HEADS)]
    o_ref[0] = jnp.concatenate(ot, axis=0).T.astype(o_ref.dtype)


def _swa(qc, kc, vct, pos, sinks):
    bsz, seq, qw = qc.shape
    w = WINDOW
    kw = kc.shape[2]
    sink_rows = jnp.broadcast_to(sinks.astype(F32)[:, None], (C_HEADS, w))
    cur = lambda width: pl.BlockSpec((1, w, width), lambda b, n: (b, n, 0))
    prev = lambda width: pl.BlockSpec((1, w, width), lambda b, n: (b, jnp.maximum(n - 1, 0), 0))
    return pl.pallas_call(
        _swa_kernel,
        grid=(bsz, seq // w),
        in_specs=[cur(qw), prev(kw), cur(kw),
                  pl.BlockSpec((1, kw, w), lambda b, n: (b, 0, jnp.maximum(n - 1, 0))),
                  pl.BlockSpec((1, kw, w), lambda b, n: (b, 0, n)),
                  pl.BlockSpec((2 * w, LANES), lambda b, n: (0, 0)),
                  pl.BlockSpec(sink_rows.shape, lambda b, n: (0, 0))],
        out_specs=cur(qw),
        out_shape=jax.ShapeDtypeStruct((bsz, seq, qw), BF16),
        scratch_shapes=[pltpu.VMEM((2, 2 * w, w), F32)],
        compiler_params=_params(("arbitrary", "arbitrary")),
        name="swa",
    )(qc, kc, kc, vct, vct, pos, sink_rows)


def _conv_kernel(cur_ref, halo_ref, w_ref, b_ref, g_ref, beta_ref, o_ref, buf_ref, shift_ref):
    i = pl.program_id(1)
    t = cur_ref.shape[1]
    halo = halo_ref.shape[1]
    buf_ref[:halo, :] = jnp.where(i > 0, halo_ref[0], 0.0)
    buf_ref[halo:, :] = cur_ref[0]
    span = shift_ref.shape[1]
    for ph in range(1, SUBLANES):
        shift_ref[ph - 1] = buf_ref[ph:ph + span, :]
    acc = jnp.zeros((t, cur_ref.shape[2]), F32) + b_ref[...]
    off = halo - (CONV_W - 1)
    for j in range(CONV_W):
        base, ph = (off + j) // SUBLANES * SUBLANES, (off + j) % SUBLANES
        rows = buf_ref[base:base + t, :] if ph == 0 else shift_ref[ph - 1, base:base + t, :]
        acc = acc + w_ref[j:j + 1, :] * rows
    mu = jnp.mean(acc, axis=-1, keepdims=True)
    xc = acc - mu
    var = jnp.mean(xc * xc, axis=-1, keepdims=True)
    y = xc * lax.rsqrt(var + EPS) * g_ref[...] + beta_ref[...]
    o_ref[0] = (y * jax.nn.sigmoid(y)).astype(o_ref.dtype)


def _conv(hglu, dw_w, dw_b, ln_g, ln_b):
    bsz, seq, ch = hglu.shape
    t = min(CONV_TILE, seq)
    per = t // CONV_HALO
    vec = lambda a: a.reshape(1, ch)
    full = lambda a: pl.BlockSpec(a.shape, lambda b, i: (0,) * a.ndim)
    small = [dw_w, vec(dw_b), vec(ln_g), vec(ln_b)]
    return pl.pallas_call(
        _conv_kernel,
        grid=(bsz, seq // t),
        in_specs=[pl.BlockSpec((1, t, ch), lambda b, i: (b, i, 0)),
                  pl.BlockSpec((1, CONV_HALO, ch), lambda b, i: (b, jnp.maximum(i * per - 1, 0), 0))]
                 + [full(a) for a in small],
        out_specs=pl.BlockSpec((1, t, ch), lambda b, i: (b, i, 0)),
        out_shape=jax.ShapeDtypeStruct((bsz, seq, ch), BF16),
        scratch_shapes=[pltpu.VMEM((CONV_HALO + t, ch), F32),
                        pltpu.VMEM((SUBLANES - 1, CONV_HALO - SUBLANES + t, ch), F32)],
        compiler_params=_params(("arbitrary", "arbitrary")),
        name="conformer_conv",
    )(hglu, hglu, *small)


def _final_kernel(oc_ref, od_ref, sg_ref, x_ref, gmod_ref, wout_ref, fg_ref, o_ref):
    y = jnp.concatenate([oc_ref[0], od_ref[0]], axis=1).astype(F32) * sg_ref[0].astype(F32)
    r = jnp.dot(y.astype(BF16), wout_ref[...], preferred_element_type=F32)
    x2 = x_ref[0] + gmod_ref[0] * r
    o_ref[0] = _rms(x2, fg_ref[...])


def _final(oc, od, sg, x1, gmod, w_out, final_g):
    bsz, seq, d = x1.shape
    t = min(ROW_TILE, seq)
    wout = w_out.astype(BF16)
    fg = final_g.reshape(1, d)
    row = lambda w: pl.BlockSpec((1, t, w), lambda b, i: (b, i, 0))
    mod = pl.BlockSpec((1, 1, d), lambda b, i: (b, 0, 0))
    full = lambda a: pl.BlockSpec(a.shape, lambda b, i: (0,) * a.ndim)
    return pl.pallas_call(
        _final_kernel,
        grid=(bsz, seq // t),
        in_specs=[row(oc.shape[2]), row(od.shape[2]), row(d), row(d), mod, full(wout), full(fg)],
        out_specs=row(d),
        out_shape=jax.ShapeDtypeStruct((bsz, seq, d), F32),
        compiler_params=_params(("arbitrary", "arbitrary")),
        name="out1_final",
    )(oc, od, sg, x1, gmod, wout, fg)


def kernel(x, c, ada_w, ada_b, norm_g, ab_w_in, ab_q_norm_g, ab_kv_norm_g, ab_w_uq, ab_w_qidx, ab_w_uv,
           ab_lam_q1, ab_lam_k1, ab_lam_q2, ab_lam_k2, ab_subln_g, ab_w_out,
           cd_w_in, cd_sinks, cd_dw_w, cd_dw_b, cd_ln_g, cd_ln_b, cd_w_out, final_g):
    bsz, seq, d = x.shape
    mods = _ada_mods(c, ada_w, ada_b)
    mod = lambda l, k: mods[l, :, k * d:(k + 1) * d].reshape(bsz, 1, d)
    pos = _key_positions(seq)
    qa, ka, vat, cq, ckv, ckvt, kidx, widxt, sg0 = _proj0(
        x, mod(0, 0), mod(0, 1), norm_g[0], ab_w_in[0], ab_q_norm_g[0], ab_kv_norm_g[0])
    oa = _diff_attn(qa, ka, vat, pos, ab_lam_q1[0], ab_lam_k1[0], ab_lam_q2[0], ab_lam_k2[0], ab_subln_g[0], 0)
    ob = _dsa(cq, ab_w_uq[0], ab_w_qidx[0], widxt, ckv, ckvt, kidx, pos, ab_w_uv[0])
    x1, qc, kc, vct, hglu, sg1 = _mid(oa, ob, sg0, x, mod(0, 2), ab_w_out[0], mod(1, 0), mod(1, 1), norm_g[1],
                                     cd_w_in[0])
    oc = _swa(qc, kc, vct, pos, cd_sinks[0])
    od = _conv(hglu, cd_dw_w[0], cd_dw_b[0], cd_ln_g[0], cd_ln_b[0])
    return _final(oc, od, sg1, x1, mod(1, 2), cd_w_out[0], final_g)
```

```python
import functools
import math

import numpy as np
import jax
import jax.numpy as jnp
from jax import lax
from jax.experimental import pallas as pl
from jax.experimental.pallas import tpu as pltpu

F32 = jnp.float32
BF16 = jnp.bfloat16
I32 = jnp.int32

D_MODEL = 1024
EPS = 1e-6
A_HEADS = 4
A_DH = 64
A_DV = 128
B_HEADS = 8
B_DQLAT = 128
B_DLAT = 128
B_DV = 64
IDX_HEADS = 8
IDX_DH = 32
TOPK_MAX = 256
C_HEADS = 8
C_KV_HEADS = 2
C_DH = 64
WINDOW = 128
D_CH = 512
CONV_W = 31

LANES = 128
SUBLANES = 8
INT_MIN = -2 ** 31
ORDER_OF_NEG_INF = INT_MIN + 0x7FFFFF
NEG_INF = float("-inf")
VMEM_LIMIT = 48 * 1024 * 1024

ROW_TILE = 1024
LAYER1_TILE = 512
KEY_CHUNK = 256
A_TQ = 256
B_TQ = 128
POS_SPLIT = 16
ONES_ROWS = 16
LOG2E = math.log2(math.e)
COEF_TERMS = 4
CONV_HALO = 32


def _alibi(n):
    return [float(2.0 ** (-8.0 * i / n)) for i in range(1, n + 1)]


def _params(sem):
    return pltpu.CompilerParams(dimension_semantics=sem, vmem_limit_bytes=VMEM_LIMIT)


def _nt_dot(a, b):
    return lax.dot_general(a, b, (((1,), (1,)), ((), ())), preferred_element_type=F32)


def _dot(a, b):
    return jnp.dot(a, b, preferred_element_type=F32)


def _rms(x, g):
    return x * lax.rsqrt(jnp.mean(x * x, axis=-1, keepdims=True) + EPS) * g


def _bf16_terms(x, n):
    terms = []
    for _ in range(n):
        t = float(np.asarray(x, np.float32).astype(jnp.bfloat16).astype(np.float32))
        terms.append(t)
        x -= t
    return terms


def _alibi_coef(rows, slope):
    lane = lax.broadcasted_iota(I32, (rows, LANES), 1)
    out = jnp.zeros((rows, LANES), F32)
    for i, t in enumerate(_bf16_terms(LOG2E * slope, COEF_TERMS)):
        out = jnp.where(lane == 2 * i, POS_SPLIT * t, jnp.where(lane == 2 * i + 1, t, out))
    return out.astype(BF16)


def _key_positions(seq):
    pos = np.zeros((seq, LANES), np.float32)
    for i in range(COEF_TERMS):
        pos[:, 2 * i] = np.arange(seq) // POS_SPLIT
        pos[:, 2 * i + 1] = np.arange(seq) % POS_SPLIT
    return jnp.asarray(pos, BF16)


def _ada_kernel(c_ref, w_ref, b_ref, o_ref):
    c = c_ref[...]
    sc = c * jax.nn.sigmoid(c)
    o_ref[0] = jnp.dot(sc.astype(BF16), w_ref[0].astype(BF16), preferred_element_type=F32) + b_ref[0]


def _ada_mods(c, ada_w, ada_b):
    depth, d, d3 = ada_w.shape
    bsz = c.shape[0]
    nt = d3 // d
    return pl.pallas_call(
        _ada_kernel,
        grid=(depth, nt),
        in_specs=[pl.BlockSpec((bsz, d), lambda l, j: (0, 0)),
                  pl.BlockSpec((1, d, d), lambda l, j: (l, 0, j)),
                  pl.BlockSpec((1, 1, d), lambda l, j: (l, 0, j))],
        out_specs=pl.BlockSpec((1, bsz, d), lambda l, j: (l, 0, j)),
        out_shape=jax.ShapeDtypeStruct((depth, bsz, d3), F32),
        compiler_params=_params(("arbitrary", "arbitrary")),
        name="ada_mods",
    )(c, ada_w, ada_b.reshape(depth, 1, d3))


def _proj0_kernel(x_ref, shift_ref, scale_ref, ng_ref, wmain_ref, wgate_ref, wkidx_ref, wwidx_ref,
                  qng_ref, kvng_ref,
                  qa_ref, ka_ref, vat_ref, cq_ref, ckv_ref, ckvt_ref, kidx_ref, widxt_ref, sg_ref):
    x = x_ref[0]
    h = _rms(x, ng_ref[...]) * (1.0 + scale_ref[0]) + shift_ref[0]
    hb = h.astype(BF16)
    main = jnp.dot(hb, wmain_ref[...], preferred_element_type=F32)
    qw = A_HEADS * 2 * A_DH
    qa_ref[0] = (main[:, :qw] * (A_DH ** -0.5 * LOG2E)).astype(BF16)
    ka_ref[0] = main[:, qw:2 * qw].astype(BF16)
    va = main[:, 2 * qw:3 * qw]
    cq_ref[0] = _rms(main[:, 3 * qw:3 * qw + B_DQLAT], qng_ref[...]).astype(BF16)
    ckv = _rms(main[:, 3 * qw + B_DQLAT:], kvng_ref[...])
    ckv_ref[0] = ckv.astype(BF16)
    ch = vat_ref.shape[3]
    ones = jnp.ones((ONES_ROWS, ch), F32)
    for c in range(vat_ref.shape[1]):
        vt = va[c * ch:(c + 1) * ch].T
        vat_ref[0, c] = jnp.concatenate(
            [blk for h in range(A_HEADS) for blk in (vt[h * A_DV:(h + 1) * A_DV], ones)], axis=0).astype(BF16)
        ckvt_ref[0, c] = jnp.concatenate([ckv[c * ch:(c + 1) * ch].T, ones], axis=0).astype(BF16)
    kidx_ref[0] = jnp.dot(hb, wkidx_ref[...], preferred_element_type=F32).astype(BF16)
    widxt_ref[0] = _nt_dot(wwidx_ref[...], hb)
    gate = jnp.dot(hb, wgate_ref[...], preferred_element_type=F32)
    sg_ref[0] = (gate * jax.nn.sigmoid(gate)).astype(BF16)


def _proj0(x, shift, scale, norm_g, w_in, q_norm_g, kv_norm_g):
    bsz, seq, d = x.shape
    t = min(ROW_TILE, seq)
    ch = min(KEY_CHUNK, seq)
    qw = A_HEADS * 2 * A_DH
    o_main = 3 * qw + B_DQLAT + B_DLAT
    wmain = w_in[:, :o_main].astype(BF16)
    wkidx = jnp.tile(w_in[:, o_main:o_main + IDX_DH], (1, IDX_HEADS)).astype(BF16)
    wwidx = w_in[:, o_main + IDX_DH:o_main + IDX_DH + IDX_HEADS].T.astype(BF16)
    wgate = w_in[:, o_main + IDX_DH + IDX_HEADS:].astype(BF16)
    row = lambda w: pl.BlockSpec((1, t, w), lambda b, i: (b, i, 0))
    rows = lambda w: (jax.ShapeDtypeStruct((bsz, seq, w), BF16), row(w))
    chunked = lambda w: (jax.ShapeDtypeStruct((bsz, seq // ch, w, ch), BF16),
                         pl.BlockSpec((1, t // ch, w, ch), lambda b, i: (b, i, 0, 0)))
    mod = pl.BlockSpec((1, 1, d), lambda b, i: (b, 0, 0))
    full = lambda a: pl.BlockSpec(a.shape, lambda b, i: (0,) * a.ndim)
    ng = norm_g.reshape(1, d)
    qng = q_norm_g.reshape(1, B_DQLAT)
    kvng = kv_norm_g.reshape(1, B_DLAT)
    iw = IDX_HEADS * IDX_DH
    outs = [rows(qw), rows(qw), chunked(A_HEADS * (A_DV + ONES_ROWS)), rows(B_DQLAT), rows(B_DLAT),
            chunked(B_DLAT + ONES_ROWS), rows(iw),
            (jax.ShapeDtypeStruct((bsz, IDX_HEADS, seq), F32),
             pl.BlockSpec((1, IDX_HEADS, t), lambda b, i: (b, 0, i))),
            rows(d)]
    return pl.pallas_call(
        _proj0_kernel,
        grid=(bsz, seq // t),
        in_specs=[row(d), mod, mod, full(ng), full(wmain), full(wgate), full(wkidx), full(wwidx),
                  full(qng), full(kvng)],
        out_specs=[o[1] for o in outs],
        out_shape=[o[0] for o in outs],
        compiler_params=_params(("arbitrary", "arbitrary")),
        name="proj0",
    )(x, shift, scale, ng, wmain, wgate, wkidx, wwidx, qng, kvng)


def _diff_attn_kernel(q_ref, k_ref, vt_ref, pos_ref, lq1_ref, lk1_ref, lq2_ref, lk2_ref, g_ref, o_ref,
                      *, lam_init):
    i = pl.program_id(1)
    tq = q_ref.shape[1]
    tk = vt_ref.shape[3]
    lam = (jnp.exp(jnp.sum(lq1_ref[...] * lk1_ref[...], keepdims=True))
           - jnp.exp(jnp.sum(lq2_ref[...] * lk2_ref[...], keepdims=True)) + lam_init)
    krow = lax.broadcasted_iota(I32, (tk, 2 * tq), 0)
    qcol = lax.broadcasted_iota(I32, (tk, 2 * tq), 1) % tq
    causal = krow <= qcol
    slopes = _alibi(A_HEADS)
    heads = [slice(h * A_DV, (h + 1) * A_DV) for h in range(A_HEADS)]
    first = lax.broadcasted_iota(I32, (tq, A_DV), 1) < A_DH
    zero = jnp.zeros((tq, A_DV), BF16)
    qs = [jnp.concatenate(
        [jnp.concatenate([jnp.where(first, q_ref[0, :, hs], zero), _alibi_coef(tq, slopes[h])], axis=1),
         jnp.concatenate([jnp.where(first, zero, q_ref[0, :, hs]), _alibi_coef(tq, slopes[h])], axis=1)], axis=0)
          for h, hs in enumerate(heads)]

    rows_v = vt_ref.shape[2] // A_HEADS
    vrows = [slice(h * rows_v, (h + 1) * rows_v) for h in range(A_HEADS)]

    def variant(n):
        rows = n * tk
        pos = pos_ref[:rows, :]
        for h, hs in enumerate(heads):
            s = _nt_dot(jnp.concatenate([k_ref[0, :rows, hs], pos], axis=1), qs[h])
            last = jnp.where(causal, s[rows - tk:], NEG_INF)
            s = last if n == 1 else jnp.concatenate([s[:rows - tk], last], axis=0)
            m = jnp.max(s, axis=0, keepdims=True)
            vt = jnp.concatenate([vt_ref[0, c, vrows[h], :] for c in range(n)], axis=1)
            acc = _dot(vt, jnp.exp2(s - m).astype(BF16))
            o = acc[:A_DV] * (1.0 / acc[A_DV:A_DV + 1])
            od = (o[:, :tq] - lam * o[:, tq:]).T
            o_ref[0, :, hs] = (_rms(od, g_ref[...]) * (1.0 - lam_init)).astype(o_ref.dtype)

    for n in range(1, vt_ref.shape[1] + 1):
        pl.when(i + 1 == n)(functools.partial(variant, n))


def _diff_attn(qa, ka, vat, pos, lam_q1, lam_k1, lam_q2, lam_k2, subln_g, layer_idx):
    bsz, seq, w = qa.shape
    tq = vat.shape[3]
    lam_init = 0.8 - 0.6 * math.exp(-0.3 * layer_idx)
    vec = lambda a: a.reshape(1, -1)
    full = lambda a: pl.BlockSpec(a.shape, lambda b, i: (0,) * a.ndim)
    small = [vec(lam_q1), vec(lam_k1), vec(lam_q2), vec(lam_k2), vec(subln_g)]
    return pl.pallas_call(
        functools.partial(_diff_attn_kernel, lam_init=lam_init),
        grid=(bsz, seq // tq),
        in_specs=[pl.BlockSpec((1, tq, w), lambda b, i: (b, i, 0)),
                  pl.BlockSpec((1, seq, w), lambda b, i: (b, 0, 0)),
                  pl.BlockSpec((1,) + vat.shape[1:], lambda b, i: (b, 0, 0, 0)),
                  full(pos)] + [full(a) for a in small],
        out_specs=pl.BlockSpec((1, tq, w), lambda b, i: (b, i, 0)),
        out_shape=jax.ShapeDtypeStruct((bsz, seq, w), BF16),
        compiler_params=_params(("arbitrary", "arbitrary")),
        name="diff_attn",
    )(qa, ka, vat, pos, *small)


def _dsa_kernel(cq_ref, wuq_ref, wqidx_ref, wt_ref, kv_ref, kvt_ref, kidx_ref, pos_ref, wuv_ref, o_ref,
                key_ref, thr_ref, lim_ref, *, topk):
    qi = pl.program_id(1)
    tq = cq_ref.shape[1]
    kc_rows = kvt_ref.shape[3]
    nch = (qi * tq) // kc_rows + 1

    cq = cq_ref[0]
    qidx = _dot(cq, wqidx_ref[...])
    qlat = (_dot(cq, wuq_ref[...]) * (B_DLAT ** -0.5 * LOG2E)).astype(BF16)
    head_of_lane = lax.broadcasted_iota(I32, qidx.shape, 1) // IDX_DH
    qstack = jnp.concatenate([jnp.where(head_of_lane == h, qidx, 0.0) for h in range(IDX_HEADS)],
                             axis=0).astype(BF16)
    wt = wt_ref[0]
    slopes = _alibi(B_HEADS)
    group = 2
    ngroups = B_HEADS // group
    qs = jnp.concatenate(
        [jnp.concatenate([qlat[:, h * B_DLAT:(h + 1) * B_DLAT], _alibi_coef(tq, slopes[h])], axis=1)
         for h in range(B_HEADS)], axis=0)

    def to_float(u):
        s = jnp.maximum(u ^ INT_MIN, ORDER_OF_NEG_INF)
        return pltpu.bitcast(s ^ ((s >> 31) & 0x7FFFFFFF), F32)

    def variant(n):
        rows = n * kc_rows
        kpos = lax.broadcasted_iota(I32, (rows, tq), 0)
        causal = kpos <= qi * tq + lax.broadcasted_iota(I32, (rows, tq), 1)
        krow = kpos[:SUBLANES]

        rel = _nt_dot(kidx_ref[0, :rows, :], qstack)
        isc = jnp.zeros((rows, tq), F32)
        for h in range(IDX_HEADS):
            isc = isc + wt[h:h + 1, :] * jnp.maximum(rel[:, h * tq:(h + 1) * tq], 0.0)
        key_ref[:rows, :] = jnp.where(causal, isc, NEG_INF)

        def count(pred):
            accs = [jnp.zeros((SUBLANES, tq), I32) for _ in range(4)]
            for r in range(rows // SUBLANES):
                ind = pred(key_ref[r * SUBLANES:(r + 1) * SUBLANES, :], r * SUBLANES + krow)
                accs[r % 4] = accs[r % 4] + ind.astype(I32)
            return jnp.sum((accs[0] + accs[1]) + (accs[2] + accs[3]), axis=0, keepdims=True)

        thr_ref[...] = jnp.full((1, tq), NEG_INF, F32)
        lim_ref[...] = jnp.full((1, tq), 2 ** 30, I32)
        if rows > topk:
            def bit_step(t, cand):
                trial = cand | jnp.left_shift(jnp.int32(1), 31 - t)
                thr = to_float(trial)
                cnt = count(lambda x, p: x >= thr)
                return jnp.where(cnt >= topk, trial, cand)
            thr = to_float(lax.fori_loop(0, 32, bit_step, jnp.zeros((1, tq), I32)))
            thr_ref[...] = thr
            need = topk - count(lambda x, p: x > thr)
            n_eq = count(lambda x, p: x == thr)

            @pl.when(jnp.max((n_eq > need).astype(I32)) > 0)
            def _():
                def pos_step(t, lim):
                    trial = lim | jnp.left_shift(jnp.int32(1), 11 - t)
                    cnt = count(lambda x, p: (x == thr) & (p < trial))
                    return jnp.where(cnt < need, trial, lim)
                lim_ref[...] = lax.fori_loop(0, 12, pos_step, jnp.zeros((1, tq), I32))

        thr = thr_ref[...]
        lim = lim_ref[...]
        x = key_ref[:rows, :]
        sel = ((x > thr) | ((x == thr) & (kpos <= lim))) & causal
        bias = jnp.where(sel, 0.0, NEG_INF)
        bias = jnp.concatenate([bias] * group, axis=1)
        kaug = jnp.concatenate([kv_ref[0, :rows, :], pos_ref[:rows, :]], axis=1)
        kvt = jnp.concatenate([kvt_ref[0, c] for c in range(n)], axis=1)
        s_all = _nt_dot(kaug, qs)
        ps = []
        for g in range(ngroups):
            s = s_all[:, g * group * tq:(g + 1) * group * tq] + bias
            m = jnp.max(s, axis=0, keepdims=True)
            ps.append(jnp.exp2(s - m).astype(BF16))
        acc = _dot(kvt, jnp.concatenate(ps, axis=1))

        o = acc[:B_DLAT] * (1.0 / acc[B_DLAT:B_DLAT + 1])
        o_all = jnp.concatenate([o[:, h * tq:(h + 1) * tq].T for h in range(B_HEADS)], axis=1).astype(BF16)
        o_ref[0] = jnp.dot(o_all, wuv_ref[...], preferred_element_type=F32).astype(o_ref.dtype)

    for n in range(1, key_ref.shape[0] // kc_rows + 1):
        pl.when(nch == n)(functools.partial(variant, n))


def _dsa(cq, w_uq, w_qidx, widxt, ckv, ckvt, kidx, pos, w_uv):
    bsz, seq, _ = cq.shape
    wuq = w_uq.reshape(B_DQLAT, B_HEADS * B_DLAT).astype(BF16)
    wqidx = w_qidx.reshape(B_DQLAT, IDX_HEADS * IDX_DH).astype(BF16)
    tq = min(B_TQ, seq)
    topk = min(TOPK_MAX, seq // 4)
    eye = jnp.eye(B_HEADS, dtype=w_uv.dtype)
    wuv = jnp.einsum('hde,hg->hdge', w_uv, eye).reshape(B_HEADS * B_DLAT, B_HEADS * B_DV).astype(BF16)
    return pl.pallas_call(
        functools.partial(_dsa_kernel, topk=topk),
        grid=(bsz, seq // tq),
        in_specs=[pl.BlockSpec((1, tq, B_DQLAT), lambda b, i: (b, i, 0)),
                  pl.BlockSpec(wuq.shape, lambda b, i: (0, 0)),
                  pl.BlockSpec(wqidx.shape, lambda b, i: (0, 0)),
                  pl.BlockSpec((1, IDX_HEADS, tq), lambda b, i: (b, 0, i)),
                  pl.BlockSpec((1, seq, B_DLAT), lambda b, i: (b, 0, 0)),
                  pl.BlockSpec((1,) + ckvt.shape[1:], lambda b, i: (b, 0, 0, 0)),
                  pl.BlockSpec((1, seq, IDX_HEADS * IDX_DH), lambda b, i: (b, 0, 0)),
                  pl.BlockSpec(pos.shape, lambda b, i: (0, 0)),
                  pl.BlockSpec(wuv.shape, lambda b, i: (0, 0))],
        out_specs=pl.BlockSpec((1, tq, B_HEADS * B_DV), lambda b, i: (b, i, 0)),
        out_shape=jax.ShapeDtypeStruct((bsz, seq, B_HEADS * B_DV), BF16),
        scratch_shapes=[pltpu.VMEM((seq, tq), F32),
                        pltpu.VMEM((1, tq), F32), pltpu.VMEM((1, tq), I32)],
        compiler_params=_params(("arbitrary", "arbitrary")),
        name="dsa",
    )(cq, wuq, wqidx, widxt, ckv, ckvt, kidx, pos, wuv)


def _swa_block(q, kp, kc, vtp, vtc, pos, sink_ref, mask):
    w = q.shape[0]
    rep = C_HEADS // C_KV_HEADS
    half = LANES // 2
    lane = lax.broadcasted_iota(I32, (w, LANES), 1)
    slopes = _alibi(C_HEADS)
    qs = []
    for h in range(C_HEADS):
        g = h // rep
        x = q[:, (h // 2) * LANES:(h // 2 + 1) * LANES].astype(F32)
        if (h % 2) != g:
            x = pltpu.roll(x, half, axis=1)
        x = jnp.where((lane // half) == g, x, 0.0).astype(BF16)
        qs.append(jnp.concatenate([x, _alibi_coef(w, slopes[h])], axis=1))
    kaug = jnp.concatenate([jnp.concatenate([kp, kc], axis=0), pos], axis=1)
    vt = jnp.concatenate([vtp, vtc], axis=1)
    s_all = _nt_dot(kaug, jnp.concatenate(qs, axis=0))
    ps, rs = [], []
    for h in range(C_HEADS):
        s = s_all[:, h * w:(h + 1) * w] + mask
        sink = LOG2E * (sink_ref[h:h + 1, :] + slopes[h] * (w + lane[:1, :]).astype(F32))
        m = jnp.maximum(jnp.max(s, axis=0, keepdims=True), sink)
        p = jnp.exp2(s - m)
        rs.append(1.0 / (jnp.sum(p, axis=0, keepdims=True) + jnp.exp2(sink - m)))
        ps.append(p.astype(BF16))
    o = _dot(vt, jnp.concatenate(ps, axis=1)) * jnp.concatenate(rs, axis=1)
    ot = [o[(h // rep) * C_DH:(h // rep + 1) * C_DH, h * w:(h + 1) * w] for h in range(C_HEADS)]
    return jnp.concatenate(ot, axis=0).T


def _layer1_kernel(oa_ref, ob_ref, sg_ref, x_ref, gmod0_ref, wout0_ref, shift_ref, scale_ref, ng_ref, win_ref,
                   pos_ref, sink_ref, dww_ref, dwb_ref, lng_ref, lnb_ref, gmod1_ref, wout1_ref, fg_ref,
                   o_ref, mask_ref, kprev_ref, vtprev_ref, buf_ref, shift_buf_ref):
    b = pl.program_id(0)
    i = pl.program_id(1)
    t = x_ref.shape[1]
    w = WINDOW
    halo = CONV_HALO

    @pl.when((b == 0) & (i == 0))
    def _():
        k = lax.broadcasted_iota(I32, (2 * w, w), 0)
        q = lax.broadcasted_iota(I32, (2 * w, w), 1)
        dist = w + q - k
        valid = (dist >= 0) & (dist < w)
        mask_ref[1] = jnp.where(valid, 0.0, NEG_INF)
        mask_ref[0] = jnp.where(valid & (k >= w), 0.0, NEG_INF)
        kprev_ref[...] = jnp.zeros(kprev_ref.shape, kprev_ref.dtype)
        vtprev_ref[...] = jnp.zeros(vtprev_ref.shape, vtprev_ref.dtype)
        buf_ref[:halo, :] = jnp.zeros((halo, buf_ref.shape[1]), F32)

    y = jnp.concatenate([oa_ref[0], ob_ref[0]], axis=1).astype(F32) * sg_ref[0].astype(F32)
    x1 = x_ref[0] + gmod0_ref[0] * _dot(y.astype(BF16), wout0_ref[...])
    h = _rms(x1, ng_ref[...]) * (1.0 + scale_ref[0]) + shift_ref[0]
    p = _dot(h.astype(BF16), win_ref[...])
    qw = C_HEADS * C_DH
    kw = C_KV_HEADS * C_DH
    qc = (p[:, :qw] * (C_DH ** -0.5 * LOG2E)).astype(BF16)
    kc = p[:, qw:qw + kw].astype(BF16)
    vct = p[:, qw + kw:qw + 2 * kw].T.astype(BF16)
    off = qw + 2 * kw
    hglu = p[:, off:off + D_CH] * jax.nn.sigmoid(p[:, off + D_CH:off + 2 * D_CH])
    gate = p[:, off + 2 * D_CH:]
    sg1 = gate * jax.nn.sigmoid(gate)

    oc = []
    for j in range(t // w):
        kp = kprev_ref[...] if j == 0 else kc[(j - 1) * w:j * w]
        vtp = vtprev_ref[...] if j == 0 else vct[:, (j - 1) * w:j * w]
        mask = mask_ref[jnp.minimum(i, 1)] if j == 0 else mask_ref[1]
        oc.append(_swa_block(qc[j * w:(j + 1) * w], kp, kc[j * w:(j + 1) * w], vtp, vct[:, j * w:(j + 1) * w],
                             pos_ref[...], sink_ref, mask))
    kprev_ref[...] = kc[t - w:]
    vtprev_ref[...] = vct[:, t - w:]

    buf_ref[:halo, :] = jnp.where(i > 0, buf_ref[:halo, :], 0.0)
    buf_ref[halo:, :] = hglu
    span = shift_buf_ref.shape[1]
    for ph in range(1, SUBLANES):
        shift_buf_ref[ph - 1] = buf_ref[ph:ph + span, :]
    acc = jnp.zeros((t, D_CH), F32) + dwb_ref[...]
    first = halo - (CONV_W - 1)
    for j in range(CONV_W):
        base, ph = (first + j) // SUBLANES * SUBLANES, (first + j) % SUBLANES
        rows = buf_ref[base:base + t, :] if ph == 0 else shift_buf_ref[ph - 1, base:base + t, :]
        acc = acc + dww_ref[j:j + 1, :] * rows
    buf_ref[:halo, :] = hglu[t - halo:]
    mu = jnp.mean(acc, axis=-1, keepdims=True)
    xc = acc - mu
    var = jnp.mean(xc * xc, axis=-1, keepdims=True)
    yn = xc * lax.rsqrt(var + EPS) * lng_ref[...] + lnb_ref[...]
    od = yn * jax.nn.sigmoid(yn)

    y1 = jnp.concatenate([jnp.concatenate(oc, axis=0), od], axis=1) * sg1
    x2 = x1 + gmod1_ref[0] * _dot(y1.astype(BF16), wout1_ref[...])
    o_ref[0] = _rms(x2, fg_ref[...])


def _layer1(oa, ob, sg, x, gmod0, w_out0, shift, scale, norm_g, w_in, pos, sinks, dw_w, dw_b, ln_g, ln_b,
            gmod1, w_out1, final_g):
    bsz, seq, d = x.shape
    t = min(LAYER1_TILE, seq)
    w = WINDOW
    kw = C_KV_HEADS * C_DH
    wout0 = w_out0.astype(BF16)
    win = w_in.astype(BF16)
    wout1 = w_out1.astype(BF16)
    vec = lambda a: a.reshape(1, -1)
    sink_rows = jnp.broadcast_to(sinks.astype(F32)[:, None], (C_HEADS, w))
    row = lambda width: pl.BlockSpec((1, t, width), lambda b, i: (b, i, 0))
    mod = pl.BlockSpec((1, 1, d), lambda b, i: (b, 0, 0))
    full = lambda a: pl.BlockSpec(a.shape, lambda b, i: (0,) * a.ndim)
    consts = [vec(norm_g), win]
    tail = [sink_rows, dw_w, vec(dw_b), vec(ln_g), vec(ln_b)]
    return pl.pallas_call(
        _layer1_kernel,
        grid=(bsz, seq // t),
        in_specs=[row(oa.shape[2]), row(ob.shape[2]), row(d), row(d), mod, full(wout0), mod, mod]
                 + [full(a) for a in consts]
                 + [pl.BlockSpec((2 * w, LANES), lambda b, i: (0, 0))]
                 + [full(a) for a in tail]
                 + [mod, full(wout1), full(vec(final_g))],
        out_specs=row(d),
        out_shape=jax.ShapeDtypeStruct((bsz, seq, d), F32),
        scratch_shapes=[pltpu.VMEM((2, 2 * w, w), F32),
                        pltpu.VMEM((w, kw), BF16), pltpu.VMEM((kw, w), BF16),
                        pltpu.VMEM((CONV_HALO + t, D_CH), F32),
                        pltpu.VMEM((SUBLANES - 1, CONV_HALO - SUBLANES + t, D_CH), F32)],
        compiler_params=_params(("arbitrary", "arbitrary")),
        name="layer1",
    )(oa, ob, sg, x, gmod0, wout0, shift, scale, *consts, pos, *tail, gmod1, wout1, vec(final_g))


def kernel(x, c, ada_w, ada_b, norm_g, ab_w_in, ab_q_norm_g, ab_kv_norm_g, ab_w_uq, ab_w_qidx, ab_w_uv,
           ab_lam_q1, ab_lam_k1, ab_lam_q2, ab_lam_k2, ab_subln_g, ab_w_out,
           cd_w_in, cd_sinks, cd_dw_w, cd_dw_b, cd_ln_g, cd_ln_b, cd_w_out, final_g):
    bsz, seq, d = x.shape
    mods = _ada_mods(c, ada_w, ada_b)
    mod = lambda l, k: mods[l, :, k * d:(k + 1) * d].reshape(bsz, 1, d)
    pos = _key_positions(seq)
    qa, ka, vat, cq, ckv, ckvt, kidx, widxt, sg0 = _proj0(
        x, mod(0, 0), mod(0, 1), norm_g[0], ab_w_in[0], ab_q_norm_g[0], ab_kv_norm_g[0])
    oa = _diff_attn(qa, ka, vat, pos, ab_lam_q1[0], ab_lam_k1[0], ab_lam_q2[0], ab_lam_k2[0], ab_subln_g[0], 0)
    ob = _dsa(cq, ab_w_uq[0], ab_w_qidx[0], widxt, ckv, ckvt, kidx, pos, ab_w_uv[0])
    return _layer1(oa, ob, sg0, x, mod(0, 2), ab_w_out[0], mod(1, 0), mod(1, 1), norm_g[1], cd_w_in[0], pos,
                   cd_sinks[0], cd_dw_w[0], cd_dw_b[0], cd_ln_g[0], cd_ln_b[0], mod(1, 2), cd_w_out[0], final_g)
```

```python
import functools
import math

import numpy as np
import jax
import jax.numpy as jnp
from jax import lax
from jax.experimental import pallas as pl
from jax.experimental.pallas import tpu as pltpu

F32 = jnp.float32
BF16 = jnp.bfloat16
I32 = jnp.int32

EPS = 1e-6
A_HEADS = 4
A_DH = 64
A_DV = 128
B_HEADS = 8
B_DQLAT = 128
B_DLAT = 128
B_DV = 64
IDX_HEADS = 8
IDX_DH = 32
TOPK_MAX = 256
C_HEADS = 8
C_KV_HEADS = 2
C_DH = 64
WINDOW = 128
D_CH = 512
CONV_W = 31

LANES = 128
SUBLANES = 8
INT_MIN = -2 ** 31
ORDER_OF_NEG_INF = INT_MIN + 0x7FFFFF
NEG_INF = float("-inf")
VMEM_LIMIT = 48 * 1024 * 1024

ROW_TILE = 1024
LAYER1_TILE = 512
KEY_CHUNK = 256
B_TQ = 128
POS_SPLIT = 16
ONES_ROWS = 16
LOG2E = math.log2(math.e)
COEF_TERMS = 4
CONV_HALO = 32


def _alibi(n):
    return [float(2.0 ** (-8.0 * i / n)) for i in range(1, n + 1)]


def _params(sem):
    return pltpu.CompilerParams(dimension_semantics=sem, vmem_limit_bytes=VMEM_LIMIT)


def _nt_dot(a, b):
    return lax.dot_general(a, b, (((1,), (1,)), ((), ())), preferred_element_type=F32)


def _dot(a, b):
    return jnp.dot(a, b, preferred_element_type=F32)


def _rms(x, g):
    return x * lax.rsqrt(jnp.mean(x * x, axis=-1, keepdims=True) + EPS) * g


def _bf16_terms(x, n):
    terms = []
    for _ in range(n):
        t = float(np.asarray(x, np.float32).astype(jnp.bfloat16).astype(np.float32))
        terms.append(t)
        x -= t
    return terms


def _alibi_coef(rows, slope):
    lane = lax.broadcasted_iota(I32, (rows, LANES), 1)
    out = jnp.zeros((rows, LANES), F32)
    for i, t in enumerate(_bf16_terms(LOG2E * slope, COEF_TERMS)):
        out = jnp.where(lane == 2 * i, POS_SPLIT * t, jnp.where(lane == 2 * i + 1, t, out))
    return out.astype(BF16)


def _key_positions(seq):
    pos = np.zeros((seq, LANES), np.float32)
    for i in range(COEF_TERMS):
        pos[:, 2 * i] = np.arange(seq) // POS_SPLIT
        pos[:, 2 * i + 1] = np.arange(seq) % POS_SPLIT
    return jnp.asarray(pos, BF16)


def _ada_kernel(c_ref, w_ref, b_ref, o_ref):
    c = c_ref[...]
    sc = c * jax.nn.sigmoid(c)
    o_ref[0] = jnp.dot(sc.astype(BF16), w_ref[0].astype(BF16), preferred_element_type=F32) + b_ref[0]


def _ada_mods(c, ada_w, ada_b):
    depth, d, d3 = ada_w.shape
    bsz = c.shape[0]
    nt = d3 // d
    return pl.pallas_call(
        _ada_kernel,
        grid=(depth, nt),
        in_specs=[pl.BlockSpec((bsz, d), lambda l, j: (0, 0)),
                  pl.BlockSpec((1, d, d), lambda l, j: (l, 0, j)),
                  pl.BlockSpec((1, 1, d), lambda l, j: (l, 0, j))],
        out_specs=pl.BlockSpec((1, bsz, d), lambda l, j: (l, 0, j)),
        out_shape=jax.ShapeDtypeStruct((depth, bsz, d3), F32),
        compiler_params=_params(("arbitrary", "arbitrary")),
        name="ada_mods",
    )(c, ada_w, ada_b.reshape(depth, 1, d3))


def _proj0_kernel(x_ref, shift_ref, scale_ref, ng_ref, wmain_ref, wgate_ref, wkidx_ref, wwidx_ref,
                  qng_ref, kvng_ref,
                  qa_ref, ka_ref, vat_ref, cq_ref, ckv_ref, ckvt_ref, kidx_ref, widxt_ref, sg_ref):
    x = x_ref[0]
    h = _rms(x, ng_ref[...]) * (1.0 + scale_ref[0]) + shift_ref[0]
    hb = h.astype(BF16)
    main = jnp.dot(hb, wmain_ref[...], preferred_element_type=F32)
    qw = A_HEADS * 2 * A_DH
    qa_ref[0] = (main[:, :qw] * (A_DH ** -0.5 * LOG2E)).astype(BF16)
    ka_ref[0] = main[:, qw:2 * qw].astype(BF16)
    va = main[:, 2 * qw:3 * qw]
    cq_ref[0] = _rms(main[:, 3 * qw:3 * qw + B_DQLAT], qng_ref[...]).astype(BF16)
    ckv = _rms(main[:, 3 * qw + B_DQLAT:], kvng_ref[...])
    ckv_ref[0] = ckv.astype(BF16)
    ch = vat_ref.shape[3]
    ones = jnp.ones((ONES_ROWS, ch), F32)
    for c in range(vat_ref.shape[1]):
        vt = va[c * ch:(c + 1) * ch].T
        vat_ref[0, c] = jnp.concatenate(
            [blk for h in range(A_HEADS) for blk in (vt[h * A_DV:(h + 1) * A_DV], ones)], axis=0).astype(BF16)
        ckvt_ref[0, c] = jnp.concatenate([ckv[c * ch:(c + 1) * ch].T, ones], axis=0).astype(BF16)
    kidx_ref[0] = jnp.dot(hb, wkidx_ref[...], preferred_element_type=F32).astype(BF16)
    widxt_ref[0] = _nt_dot(wwidx_ref[...], hb)
    gate = jnp.dot(hb, wgate_ref[...], preferred_element_type=F32)
    sg_ref[0] = (gate * jax.nn.sigmoid(gate)).astype(BF16)


def _proj0(x, shift, scale, norm_g, w_in, q_norm_g, kv_norm_g):
    bsz, seq, d = x.shape
    t = min(ROW_TILE, seq)
    ch = min(KEY_CHUNK, seq)
    qw = A_HEADS * 2 * A_DH
    o_main = 3 * qw + B_DQLAT + B_DLAT
    wmain = w_in[:, :o_main].astype(BF16)
    wkidx = jnp.tile(w_in[:, o_main:o_main + IDX_DH], (1, IDX_HEADS)).astype(BF16)
    wwidx = w_in[:, o_main + IDX_DH:o_main + IDX_DH + IDX_HEADS].T.astype(BF16)
    wgate = w_in[:, o_main + IDX_DH + IDX_HEADS:].astype(BF16)
    row = lambda w: pl.BlockSpec((1, t, w), lambda b, i: (b, i, 0))
    rows = lambda w: (jax.ShapeDtypeStruct((bsz, seq, w), BF16), row(w))
    chunked = lambda w: (jax.ShapeDtypeStruct((bsz, seq // ch, w, ch), BF16),
                         pl.BlockSpec((1, t // ch, w, ch), lambda b, i: (b, i, 0, 0)))
    mod = pl.BlockSpec((1, 1, d), lambda b, i: (b, 0, 0))
    full = lambda a: pl.BlockSpec(a.shape, lambda b, i: (0,) * a.ndim)
    ng = norm_g.reshape(1, d)
    qng = q_norm_g.reshape(1, B_DQLAT)
    kvng = kv_norm_g.reshape(1, B_DLAT)
    iw = IDX_HEADS * IDX_DH
    outs = [rows(qw), rows(qw), chunked(A_HEADS * (A_DV + ONES_ROWS)), rows(B_DQLAT), rows(B_DLAT),
            chunked(B_DLAT + ONES_ROWS), rows(iw),
            (jax.ShapeDtypeStruct((bsz, IDX_HEADS, seq), F32),
             pl.BlockSpec((1, IDX_HEADS, t), lambda b, i: (b, 0, i))),
            rows(d)]
    return pl.pallas_call(
        _proj0_kernel,
        grid=(bsz, seq // t),
        in_specs=[row(d), mod, mod, full(ng), full(wmain), full(wgate), full(wkidx), full(wwidx),
                  full(qng), full(kvng)],
        out_specs=[o[1] for o in outs],
        out_shape=[o[0] for o in outs],
        compiler_params=_params(("arbitrary", "arbitrary")),
        name="proj0",
    )(x, shift, scale, ng, wmain, wgate, wkidx, wwidx, qng, kvng)


def _diff_attn_kernel(q_ref, k_ref, vt_ref, pos_ref, lq1_ref, lk1_ref, lq2_ref, lk2_ref, g_ref, o_ref,
                      *, lam_init):
    i = pl.program_id(1)
    tq = q_ref.shape[1]
    tk = vt_ref.shape[3]
    lam = (jnp.exp(jnp.sum(lq1_ref[...] * lk1_ref[...], keepdims=True))
           - jnp.exp(jnp.sum(lq2_ref[...] * lk2_ref[...], keepdims=True)) + lam_init)
    krow = lax.broadcasted_iota(I32, (tk, 2 * tq), 0)
    qcol = lax.broadcasted_iota(I32, (tk, 2 * tq), 1) % tq
    causal = krow <= qcol
    slopes = _alibi(A_HEADS)
    heads = [slice(h * A_DV, (h + 1) * A_DV) for h in range(A_HEADS)]
    first = lax.broadcasted_iota(I32, (tq, A_DV), 1) < A_DH
    zero = jnp.zeros((tq, A_DV), BF16)
    qs = [jnp.concatenate(
        [jnp.concatenate([jnp.where(first, q_ref[0, :, hs], zero), _alibi_coef(tq, slopes[h])], axis=1),
         jnp.concatenate([jnp.where(first, zero, q_ref[0, :, hs]), _alibi_coef(tq, slopes[h])], axis=1)], axis=0)
          for h, hs in enumerate(heads)]

    rows_v = vt_ref.shape[2] // A_HEADS
    vrows = [slice(h * rows_v, (h + 1) * rows_v) for h in range(A_HEADS)]

    def variant(n):
        rows = n * tk
        pos = pos_ref[:rows, :]
        for h, hs in enumerate(heads):
            s = _nt_dot(jnp.concatenate([k_ref[0, :rows, hs], pos], axis=1), qs[h])
            last = jnp.where(causal, s[rows - tk:], NEG_INF)
            s = last if n == 1 else jnp.concatenate([s[:rows - tk], last], axis=0)
            m = jnp.max(s, axis=0, keepdims=True)
            vt = jnp.concatenate([vt_ref[0, c, vrows[h], :] for c in range(n)], axis=1)
            acc = _dot(vt, jnp.exp2(s - m).astype(BF16))
            o = acc[:A_DV] * (1.0 / acc[A_DV:A_DV + 1])
            od = (o[:, :tq] - lam * o[:, tq:]).T
            o_ref[0, :, hs] = (_rms(od, g_ref[...]) * (1.0 - lam_init)).astype(o_ref.dtype)

    for n in range(1, vt_ref.shape[1] + 1):
        pl.when(i + 1 == n)(functools.partial(variant, n))


def _diff_attn(qa, ka, vat, pos, lam_q1, lam_k1, lam_q2, lam_k2, subln_g, layer_idx):
    bsz, seq, w = qa.shape
    tq = vat.shape[3]
    lam_init = 0.8 - 0.6 * math.exp(-0.3 * layer_idx)
    vec = lambda a: a.reshape(1, -1)
    full = lambda a: pl.BlockSpec(a.shape, lambda b, i: (0,) * a.ndim)
    small = [vec(lam_q1), vec(lam_k1), vec(lam_q2), vec(lam_k2), vec(subln_g)]
    return pl.pallas_call(
        functools.partial(_diff_attn_kernel, lam_init=lam_init),
        grid=(bsz, seq // tq),
        in_specs=[pl.BlockSpec((1, tq, w), lambda b, i: (b, i, 0)),
                  pl.BlockSpec((1, seq, w), lambda b, i: (b, 0, 0)),
                  pl.BlockSpec((1,) + vat.shape[1:], lambda b, i: (b, 0, 0, 0)),
                  full(pos)] + [full(a) for a in small],
        out_specs=pl.BlockSpec((1, tq, w), lambda b, i: (b, i, 0)),
        out_shape=jax.ShapeDtypeStruct((bsz, seq, w), BF16),
        compiler_params=_params(("arbitrary", "arbitrary")),
        name="diff_attn",
    )(qa, ka, vat, pos, *small)


def _dsa_kernel(cq_ref, wuq_ref, wqidx_ref, wt_ref, kv_ref, kvt_ref, kidx_ref, pos_ref, wuv_ref, o_ref,
                key_ref, thr_ref, lim_ref, *, topk):
    qi = pl.program_id(1)
    tq = cq_ref.shape[1]
    kc_rows = kvt_ref.shape[3]
    nch = (qi * tq) // kc_rows + 1

    cq = cq_ref[0]
    qidx = _dot(cq, wqidx_ref[...])
    qlat = (_dot(cq, wuq_ref[...]) * (B_DLAT ** -0.5 * LOG2E)).astype(BF16)
    head_of_lane = lax.broadcasted_iota(I32, qidx.shape, 1) // IDX_DH
    qstack = jnp.concatenate([jnp.where(head_of_lane == h, qidx, 0.0) for h in range(IDX_HEADS)],
                             axis=0).astype(BF16)
    wt = wt_ref[0]
    slopes = _alibi(B_HEADS)
    group = 2
    ngroups = B_HEADS // group
    qs = jnp.concatenate(
        [jnp.concatenate([qlat[:, h * B_DLAT:(h + 1) * B_DLAT], _alibi_coef(tq, slopes[h])], axis=1)
         for h in range(B_HEADS)], axis=0)

    def to_float(u):
        s = jnp.maximum(u ^ INT_MIN, ORDER_OF_NEG_INF)
        return pltpu.bitcast(s ^ ((s >> 31) & 0x7FFFFFFF), F32)

    def variant(n):
        rows = n * kc_rows
        kpos = lax.broadcasted_iota(I32, (rows, tq), 0)
        causal = kpos <= qi * tq + lax.broadcasted_iota(I32, (rows, tq), 1)
        krow = kpos[:SUBLANES]

        rel = _nt_dot(kidx_ref[0, :rows, :], qstack)
        isc = jnp.zeros((rows, tq), F32)
        for h in range(IDX_HEADS):
            isc = isc + wt[h:h + 1, :] * jnp.maximum(rel[:, h * tq:(h + 1) * tq], 0.0)
        key_ref[:rows, :] = jnp.where(causal, isc, NEG_INF)

        def count(pred):
            accs = [jnp.zeros((SUBLANES, tq), I32) for _ in range(4)]
            for r in range(rows // SUBLANES):
                ind = pred(key_ref[r * SUBLANES:(r + 1) * SUBLANES, :], r * SUBLANES + krow)
                accs[r % 4] = accs[r % 4] + ind.astype(I32)
            return jnp.sum((accs[0] + accs[1]) + (accs[2] + accs[3]), axis=0, keepdims=True)

        thr_ref[...] = jnp.full((1, tq), NEG_INF, F32)
        lim_ref[...] = jnp.full((1, tq), 2 ** 30, I32)
        if rows > topk:
            def bit_step(t, cand):
                trial = cand | jnp.left_shift(jnp.int32(1), 31 - t)
                thr = to_float(trial)
                cnt = count(lambda x, p: x >= thr)
                return jnp.where(cnt >= topk, trial, cand)
            thr = to_float(lax.fori_loop(0, 32, bit_step, jnp.zeros((1, tq), I32)))
            thr_ref[...] = thr
            need = topk - count(lambda x, p: x > thr)
            n_eq = count(lambda x, p: x == thr)

            @pl.when(jnp.max((n_eq > need).astype(I32)) > 0)
            def _():
                def pos_step(t, lim):
                    trial = lim | jnp.left_shift(jnp.int32(1), 11 - t)
                    cnt = count(lambda x, p: (x == thr) & (p < trial))
                    return jnp.where(cnt < need, trial, lim)
                lim_ref[...] = lax.fori_loop(0, 12, pos_step, jnp.zeros((1, tq), I32))

        thr = thr_ref[...]
        lim = lim_ref[...]
        x = key_ref[:rows, :]
        sel = ((x > thr) | ((x == thr) & (kpos <= lim))) & causal
        bias = jnp.where(sel, 0.0, NEG_INF)
        bias = jnp.concatenate([bias] * group, axis=1)
        kaug = jnp.concatenate([kv_ref[0, :rows, :], pos_ref[:rows, :]], axis=1)
        kvt = jnp.concatenate([kvt_ref[0, c] for c in range(n)], axis=1)
        s_all = _nt_dot(kaug, qs)
        ps = []
        for g in range(ngroups):
            s = s_all[:, g * group * tq:(g + 1) * group * tq] + bias
            m = jnp.max(s, axis=0, keepdims=True)
            ps.append(jnp.exp2(s - m).astype(BF16))
        acc = _dot(kvt, jnp.concatenate(ps, axis=1))

        o = acc[:B_DLAT] * (1.0 / acc[B_DLAT:B_DLAT + 1])
        o_all = jnp.concatenate([o[:, h * tq:(h + 1) * tq].T for h in range(B_HEADS)], axis=1).astype(BF16)
        o_ref[0] = jnp.dot(o_all, wuv_ref[...], preferred_element_type=F32).astype(o_ref.dtype)

    for n in range(1, key_ref.shape[0] // kc_rows + 1):
        pl.when(nch == n)(functools.partial(variant, n))


def _dsa(cq, w_uq, w_qidx, widxt, ckv, ckvt, kidx, pos, w_uv):
    bsz, seq, _ = cq.shape
    wuq = w_uq.reshape(B_DQLAT, B_HEADS * B_DLAT).astype(BF16)
    wqidx = w_qidx.reshape(B_DQLAT, IDX_HEADS * IDX_DH).astype(BF16)
    tq = min(B_TQ, seq)
    topk = min(TOPK_MAX, seq // 4)
    eye = jnp.eye(B_HEADS, dtype=w_uv.dtype)
    wuv = jnp.einsum('hde,hg->hdge', w_uv, eye).reshape(B_HEADS * B_DLAT, B_HEADS * B_DV).astype(BF16)
    return pl.pallas_call(
        functools.partial(_dsa_kernel, topk=topk),
        grid=(bsz, seq // tq),
        in_specs=[pl.BlockSpec((1, tq, B_DQLAT), lambda b, i: (b, i, 0)),
                  pl.BlockSpec(wuq.shape, lambda b, i: (0, 0)),
                  pl.BlockSpec(wqidx.shape, lambda b, i: (0, 0)),
                  pl.BlockSpec((1, IDX_HEADS, tq), lambda b, i: (b, 0, i)),
                  pl.BlockSpec((1, seq, B_DLAT), lambda b, i: (b, 0, 0)),
                  pl.BlockSpec((1,) + ckvt.shape[1:], lambda b, i: (b, 0, 0, 0)),
                  pl.BlockSpec((1, seq, IDX_HEADS * IDX_DH), lambda b, i: (b, 0, 0)),
                  pl.BlockSpec(pos.shape, lambda b, i: (0, 0)),
                  pl.BlockSpec(wuv.shape, lambda b, i: (0, 0))],
        out_specs=pl.BlockSpec((1, tq, B_HEADS * B_DV), lambda b, i: (b, i, 0)),
        out_shape=jax.ShapeDtypeStruct((bsz, seq, B_HEADS * B_DV), BF16),
        scratch_shapes=[pltpu.VMEM((seq, tq), F32),
                        pltpu.VMEM((1, tq), F32), pltpu.VMEM((1, tq), I32)],
        compiler_params=_params(("arbitrary", "arbitrary")),
        name="dsa",
    )(cq, wuq, wqidx, widxt, ckv, ckvt, kidx, pos, wuv)


def _swa_block(q, kp, kc, vtp, vtc, pos, sink_ref, mask):
    w = q.shape[0]
    rep = C_HEADS // C_KV_HEADS
    half = LANES // 2
    lane = lax.broadcasted_iota(I32, (w, LANES), 1)
    slopes = _alibi(C_HEADS)
    qs = []
    for h in range(C_HEADS):
        g = h // rep
        x = q[:, (h // 2) * LANES:(h // 2 + 1) * LANES].astype(F32)
        if (h % 2) != g:
            x = pltpu.roll(x, half, axis=1)
        x = jnp.where((lane // half) == g, x, 0.0).astype(BF16)
        qs.append(jnp.concatenate([x, _alibi_coef(w, slopes[h])], axis=1))
    kaug = jnp.concatenate([jnp.concatenate([kp, kc], axis=0), pos], axis=1)
    vt = jnp.concatenate([vtp, vtc], axis=1)
    s_all = _nt_dot(kaug, jnp.concatenate(qs, axis=0))
    ps, rs = [], []
    for h in range(C_HEADS):
        s = s_all[:, h * w:(h + 1) * w] + mask
        sink = LOG2E * (sink_ref[h:h + 1, :] + slopes[h] * (w + lane[:1, :]).astype(F32))
        m = jnp.maximum(jnp.max(s, axis=0, keepdims=True), sink)
        p = jnp.exp2(s - m)
        rs.append(1.0 / (jnp.sum(p, axis=0, keepdims=True) + jnp.exp2(sink - m)))
        ps.append(p.astype(BF16))
    o = _dot(vt, jnp.concatenate(ps, axis=1)) * jnp.concatenate(rs, axis=1)
    ot = [o[(h // rep) * C_DH:(h // rep + 1) * C_DH, h * w:(h + 1) * w] for h in range(C_HEADS)]
    return jnp.concatenate(ot, axis=0).T


def _layer1_kernel(oa_ref, ob_ref, sg_ref, x_ref, gmod0_ref, wout0_ref, shift_ref, scale_ref, ng_ref, win_ref,
                   pos_ref, sink_ref, dww_ref, dwb_ref, lng_ref, lnb_ref, gmod1_ref, wout1_ref, fg_ref,
                   o_ref, mask_ref, kprev_ref, vtprev_ref, buf_ref, shift_buf_ref):
    b = pl.program_id(0)
    i = pl.program_id(1)
    t = x_ref.shape[1]
    w = WINDOW
    halo = CONV_HALO

    @pl.when((b == 0) & (i == 0))
    def _():
        k = lax.broadcasted_iota(I32, (2 * w, w), 0)
        q = lax.broadcasted_iota(I32, (2 * w, w), 1)
        dist = w + q - k
        valid = (dist >= 0) & (dist < w)
        mask_ref[1] = jnp.where(valid, 0.0, NEG_INF)
        mask_ref[0] = jnp.where(valid & (k >= w), 0.0, NEG_INF)
        kprev_ref[...] = jnp.zeros(kprev_ref.shape, kprev_ref.dtype)
        vtprev_ref[...] = jnp.zeros(vtprev_ref.shape, vtprev_ref.dtype)
        buf_ref[:halo, :] = jnp.zeros((halo, buf_ref.shape[1]), F32)

    y = jnp.concatenate([oa_ref[0], ob_ref[0]], axis=1).astype(F32) * sg_ref[0].astype(F32)
    x1 = x_ref[0] + gmod0_ref[0] * _dot(y.astype(BF16), wout0_ref[...])
    h = _rms(x1, ng_ref[...]) * (1.0 + scale_ref[0]) + shift_ref[0]
    p = _dot(h.astype(BF16), win_ref[...])
    qw = C_HEADS * C_DH
    kw = C_KV_HEADS * C_DH
    qc = (p[:, :qw] * (C_DH ** -0.5 * LOG2E)).astype(BF16)
    kc = p[:, qw:qw + kw].astype(BF16)
    vct = p[:, qw + kw:qw + 2 * kw].T.astype(BF16)
    off = qw + 2 * kw
    hglu = p[:, off:off + D_CH] * jax.nn.sigmoid(p[:, off + D_CH:off + 2 * D_CH])
    gate = p[:, off + 2 * D_CH:]
    sg1 = gate * jax.nn.sigmoid(gate)

    oc = []
    for j in range(t // w):
        kp = kprev_ref[...] if j == 0 else kc[(j - 1) * w:j * w]
        vtp = vtprev_ref[...] if j == 0 else vct[:, (j - 1) * w:j * w]
        mask = mask_ref[jnp.minimum(i, 1)] if j == 0 else mask_ref[1]
        oc.append(_swa_block(qc[j * w:(j + 1) * w], kp, kc[j * w:(j + 1) * w], vtp, vct[:, j * w:(j + 1) * w],
                             pos_ref[...], sink_ref, mask))
    kprev_ref[...] = kc[t - w:]
    vtprev_ref[...] = vct[:, t - w:]

    buf_ref[:halo, :] = jnp.where(i > 0, buf_ref[:halo, :], 0.0)
    buf_ref[halo:, :] = hglu
    span = shift_buf_ref.shape[1]
    for ph in range(1, SUBLANES):
        shift_buf_ref[ph - 1] = buf_ref[ph:ph + span, :]
    acc = jnp.zeros((t, D_CH), F32) + dwb_ref[...]
    first = halo - (CONV_W - 1)
    for j in range(CONV_W):
        base, ph = (first + j) // SUBLANES * SUBLANES, (first + j) % SUBLANES
        rows = buf_ref[base:base + t, :] if ph == 0 else shift_buf_ref[ph - 1, base:base + t, :]
        acc = acc + dww_ref[j:j + 1, :] * rows
    buf_ref[:halo, :] = hglu[t - halo:]
    mu = jnp.mean(acc, axis=-1, keepdims=True)
    xc = acc - mu
    var = jnp.mean(xc * xc, axis=-1, keepdims=True)
    yn = xc * lax.rsqrt(var + EPS) * lng_ref[...] + lnb_ref[...]
    od = yn * jax.nn.sigmoid(yn)

    y1 = jnp.concatenate([jnp.concatenate(oc, axis=0), od], axis=1) * sg1
    x2 = x1 + gmod1_ref[0] * _dot(y1.astype(BF16), wout1_ref[...])
    o_ref[0] = _rms(x2, fg_ref[...])


def _layer1(oa, ob, sg, x, gmod0, w_out0, shift, scale, norm_g, w_in, pos, sinks, dw_w, dw_b, ln_g, ln_b,
            gmod1, w_out1, final_g):
    bsz, seq, d = x.shape
    t = min(LAYER1_TILE, seq)
    w = WINDOW
    kw = C_KV_HEADS * C_DH
    wout0 = w_out0.astype(BF16)
    win = w_in.astype(BF16)
    wout1 = w_out1.astype(BF16)
    vec = lambda a: a.reshape(1, -1)
    sink_rows = jnp.broadcast_to(sinks.astype(F32)[:, None], (C_HEADS, w))
    row = lambda width: pl.BlockSpec((1, t, width), lambda b, i: (b, i, 0))
    mod = pl.BlockSpec((1, 1, d), lambda b, i: (b, 0, 0))
    full = lambda a: pl.BlockSpec(a.shape, lambda b, i: (0,) * a.ndim)
    consts = [vec(norm_g), win]
    tail = [sink_rows, dw_w, vec(dw_b), vec(ln_g), vec(ln_b)]
    return pl.pallas_call(
        _layer1_kernel,
        grid=(bsz, seq // t),
        in_specs=[row(oa.shape[2]), row(ob.shape[2]), row(d), row(d), mod, full(wout0), mod, mod]
                 + [full(a) for a in consts]
                 + [pl.BlockSpec((2 * w, LANES), lambda b, i: (0, 0))]
                 + [full(a) for a in tail]
                 + [mod, full(wout1), full(vec(final_g))],
        out_specs=row(d),
        out_shape=jax.ShapeDtypeStruct((bsz, seq, d), F32),
        scratch_shapes=[pltpu.VMEM((2, 2 * w, w), F32),
                        pltpu.VMEM((w, kw), BF16), pltpu.VMEM((kw, w), BF16),
                        pltpu.VMEM((CONV_HALO + t, D_CH), F32),
                        pltpu.VMEM((SUBLANES - 1, CONV_HALO - SUBLANES + t, D_CH), F32)],
        compiler_params=_params(("arbitrary", "arbitrary")),
        name="layer1",
    )(oa, ob, sg, x, gmod0, wout0, shift, scale, *consts, pos, *tail, gmod1, wout1, vec(final_g))


def kernel(x, c, ada_w, ada_b, norm_g, ab_w_in, ab_q_norm_g, ab_kv_norm_g, ab_w_uq, ab_w_qidx, ab_w_uv,
           ab_lam_q1, ab_lam_k1, ab_lam_q2, ab_lam_k2, ab_subln_g, ab_w_out,
           cd_w_in, cd_sinks, cd_dw_w, cd_dw_b, cd_ln_g, cd_ln_b, cd_w_out, final_g):
    bsz, seq, d = x.shape
    mods = _ada_mods(c, ada_w, ada_b)
    mod = lambda l, k: mods[l, :, k * d:(k + 1) * d].reshape(bsz, 1, d)
    pos = _key_positions(seq)
    qa, ka, vat, cq, ckv, ckvt, kidx, widxt, sg0 = _proj0(
        x, mod(0, 0), mod(0, 1), norm_g[0], ab_w_in[0], ab_q_norm_g[0], ab_kv_norm_g[0])
    oa = _diff_attn(qa, ka, vat, pos, ab_lam_q1[0], ab_lam_k1[0], ab_lam_q2[0], ab_lam_k2[0], ab_subln_g[0], 0)
    ob = _dsa(cq, ab_w_uq[0], ab_w_qidx[0], widxt, ckv, ckvt, kidx, pos, ab_w_uv[0])
    return _layer1(oa, ob, sg0, x, mod(0, 2), ab_w_out[0], mod(1, 0), mod(1, 1), norm_g[1], cd_w_in[0], pos,
                   cd_sinks[0], cd_dw_w[0], cd_dw_b[0], cd_ln_g[0], cd_ln_b[0], mod(1, 2), cd_w_out[0], final_g)
```

```python
import functools
import math

import numpy as np
import jax
import jax.numpy as jnp
from jax import lax
from jax.experimental import pallas as pl
from jax.experimental.pallas import tpu as pltpu

F32 = jnp.float32
BF16 = jnp.bfloat16
I32 = jnp.int32

EPS = 1e-6
A_HEADS = 4
A_DH = 64
A_DV = 128
B_HEADS = 8
B_DQLAT = 128
B_DLAT = 128
B_DV = 64
IDX_HEADS = 8
IDX_DH = 32
TOPK_MAX = 256
C_HEADS = 8
C_KV_HEADS = 2
C_DH = 64
WINDOW = 128
D_CH = 512
CONV_W = 31

LANES = 128
SUBLANES = 8
INT_MIN = -2 ** 31
ORDER_OF_NEG_INF = INT_MIN + 0x7FFFFF
NEG_INF = float("-inf")
VMEM_LIMIT = 48 * 1024 * 1024

ROW_TILE = 1024
LAYER1_TILE = 512
KEY_CHUNK = 256
B_TQ = 128
POS_SPLIT = 16
ONES_ROWS = 16
LOG2E = math.log2(math.e)
COEF_TERMS = 4
CONV_HALO = 32


def _alibi(n):
    return [float(2.0 ** (-8.0 * i / n)) for i in range(1, n + 1)]


def _params(sem):
    return pltpu.CompilerParams(dimension_semantics=sem, vmem_limit_bytes=VMEM_LIMIT)


def _nt_dot(a, b):
    return lax.dot_general(a, b, (((1,), (1,)), ((), ())), preferred_element_type=F32)


def _dot(a, b):
    return jnp.dot(a, b, preferred_element_type=F32)


def _rms(x, g):
    return x * lax.rsqrt(jnp.mean(x * x, axis=-1, keepdims=True) + EPS) * g


def _bf16_terms(x, n):
    terms = []
    for _ in range(n):
        t = float(np.asarray(x, np.float32).astype(jnp.bfloat16).astype(np.float32))
        terms.append(t)
        x -= t
    return terms


def _alibi_coef(rows, slope):
    lane = lax.broadcasted_iota(I32, (rows, LANES), 1)
    out = jnp.zeros((rows, LANES), F32)
    for i, t in enumerate(_bf16_terms(LOG2E * slope, COEF_TERMS)):
        out = jnp.where(lane == 2 * i, POS_SPLIT * t, jnp.where(lane == 2 * i + 1, t, out))
    return out.astype(BF16)


def _key_positions(seq):
    pos = np.zeros((seq, LANES), np.float32)
    for i in range(COEF_TERMS):
        pos[:, 2 * i] = np.arange(seq) // POS_SPLIT
        pos[:, 2 * i + 1] = np.arange(seq) % POS_SPLIT
    return jnp.asarray(pos, BF16)


def _ada_kernel(c_ref, w_ref, b_ref, o_ref):
    c = c_ref[...]
    sc = c * jax.nn.sigmoid(c)
    o_ref[0] = jnp.dot(sc.astype(BF16), w_ref[0].astype(BF16), preferred_element_type=F32) + b_ref[0]


def _ada_mods(c, ada_w, ada_b):
    depth, d, d3 = ada_w.shape
    bsz = c.shape[0]
    nt = d3 // d
    return pl.pallas_call(
        _ada_kernel,
        grid=(depth, nt),
        in_specs=[pl.BlockSpec((bsz, d), lambda l, j: (0, 0)),
                  pl.BlockSpec((1, d, d), lambda l, j: (l, 0, j)),
                  pl.BlockSpec((1, 1, d), lambda l, j: (l, 0, j))],
        out_specs=pl.BlockSpec((1, bsz, d), lambda l, j: (l, 0, j)),
        out_shape=jax.ShapeDtypeStruct((depth, bsz, d3), F32),
        compiler_params=_params(("arbitrary", "arbitrary")),
        name="ada_mods",
    )(c, ada_w, ada_b.reshape(depth, 1, d3))


def _proj0_kernel(x_ref, shift_ref, scale_ref, ng_ref, wmain_ref, wgate_ref, wkidx_ref, wwidx_ref,
                  qng_ref, kvng_ref,
                  qa_ref, ka_ref, vat_ref, cq_ref, ckv_ref, ckvt_ref, kidx_ref, widxt_ref, sg_ref):
    x = x_ref[0]
    h = _rms(x, ng_ref[...]) * (1.0 + scale_ref[0]) + shift_ref[0]
    hb = h.astype(BF16)
    main = jnp.dot(hb, wmain_ref[...], preferred_element_type=F32)
    qw = A_HEADS * 2 * A_DH
    qa_ref[0] = (main[:, :qw] * (A_DH ** -0.5 * LOG2E)).astype(BF16)
    ka_ref[0] = main[:, qw:2 * qw].astype(BF16)
    va = main[:, 2 * qw:3 * qw]
    cq_ref[0] = _rms(main[:, 3 * qw:3 * qw + B_DQLAT], qng_ref[...]).astype(BF16)
    ckv = _rms(main[:, 3 * qw + B_DQLAT:], kvng_ref[...])
    ckv_ref[0] = ckv.astype(BF16)
    ch = vat_ref.shape[3]
    ones = jnp.ones((ONES_ROWS, ch), F32)
    for c in range(vat_ref.shape[1]):
        vt = va[c * ch:(c + 1) * ch].T
        vat_ref[0, c] = jnp.concatenate(
            [blk for h in range(A_HEADS) for blk in (vt[h * A_DV:(h + 1) * A_DV], ones)], axis=0).astype(BF16)
        ckvt_ref[0, c] = jnp.concatenate([ckv[c * ch:(c + 1) * ch].T, ones], axis=0).astype(BF16)
    kidx_ref[0] = jnp.dot(hb, wkidx_ref[...], preferred_element_type=F32).astype(BF16)
    widxt = _nt_dot(wwidx_ref[...], hb)
    for j in range(widxt_ref.shape[1]):
        widxt_ref[0, j] = widxt[:, j * B_TQ:(j + 1) * B_TQ]
    gate = jnp.dot(hb, wgate_ref[...], preferred_element_type=F32)
    sg_ref[0] = (gate * jax.nn.sigmoid(gate)).astype(BF16)


def _proj0(x, shift, scale, norm_g, w_in, q_norm_g, kv_norm_g):
    bsz, seq, d = x.shape
    t = min(ROW_TILE, seq)
    ch = min(KEY_CHUNK, seq)
    qw = A_HEADS * 2 * A_DH
    o_main = 3 * qw + B_DQLAT + B_DLAT
    wmain = w_in[:, :o_main].astype(BF16)
    wkidx = jnp.tile(w_in[:, o_main:o_main + IDX_DH], (1, IDX_HEADS)).astype(BF16)
    wwidx = w_in[:, o_main + IDX_DH:o_main + IDX_DH + IDX_HEADS].T.astype(BF16)
    wgate = w_in[:, o_main + IDX_DH + IDX_HEADS:].astype(BF16)
    row = lambda w: pl.BlockSpec((1, t, w), lambda b, i: (b, i, 0))
    rows = lambda w: (jax.ShapeDtypeStruct((bsz, seq, w), BF16), row(w))
    chunked = lambda w: (jax.ShapeDtypeStruct((bsz, seq // ch, w, ch), BF16),
                         pl.BlockSpec((1, t // ch, w, ch), lambda b, i: (b, i, 0, 0)))
    mod = pl.BlockSpec((1, 1, d), lambda b, i: (b, 0, 0))
    full = lambda a: pl.BlockSpec(a.shape, lambda b, i: (0,) * a.ndim)
    ng = norm_g.reshape(1, d)
    qng = q_norm_g.reshape(1, B_DQLAT)
    kvng = kv_norm_g.reshape(1, B_DLAT)
    iw = IDX_HEADS * IDX_DH
    outs = [rows(qw), rows(qw), chunked(A_HEADS * (A_DV + ONES_ROWS)), rows(B_DQLAT), rows(B_DLAT),
            chunked(B_DLAT + ONES_ROWS), rows(iw),
            (jax.ShapeDtypeStruct((bsz, seq // B_TQ, IDX_HEADS, B_TQ), F32),
             pl.BlockSpec((1, t // B_TQ, IDX_HEADS, B_TQ), lambda b, i: (b, i, 0, 0))),
            rows(d)]
    return pl.pallas_call(
        _proj0_kernel,
        grid=(bsz, seq // t),
        in_specs=[row(d), mod, mod, full(ng), full(wmain), full(wgate), full(wkidx), full(wwidx),
                  full(qng), full(kvng)],
        out_specs=[o[1] for o in outs],
        out_shape=[o[0] for o in outs],
        compiler_params=_params(("arbitrary", "arbitrary")),
        name="proj0",
    )(x, shift, scale, ng, wmain, wgate, wkidx, wwidx, qng, kvng)


def _diff_attn_kernel(q_ref, k_ref, vt_ref, pos_ref, lq1_ref, lk1_ref, lq2_ref, lk2_ref, g_ref, o_ref,
                      *, lam_init):
    tk = vt_ref.shape[3]
    tq = tk
    lam = (jnp.exp(jnp.sum(lq1_ref[...] * lk1_ref[...], keepdims=True))
           - jnp.exp(jnp.sum(lq2_ref[...] * lk2_ref[...], keepdims=True)) + lam_init)
    krow = lax.broadcasted_iota(I32, (tk, 2 * tq), 0)
    qcol = lax.broadcasted_iota(I32, (tk, 2 * tq), 1) % tq
    causal = krow <= qcol
    slopes = _alibi(A_HEADS)
    heads = [slice(h * A_DV, (h + 1) * A_DV) for h in range(A_HEADS)]
    first = lax.broadcasted_iota(I32, (tq, A_DV), 1) < A_DH
    zero = jnp.zeros((tq, A_DV), BF16)
    rows_v = vt_ref.shape[2] // A_HEADS
    vrows = [slice(h * rows_v, (h + 1) * rows_v) for h in range(A_HEADS)]

    def variant(n, i):
        rows = n * tk
        pos = pos_ref[:rows, :]
        qrows = pl.ds(pl.multiple_of(i * tq, tq), tq)
        qs = [jnp.concatenate(
            [jnp.concatenate([jnp.where(first, q_ref[0, qrows, hs], zero), _alibi_coef(tq, slopes[h])], axis=1),
             jnp.concatenate([jnp.where(first, zero, q_ref[0, qrows, hs]), _alibi_coef(tq, slopes[h])], axis=1)],
            axis=0) for h, hs in enumerate(heads)]
        for h, hs in enumerate(heads):
            s = _nt_dot(jnp.concatenate([k_ref[0, :rows, hs], pos], axis=1), qs[h])
            last = jnp.where(causal, s[rows - tk:], NEG_INF)
            s = last if n == 1 else jnp.concatenate([s[:rows - tk], last], axis=0)
            m = jnp.max(s, axis=0, keepdims=True)
            vt = jnp.concatenate([vt_ref[0, c, vrows[h], :] for c in range(n)], axis=1)
            acc = _dot(vt, jnp.exp2(s - m).astype(BF16))
            o = acc[:A_DV] * (1.0 / acc[A_DV:A_DV + 1])
            od = (o[:, :tq] - lam * o[:, tq:]).T
            o_ref[0, qrows, hs] = (_rms(od, g_ref[...]) * (1.0 - lam_init)).astype(o_ref.dtype)

    def tile(i, carry):
        for n in range(1, vt_ref.shape[1] + 1):
            pl.when(i + 1 == n)(functools.partial(variant, n, i))
        return carry

    lax.fori_loop(0, vt_ref.shape[1], tile, 0)


def _diff_attn(qa, ka, vat, pos, lam_q1, lam_k1, lam_q2, lam_k2, subln_g, layer_idx):
    bsz, seq, w = qa.shape
    tq = vat.shape[3]
    lam_init = 0.8 - 0.6 * math.exp(-0.3 * layer_idx)
    vec = lambda a: a.reshape(1, -1)
    full = lambda a: pl.BlockSpec(a.shape, lambda b: (0,) * a.ndim)
    small = [vec(lam_q1), vec(lam_k1), vec(lam_q2), vec(lam_k2), vec(subln_g)]
    return pl.pallas_call(
        functools.partial(_diff_attn_kernel, lam_init=lam_init),
        grid=(bsz,),
        in_specs=[pl.BlockSpec((1, seq, w), lambda b: (b, 0, 0)),
                  pl.BlockSpec((1, seq, w), lambda b: (b, 0, 0)),
                  pl.BlockSpec((1,) + vat.shape[1:], lambda b: (b, 0, 0, 0)),
                  full(pos)] + [full(a) for a in small],
        out_specs=pl.BlockSpec((1, seq, w), lambda b: (b, 0, 0)),
        out_shape=jax.ShapeDtypeStruct((bsz, seq, w), BF16),
        compiler_params=_params(("arbitrary",)),
        name="diff_attn",
    )(qa, ka, vat, pos, *small)


def _dsa_kernel(cq_ref, wuq_ref, wqidx_ref, wt_ref, kv_ref, kvt_ref, kidx_ref, pos_ref, wuv_ref, o_ref,
                key_ref, thr_ref, lim_ref, *, topk):
    tiles, tq = wt_ref.shape[1], wt_ref.shape[3]
    kc_rows = kvt_ref.shape[3]
    slopes = _alibi(B_HEADS)
    group = 2
    ngroups = B_HEADS // group

    def query_operands(j):
        cq = cq_ref[0, pl.ds(pl.multiple_of(j * tq, tq), tq), :]
        qidx = _dot(cq, wqidx_ref[...])
        qlat = (_dot(cq, wuq_ref[...]) * (B_DLAT ** -0.5 * LOG2E)).astype(BF16)
        head_of_lane = lax.broadcasted_iota(I32, qidx.shape, 1) // IDX_DH
        qstack = jnp.concatenate([jnp.where(head_of_lane == h, qidx, 0.0) for h in range(IDX_HEADS)],
                                 axis=0).astype(BF16)
        qs = jnp.concatenate(
            [jnp.concatenate([qlat[:, h * B_DLAT:(h + 1) * B_DLAT], _alibi_coef(tq, slopes[h])], axis=1)
             for h in range(B_HEADS)], axis=0)
        return qstack, qs

    def to_float(u):
        s = jnp.maximum(u ^ INT_MIN, ORDER_OF_NEG_INF)
        return pltpu.bitcast(s ^ ((s >> 31) & 0x7FFFFFFF), F32)

    def variant(n, qi):
        rows = n * kc_rows
        qstack, qs = query_operands(qi)
        wt = wt_ref[0, qi]
        kpos = lax.broadcasted_iota(I32, (rows, tq), 0)
        causal = kpos <= qi * tq + lax.broadcasted_iota(I32, (rows, tq), 1)
        krow = kpos[:SUBLANES]

        rel = _nt_dot(kidx_ref[0, :rows, :], qstack)
        isc = jnp.zeros((rows, tq), F32)
        for h in range(IDX_HEADS):
            isc = isc + wt[h:h + 1, :] * jnp.maximum(rel[:, h * tq:(h + 1) * tq], 0.0)
        key_ref[:rows, :] = jnp.where(causal, isc, NEG_INF)

        def count(pred):
            accs = [jnp.zeros((SUBLANES, tq), I32) for _ in range(4)]
            for r in range(rows // SUBLANES):
                ind = pred(key_ref[r * SUBLANES:(r + 1) * SUBLANES, :], r * SUBLANES + krow)
                accs[r % 4] = accs[r % 4] + ind.astype(I32)
            return jnp.sum((accs[0] + accs[1]) + (accs[2] + accs[3]), axis=0, keepdims=True)

        thr_ref[...] = jnp.full((1, tq), NEG_INF, F32)
        lim_ref[...] = jnp.full((1, tq), 2 ** 30, I32)
        if rows > topk:
            def bit_step(t, cand):
                trial = cand | jnp.left_shift(jnp.int32(1), 31 - t)
                thr = to_float(trial)
                cnt = count(lambda x, p: x >= thr)
                return jnp.where(cnt >= topk, trial, cand)
            thr = to_float(lax.fori_loop(0, 32, bit_step, jnp.zeros((1, tq), I32)))
            thr_ref[...] = thr
            need = topk - count(lambda x, p: x > thr)
            n_eq = count(lambda x, p: x == thr)

            @pl.when(jnp.max((n_eq > need).astype(I32)) > 0)
            def _():
                def pos_step(t, lim):
                    trial = lim | jnp.left_shift(jnp.int32(1), 11 - t)
                    cnt = count(lambda x, p: (x == thr) & (p < trial))
                    return jnp.where(cnt < need, trial, lim)
                lim_ref[...] = lax.fori_loop(0, 12, pos_step, jnp.zeros((1, tq), I32))

        thr = thr_ref[...]
        lim = lim_ref[...]
        x = key_ref[:rows, :]
        sel = ((x > thr) | ((x == thr) & (kpos <= lim))) & causal
        bias = jnp.where(sel, 0.0, NEG_INF)
        bias = jnp.concatenate([bias] * group, axis=1)
        kaug = jnp.concatenate([kv_ref[0, :rows, :], pos_ref[:rows, :]], axis=1)
        kvt = jnp.concatenate([kvt_ref[0, c] for c in range(n)], axis=1)
        s_all = _nt_dot(kaug, qs)
        ps = []
        for g in range(ngroups):
            s = s_all[:, g * group * tq:(g + 1) * group * tq] + bias
            m = jnp.max(s, axis=0, keepdims=True)
            ps.append(jnp.exp2(s - m).astype(BF16))
        acc = _dot(kvt, jnp.concatenate(ps, axis=1))

        o = acc[:B_DLAT] * (1.0 / acc[B_DLAT:B_DLAT + 1])
        o_all = jnp.concatenate([o[:, h * tq:(h + 1) * tq].T for h in range(B_HEADS)], axis=1).astype(BF16)
        o_ref[0, pl.ds(pl.multiple_of(qi * tq, tq), tq), :] = _dot(o_all, wuv_ref[...]).astype(o_ref.dtype)

    def tile(qi, carry):
        nch = (qi * tq) // kc_rows + 1
        for n in range(1, key_ref.shape[0] // kc_rows + 1):
            pl.when(nch == n)(functools.partial(variant, n, qi))
        return carry

    lax.fori_loop(0, tiles, tile, 0)


def _dsa(cq, w_uq, w_qidx, widxt, ckv, ckvt, kidx, pos, w_uv):
    bsz, seq, _ = cq.shape
    wuq = w_uq.reshape(B_DQLAT, B_HEADS * B_DLAT).astype(BF16)
    wqidx = w_qidx.reshape(B_DQLAT, IDX_HEADS * IDX_DH).astype(BF16)
    tq = min(B_TQ, seq)
    topk = min(TOPK_MAX, seq // 4)
    eye = jnp.eye(B_HEADS, dtype=w_uv.dtype)
    wuv = jnp.einsum('hde,hg->hdge', w_uv, eye).reshape(B_HEADS * B_DLAT, B_HEADS * B_DV).astype(BF16)
    return pl.pallas_call(
        functools.partial(_dsa_kernel, topk=topk),
        grid=(bsz,),
        in_specs=[pl.BlockSpec((1, seq, B_DQLAT), lambda b: (b, 0, 0)),
                  pl.BlockSpec(wuq.shape, lambda b: (0, 0)),
                  pl.BlockSpec(wqidx.shape, lambda b: (0, 0)),
                  pl.BlockSpec((1,) + widxt.shape[1:], lambda b: (b, 0, 0, 0)),
                  pl.BlockSpec((1, seq, B_DLAT), lambda b: (b, 0, 0)),
                  pl.BlockSpec((1,) + ckvt.shape[1:], lambda b: (b, 0, 0, 0)),
                  pl.BlockSpec((1, seq, IDX_HEADS * IDX_DH), lambda b: (b, 0, 0)),
                  pl.BlockSpec(pos.shape, lambda b: (0, 0)),
                  pl.BlockSpec(wuv.shape, lambda b: (0, 0))],
        out_specs=pl.BlockSpec((1, seq, B_HEADS * B_DV), lambda b: (b, 0, 0)),
        out_shape=jax.ShapeDtypeStruct((bsz, seq, B_HEADS * B_DV), BF16),
        scratch_shapes=[pltpu.VMEM((seq, tq), F32),
                        pltpu.VMEM((1, tq), F32), pltpu.VMEM((1, tq), I32)],
        compiler_params=_params(("arbitrary",)),
        name="dsa",
    )(cq, wuq, wqidx, widxt, ckv, ckvt, kidx, pos, wuv)


def _swa_block(q, kp, kc, vtp, vtc, pos, sink_ref, mask):
    w = q.shape[0]
    rep = C_HEADS // C_KV_HEADS
    half = LANES // 2
    lane = lax.broadcasted_iota(I32, (w, LANES), 1)
    slopes = _alibi(C_HEADS)
    qs = []
    for h in range(C_HEADS):
        g = h // rep
        x = q[:, (h // 2) * LANES:(h // 2 + 1) * LANES].astype(F32)
        if (h % 2) != g:
            x = pltpu.roll(x, half, axis=1)
        x = jnp.where((lane // half) == g, x, 0.0).astype(BF16)
        qs.append(jnp.concatenate([x, _alibi_coef(w, slopes[h])], axis=1))
    kaug = jnp.concatenate([jnp.concatenate([kp, kc], axis=0), pos], axis=1)
    vt = jnp.concatenate([vtp, vtc], axis=1)
    s_all = _nt_dot(kaug, jnp.concatenate(qs, axis=0))
    ps, rs = [], []
    for h in range(C_HEADS):
        s = s_all[:, h * w:(h + 1) * w] + mask
        sink = LOG2E * (sink_ref[h:h + 1, :] + slopes[h] * (w + lane[:1, :]).astype(F32))
        m = jnp.maximum(jnp.max(s, axis=0, keepdims=True), sink)
        p = jnp.exp2(s - m)
        rs.append(1.0 / (jnp.sum(p, axis=0, keepdims=True) + jnp.exp2(sink - m)))
        ps.append(p.astype(BF16))
    o = _dot(vt, jnp.concatenate(ps, axis=1)) * jnp.concatenate(rs, axis=1)
    ot = [o[(h // rep) * C_DH:(h // rep + 1) * C_DH, h * w:(h + 1) * w] for h in range(C_HEADS)]
    return jnp.concatenate(ot, axis=0).T


def _layer1_kernel(oa_ref, ob_ref, sg_ref, x_ref, gmod0_ref, wout0_ref, shift_ref, scale_ref, ng_ref, win_ref,
                   pos_ref, sink_ref, dww_ref, dwb_ref, lng_ref, lnb_ref, gmod1_ref, wout1_ref, fg_ref,
                   o_ref, mask_ref, kprev_ref, vtprev_ref, buf_ref, shift_buf_ref):
    b = pl.program_id(0)
    i = pl.program_id(1)
    t = x_ref.shape[1]
    w = WINDOW
    halo = CONV_HALO

    @pl.when((b == 0) & (i == 0))
    def _():
        k = lax.broadcasted_iota(I32, (2 * w, w), 0)
        q = lax.broadcasted_iota(I32, (2 * w, w), 1)
        dist = w + q - k
        valid = (dist >= 0) & (dist < w)
        mask_ref[1] = jnp.where(valid, 0.0, NEG_INF)
        mask_ref[0] = jnp.where(valid & (k >= w), 0.0, NEG_INF)
        kprev_ref[...] = jnp.zeros(kprev_ref.shape, kprev_ref.dtype)
        vtprev_ref[...] = jnp.zeros(vtprev_ref.shape, vtprev_ref.dtype)
        buf_ref[:halo, :] = jnp.zeros((halo, buf_ref.shape[1]), F32)

    y = jnp.concatenate([oa_ref[0], ob_ref[0]], axis=1).astype(F32) * sg_ref[0].astype(F32)
    x1 = x_ref[0] + gmod0_ref[0] * _dot(y.astype(BF16), wout0_ref[...])
    h = _rms(x1, ng_ref[...]) * (1.0 + scale_ref[0]) + shift_ref[0]
    p = _dot(h.astype(BF16), win_ref[...])
    qw = C_HEADS * C_DH
    kw = C_KV_HEADS * C_DH
    qc = (p[:, :qw] * (C_DH ** -0.5 * LOG2E)).astype(BF16)
    kc = p[:, qw:qw + kw].astype(BF16)
    vct = p[:, qw + kw:qw + 2 * kw].T.astype(BF16)
    off = qw + 2 * kw
    hglu = p[:, off:off + D_CH] * jax.nn.sigmoid(p[:, off + D_CH:off + 2 * D_CH])
    gate = p[:, off + 2 * D_CH:]
    sg1 = gate * jax.nn.sigmoid(gate)

    oc = []
    for j in range(t // w):
        kp = kprev_ref[...] if j == 0 else kc[(j - 1) * w:j * w]
        vtp = vtprev_ref[...] if j == 0 else vct[:, (j - 1) * w:j * w]
        mask = mask_ref[jnp.minimum(i, 1)] if j == 0 else mask_ref[1]
        oc.append(_swa_block(qc[j * w:(j + 1) * w], kp, kc[j * w:(j + 1) * w], vtp, vct[:, j * w:(j + 1) * w],
                             pos_ref[...], sink_ref, mask))
    kprev_ref[...] = kc[t - w:]
    vtprev_ref[...] = vct[:, t - w:]

    buf_ref[:halo, :] = jnp.where(i > 0, buf_ref[:halo, :], 0.0)
    buf_ref[halo:, :] = hglu
    span = shift_buf_ref.shape[1]
    for ph in range(1, SUBLANES):
        shift_buf_ref[ph - 1] = buf_ref[ph:ph + span, :]
    acc = jnp.zeros((t, D_CH), F32) + dwb_ref[...]
    first = halo - (CONV_W - 1)
    for j in range(CONV_W):
        base, ph = (first + j) // SUBLANES * SUBLANES, (first + j) % SUBLANES
        rows = buf_ref[base:base + t, :] if ph == 0 else shift_buf_ref[ph - 1, base:base + t, :]
        acc = acc + dww_ref[j:j + 1, :] * rows
    buf_ref[:halo, :] = hglu[t - halo:]
    mu = jnp.mean(acc, axis=-1, keepdims=True)
    xc = acc - mu
    var = jnp.mean(xc * xc, axis=-1, keepdims=True)
    yn = xc * lax.rsqrt(var + EPS) * lng_ref[...] + lnb_ref[...]
    od = yn * jax.nn.sigmoid(yn)

    y1 = jnp.concatenate([jnp.concatenate(oc, axis=0), od], axis=1) * sg1
    x2 = x1 + gmod1_ref[0] * _dot(y1.astype(BF16), wout1_ref[...])
    o_ref[0] = _rms(x2, fg_ref[...])


def _layer1(oa, ob, sg, x, gmod0, w_out0, shift, scale, norm_g, w_in, pos, sinks, dw_w, dw_b, ln_g, ln_b,
            gmod1, w_out1, final_g):
    bsz, seq, d = x.shape
    t = min(LAYER1_TILE, seq)
    w = WINDOW
    kw = C_KV_HEADS * C_DH
    wout0 = w_out0.astype(BF16)
    win = w_in.astype(BF16)
    wout1 = w_out1.astype(BF16)
    vec = lambda a: a.reshape(1, -1)
    sink_rows = jnp.broadcast_to(sinks.astype(F32)[:, None], (C_HEADS, w))
    row = lambda width: pl.BlockSpec((1, t, width), lambda b, i: (b, i, 0))
    mod = pl.BlockSpec((1, 1, d), lambda b, i: (b, 0, 0))
    full = lambda a: pl.BlockSpec(a.shape, lambda b, i: (0,) * a.ndim)
    consts = [vec(norm_g), win]
    tail = [sink_rows, dw_w, vec(dw_b), vec(ln_g), vec(ln_b)]
    return pl.pallas_call(
        _layer1_kernel,
        grid=(bsz, seq // t),
        in_specs=[row(oa.shape[2]), row(ob.shape[2]), row(d), row(d), mod, full(wout0), mod, mod]
                 + [full(a) for a in consts]
                 + [pl.BlockSpec((2 * w, LANES), lambda b, i: (0, 0))]
                 + [full(a) for a in tail]
                 + [mod, full(wout1), full(vec(final_g))],
        out_specs=row(d),
        out_shape=jax.ShapeDtypeStruct((bsz, seq, d), F32),
        scratch_shapes=[pltpu.VMEM((2, 2 * w, w), F32),
                        pltpu.VMEM((w, kw), BF16), pltpu.VMEM((kw, w), BF16),
                        pltpu.VMEM((CONV_HALO + t, D_CH), F32),
                        pltpu.VMEM((SUBLANES - 1, CONV_HALO - SUBLANES + t, D_CH), F32)],
        compiler_params=_params(("arbitrary", "arbitrary")),
        name="layer1",
    )(oa, ob, sg, x, gmod0, wout0, shift, scale, *consts, pos, *tail, gmod1, wout1, vec(final_g))


def kernel(x, c, ada_w, ada_b, norm_g, ab_w_in, ab_q_norm_g, ab_kv_norm_g, ab_w_uq, ab_w_qidx, ab_w_uv,
           ab_lam_q1, ab_lam_k1, ab_lam_q2, ab_lam_k2, ab_subln_g, ab_w_out,
           cd_w_in, cd_sinks, cd_dw_w, cd_dw_b, cd_ln_g, cd_ln_b, cd_w_out, final_g):
    bsz, seq, d = x.shape
    mods = _ada_mods(c, ada_w, ada_b)
    mod = lambda l, k: mods[l, :, k * d:(k + 1) * d].reshape(bsz, 1, d)
    pos = _key_positions(seq)
    qa, ka, vat, cq, ckv, ckvt, kidx, widxt, sg0 = _proj0(
        x, mod(0, 0), mod(0, 1), norm_g[0], ab_w_in[0], ab_q_norm_g[0], ab_kv_norm_g[0])
    oa = _diff_attn(qa, ka, vat, pos, ab_lam_q1[0], ab_lam_k1[0], ab_lam_q2[0], ab_lam_k2[0], ab_subln_g[0], 0)
    ob = _dsa(cq, ab_w_uq[0], ab_w_qidx[0], widxt, ckv, ckvt, kidx, pos, ab_w_uv[0])
    return _layer1(oa, ob, sg0, x, mod(0, 2), ab_w_out[0], mod(1, 0), mod(1, 1), norm_g[1], cd_w_in[0], pos,
                   cd_sinks[0], cd_dw_w[0], cd_dw_b[0], cd_ln_g[0], cd_ln_b[0], mod(1, 2), cd_w_out[0], final_g)
```

```python
import functools
import math

import numpy as np
import jax
import jax.numpy as jnp
from jax import lax
from jax.experimental import pallas as pl
from jax.experimental.pallas import tpu as pltpu

F32 = jnp.float32
BF16 = jnp.bfloat16
I32 = jnp.int32

EPS = 1e-6
A_HEADS = 4
A_DH = 64
A_DV = 128
B_HEADS = 8
B_DQLAT = 128
B_DLAT = 128
B_DV = 64
IDX_HEADS = 8
IDX_DH = 32
TOPK_MAX = 256
C_HEADS = 8
C_KV_HEADS = 2
C_DH = 64
WINDOW = 128
D_CH = 512
CONV_W = 31

LANES = 128
SUBLANES = 8
INT_MIN = -2 ** 31
ORDER_OF_NEG_INF = INT_MIN + 0x7FFFFF
NEG_INF = float("-inf")
VMEM_LIMIT = 48 * 1024 * 1024

ROW_TILE = 1024
LAYER1_TILE = 512
KEY_CHUNK = 256
B_TQ = 128
POS_SPLIT = 16
ONES_ROWS = 16
LOG2E = math.log2(math.e)
COEF_TERMS = 4
CONV_HALO = 32


def _alibi(n):
    return [float(2.0 ** (-8.0 * i / n)) for i in range(1, n + 1)]


def _params(sem):
    return pltpu.CompilerParams(dimension_semantics=sem, vmem_limit_bytes=VMEM_LIMIT)


def _nt_dot(a, b):
    return lax.dot_general(a, b, (((1,), (1,)), ((), ())), preferred_element_type=F32)


def _dot(a, b):
    return jnp.dot(a, b, preferred_element_type=F32)


def _rms(x, g):
    return x * lax.rsqrt(jnp.mean(x * x, axis=-1, keepdims=True) + EPS) * g


def _bf16_terms(x, n):
    terms = []
    for _ in range(n):
        t = float(np.asarray(x, np.float32).astype(jnp.bfloat16).astype(np.float32))
        terms.append(t)
        x -= t
    return terms


def _alibi_coef(rows, slope):
    lane = lax.broadcasted_iota(I32, (rows, LANES), 1)
    out = jnp.zeros((rows, LANES), F32)
    for i, t in enumerate(_bf16_terms(LOG2E * slope, COEF_TERMS)):
        out = jnp.where(lane == 2 * i, POS_SPLIT * t, jnp.where(lane == 2 * i + 1, t, out))
    return out.astype(BF16)


def _key_positions(seq):
    pos = np.zeros((seq, LANES), np.float32)
    for i in range(COEF_TERMS):
        pos[:, 2 * i] = np.arange(seq) // POS_SPLIT
        pos[:, 2 * i + 1] = np.arange(seq) % POS_SPLIT
    return jnp.asarray(pos, BF16)


def _ada_kernel(c_ref, w_ref, b_ref, o_ref):
    c = c_ref[...]
    sc = c * jax.nn.sigmoid(c)
    o_ref[0] = jnp.dot(sc.astype(BF16), w_ref[0].astype(BF16), preferred_element_type=F32) + b_ref[0]


def _ada_mods(c, ada_w, ada_b):
    depth, d, d3 = ada_w.shape
    bsz = c.shape[0]
    nt = d3 // d
    return pl.pallas_call(
        _ada_kernel,
        grid=(depth, nt),
        in_specs=[pl.BlockSpec((bsz, d), lambda l, j: (0, 0)),
                  pl.BlockSpec((1, d, d), lambda l, j: (l, 0, j)),
                  pl.BlockSpec((1, 1, d), lambda l, j: (l, 0, j))],
        out_specs=pl.BlockSpec((1, bsz, d), lambda l, j: (l, 0, j)),
        out_shape=jax.ShapeDtypeStruct((depth, bsz, d3), F32),
        compiler_params=_params(("arbitrary", "arbitrary")),
        name="ada_mods",
    )(c, ada_w, ada_b.reshape(depth, 1, d3))


def _proj0_kernel(x_ref, shift_ref, scale_ref, ng_ref, wmain_ref, wgate_ref, wkidx_ref, wwidx_ref,
                  qng_ref, kvng_ref,
                  qa_ref, ka_ref, vat_ref, cq_ref, ckv_ref, ckvt_ref, kidx_ref, widxt_ref, sg_ref):
    x = x_ref[0]
    h = _rms(x, ng_ref[...]) * (1.0 + scale_ref[0]) + shift_ref[0]
    hb = h.astype(BF16)
    main = jnp.dot(hb, wmain_ref[...], preferred_element_type=F32)
    qw = A_HEADS * 2 * A_DH
    qa_ref[0] = (main[:, :qw] * (A_DH ** -0.5 * LOG2E)).astype(BF16)
    ka_ref[0] = main[:, qw:2 * qw].astype(BF16)
    va = main[:, 2 * qw:3 * qw]
    cq_ref[0] = _rms(main[:, 3 * qw:3 * qw + B_DQLAT], qng_ref[...]).astype(BF16)
    ckv = _rms(main[:, 3 * qw + B_DQLAT:], kvng_ref[...])
    ckv_ref[0] = ckv.astype(BF16)
    ch = vat_ref.shape[3]
    ones = jnp.ones((ONES_ROWS, ch), F32)
    for c in range(vat_ref.shape[1]):
        vt = va[c * ch:(c + 1) * ch].T
        vat_ref[0, c] = jnp.concatenate(
            [blk for h in range(A_HEADS) for blk in (vt[h * A_DV:(h + 1) * A_DV], ones)], axis=0).astype(BF16)
        ckvt_ref[0, c] = jnp.concatenate([ckv[c * ch:(c + 1) * ch].T, ones], axis=0).astype(BF16)
    kidx_ref[0] = jnp.dot(hb, wkidx_ref[...], preferred_element_type=F32).astype(BF16)
    widxt = _nt_dot(wwidx_ref[...], hb)
    for j in range(widxt_ref.shape[1]):
        widxt_ref[0, j] = widxt[:, j * B_TQ:(j + 1) * B_TQ]
    gate = jnp.dot(hb, wgate_ref[...], preferred_element_type=F32)
    sg_ref[0] = (gate * jax.nn.sigmoid(gate)).astype(BF16)


def _proj0(x, shift, scale, norm_g, w_in, q_norm_g, kv_norm_g):
    bsz, seq, d = x.shape
    t = min(ROW_TILE, seq)
    ch = min(KEY_CHUNK, seq)
    qw = A_HEADS * 2 * A_DH
    o_main = 3 * qw + B_DQLAT + B_DLAT
    wmain = w_in[:, :o_main].astype(BF16)
    wkidx = jnp.tile(w_in[:, o_main:o_main + IDX_DH], (1, IDX_HEADS)).astype(BF16)
    wwidx = w_in[:, o_main + IDX_DH:o_main + IDX_DH + IDX_HEADS].T.astype(BF16)
    wgate = w_in[:, o_main + IDX_DH + IDX_HEADS:].astype(BF16)
    row = lambda w: pl.BlockSpec((1, t, w), lambda b, i: (b, i, 0))
    rows = lambda w: (jax.ShapeDtypeStruct((bsz, seq, w), BF16), row(w))
    chunked = lambda w: (jax.ShapeDtypeStruct((bsz, seq // ch, w, ch), BF16),
                         pl.BlockSpec((1, t // ch, w, ch), lambda b, i: (b, i, 0, 0)))
    mod = pl.BlockSpec((1, 1, d), lambda b, i: (b, 0, 0))
    full = lambda a: pl.BlockSpec(a.shape, lambda b, i: (0,) * a.ndim)
    ng = norm_g.reshape(1, d)
    qng = q_norm_g.reshape(1, B_DQLAT)
    kvng = kv_norm_g.reshape(1, B_DLAT)
    iw = IDX_HEADS * IDX_DH
    outs = [rows(qw), rows(qw), chunked(A_HEADS * (A_DV + ONES_ROWS)), rows(B_DQLAT), rows(B_DLAT),
            chunked(B_DLAT + ONES_ROWS), rows(iw),
            (jax.ShapeDtypeStruct((bsz, seq // B_TQ, IDX_HEADS, B_TQ), F32),
             pl.BlockSpec((1, t // B_TQ, IDX_HEADS, B_TQ), lambda b, i: (b, i, 0, 0))),
            rows(d)]
    return pl.pallas_call(
        _proj0_kernel,
        grid=(bsz, seq // t),
        in_specs=[row(d), mod, mod, full(ng), full(wmain), full(wgate), full(wkidx), full(wwidx),
                  full(qng), full(kvng)],
        out_specs=[o[1] for o in outs],
        out_shape=[o[0] for o in outs],
        compiler_params=_params(("arbitrary", "arbitrary")),
        name="proj0",
    )(x, shift, scale, ng, wmain, wgate, wkidx, wwidx, qng, kvng)


def _diff_attn_kernel(q_ref, k_ref, vt_ref, pos_ref, lq1_ref, lk1_ref, lq2_ref, lk2_ref, g_ref, o_ref,
                      *, lam_init):
    tk = vt_ref.shape[3]
    tq = tk
    lam = (jnp.exp(jnp.sum(lq1_ref[...] * lk1_ref[...], keepdims=True))
           - jnp.exp(jnp.sum(lq2_ref[...] * lk2_ref[...], keepdims=True)) + lam_init)
    krow = lax.broadcasted_iota(I32, (tk, 2 * tq), 0)
    qcol = lax.broadcasted_iota(I32, (tk, 2 * tq), 1) % tq
    causal = krow <= qcol
    slopes = _alibi(A_HEADS)
    heads = [slice(h * A_DV, (h + 1) * A_DV) for h in range(A_HEADS)]
    first = lax.broadcasted_iota(I32, (tq, A_DV), 1) < A_DH
    zero = jnp.zeros((tq, A_DV), BF16)
    rows_v = vt_ref.shape[2] // A_HEADS
    vrows = [slice(h * rows_v, (h + 1) * rows_v) for h in range(A_HEADS)]

    def variant(n, i):
        rows = n * tk
        pos = pos_ref[:rows, :]
        qrows = pl.ds(pl.multiple_of(i * tq, tq), tq)
        qs = [jnp.concatenate(
            [jnp.concatenate([jnp.where(first, q_ref[0, qrows, hs], zero), _alibi_coef(tq, slopes[h])], axis=1),
             jnp.concatenate([jnp.where(first, zero, q_ref[0, qrows, hs]), _alibi_coef(tq, slopes[h])], axis=1)],
            axis=0) for h, hs in enumerate(heads)]
        for h, hs in enumerate(heads):
            s = _nt_dot(jnp.concatenate([k_ref[0, :rows, hs], pos], axis=1), qs[h])
            last = jnp.where(causal, s[rows - tk:], NEG_INF)
            s = last if n == 1 else jnp.concatenate([s[:rows - tk], last], axis=0)
            m = jnp.max(s, axis=0, keepdims=True)
            vt = jnp.concatenate([vt_ref[0, c, vrows[h], :] for c in range(n)], axis=1)
            acc = _dot(vt, jnp.exp2(s - m).astype(BF16))
            o = acc[:A_DV] * (1.0 / acc[A_DV:A_DV + 1])
            od = (o[:, :tq] - lam * o[:, tq:]).T
            o_ref[0, qrows, hs] = (_rms(od, g_ref[...]) * (1.0 - lam_init)).astype(o_ref.dtype)

    def tile(i, carry):
        for n in range(1, vt_ref.shape[1] + 1):
            pl.when(i + 1 == n)(functools.partial(variant, n, i))
        return carry

    lax.fori_loop(0, vt_ref.shape[1], tile, 0)


def _diff_attn(qa, ka, vat, pos, lam_q1, lam_k1, lam_q2, lam_k2, subln_g, layer_idx):
    bsz, seq, w = qa.shape
    tq = vat.shape[3]
    lam_init = 0.8 - 0.6 * math.exp(-0.3 * layer_idx)
    vec = lambda a: a.reshape(1, -1)
    full = lambda a: pl.BlockSpec(a.shape, lambda b: (0,) * a.ndim)
    small = [vec(lam_q1), vec(lam_k1), vec(lam_q2), vec(lam_k2), vec(subln_g)]
    return pl.pallas_call(
        functools.partial(_diff_attn_kernel, lam_init=lam_init),
        grid=(bsz,),
        in_specs=[pl.BlockSpec((1, seq, w), lambda b: (b, 0, 0)),
                  pl.BlockSpec((1, seq, w), lambda b: (b, 0, 0)),
                  pl.BlockSpec((1,) + vat.shape[1:], lambda b: (b, 0, 0, 0)),
                  full(pos)] + [full(a) for a in small],
        out_specs=pl.BlockSpec((1, seq, w), lambda b: (b, 0, 0)),
        out_shape=jax.ShapeDtypeStruct((bsz, seq, w), BF16),
        compiler_params=_params(("arbitrary",)),
        name="diff_attn",
    )(qa, ka, vat, pos, *small)


def _dsa_kernel(cq_ref, wuq_ref, wqidx_ref, wt_ref, kv_ref, kvt_ref, kidx_ref, pos_ref, wuv_ref, o_ref,
                key_ref, thr_ref, lim_ref, *, topk):
    tiles, tq = wt_ref.shape[1], wt_ref.shape[3]
    kc_rows = kvt_ref.shape[3]
    slopes = _alibi(B_HEADS)
    group = 2
    ngroups = B_HEADS // group

    def query_operands(j):
        cq = cq_ref[0, pl.ds(pl.multiple_of(j * tq, tq), tq), :]
        qidx = _dot(cq, wqidx_ref[...])
        qlat = (_dot(cq, wuq_ref[...]) * (B_DLAT ** -0.5 * LOG2E)).astype(BF16)
        head_of_lane = lax.broadcasted_iota(I32, qidx.shape, 1) // IDX_DH
        qstack = jnp.concatenate([jnp.where(head_of_lane == h, qidx, 0.0) for h in range(IDX_HEADS)],
                                 axis=0).astype(BF16)
        qs = jnp.concatenate(
            [jnp.concatenate([qlat[:, h * B_DLAT:(h + 1) * B_DLAT], _alibi_coef(tq, slopes[h])], axis=1)
             for h in range(B_HEADS)], axis=0)
        return qstack, qs

    def to_float(u):
        s = jnp.maximum(u ^ INT_MIN, ORDER_OF_NEG_INF)
        return pltpu.bitcast(s ^ ((s >> 31) & 0x7FFFFFFF), F32)

    def variant(n, qi):
        rows = n * kc_rows
        qstack, qs = query_operands(qi)
        wt = wt_ref[0, qi]
        kpos = lax.broadcasted_iota(I32, (rows, tq), 0)
        causal = kpos <= qi * tq + lax.broadcasted_iota(I32, (rows, tq), 1)
        krow = kpos[:SUBLANES]

        rel = _nt_dot(kidx_ref[0, :rows, :], qstack)
        isc = jnp.zeros((rows, tq), F32)
        for h in range(IDX_HEADS):
            isc = isc + wt[h:h + 1, :] * jnp.maximum(rel[:, h * tq:(h + 1) * tq], 0.0)
        key_ref[:rows, :] = jnp.where(causal, isc, NEG_INF)

        def count(pred):
            accs = [jnp.zeros((SUBLANES, tq), I32) for _ in range(4)]
            for r in range(rows // SUBLANES):
                ind = pred(key_ref[r * SUBLANES:(r + 1) * SUBLANES, :], r * SUBLANES + krow)
                accs[r % 4] = accs[r % 4] + ind.astype(I32)
            return jnp.sum((accs[0] + accs[1]) + (accs[2] + accs[3]), axis=0, keepdims=True)

        thr_ref[...] = jnp.full((1, tq), NEG_INF, F32)
        lim_ref[...] = jnp.full((1, tq), 2 ** 30, I32)
        if rows > topk:
            def bit_step(t, state):
                cand, trial, thr = state
                nxt = lax.shift_right_logical(lax.shift_right_logical(jnp.int32(INT_MIN), t), 1)
                thr_hit, thr_miss = to_float(trial | nxt), to_float(cand | nxt)
                hit = count(lambda x, p: x >= thr) >= topk
                cand = jnp.where(hit, trial, cand)
                return cand, cand | nxt, jnp.where(hit, thr_hit, thr_miss)
            top = jnp.full((1, tq), INT_MIN, I32)
            cand = lax.fori_loop(0, 32, bit_step, (jnp.zeros((1, tq), I32), top, to_float(top)))[0]
            thr = to_float(cand)
            thr_ref[...] = thr
            need = topk - count(lambda x, p: x > thr)
            n_eq = count(lambda x, p: x == thr)

            @pl.when(jnp.max((n_eq > need).astype(I32)) > 0)
            def _():
                def pos_step(t, lim):
                    trial = lim | jnp.left_shift(jnp.int32(1), 11 - t)
                    cnt = count(lambda x, p: (x == thr) & (p < trial))
                    return jnp.where(cnt < need, trial, lim)
                lim_ref[...] = lax.fori_loop(0, 12, pos_step, jnp.zeros((1, tq), I32))

        thr = thr_ref[...]
        lim = lim_ref[...]
        x = key_ref[:rows, :]
        sel = ((x > thr) | ((x == thr) & (kpos <= lim))) & causal
        bias = jnp.where(sel, 0.0, NEG_INF)
        bias = jnp.concatenate([bias] * group, axis=1)
        kaug = jnp.concatenate([kv_ref[0, :rows, :], pos_ref[:rows, :]], axis=1)
        kvt = jnp.concatenate([kvt_ref[0, c] for c in range(n)], axis=1)
        s_all = _nt_dot(kaug, qs)
        ps = []
        for g in range(ngroups):
            s = s_all[:, g * group * tq:(g + 1) * group * tq] + bias
            m = jnp.max(s, axis=0, keepdims=True)
            ps.append(jnp.exp2(s - m).astype(BF16))
        acc = _dot(kvt, jnp.concatenate(ps, axis=1))

        o = acc[:B_DLAT] * (1.0 / acc[B_DLAT:B_DLAT + 1])
        o_all = jnp.concatenate([o[:, h * tq:(h + 1) * tq].T for h in range(B_HEADS)], axis=1).astype(BF16)
        o_ref[0, pl.ds(pl.multiple_of(qi * tq, tq), tq), :] = _dot(o_all, wuv_ref[...]).astype(o_ref.dtype)

    def tile(qi, carry):
        nch = (qi * tq) // kc_rows + 1
        for n in range(1, key_ref.shape[0] // kc_rows + 1):
            pl.when(nch == n)(functools.partial(variant, n, qi))
        return carry

    lax.fori_loop(0, tiles, tile, 0)


def _dsa(cq, w_uq, w_qidx, widxt, ckv, ckvt, kidx, pos, w_uv):
    bsz, seq, _ = cq.shape
    wuq = w_uq.reshape(B_DQLAT, B_HEADS * B_DLAT).astype(BF16)
    wqidx = w_qidx.reshape(B_DQLAT, IDX_HEADS * IDX_DH).astype(BF16)
    tq = min(B_TQ, seq)
    topk = min(TOPK_MAX, seq // 4)
    eye = jnp.eye(B_HEADS, dtype=w_uv.dtype)
    wuv = jnp.einsum('hde,hg->hdge', w_uv, eye).reshape(B_HEADS * B_DLAT, B_HEADS * B_DV).astype(BF16)
    return pl.pallas_call(
        functools.partial(_dsa_kernel, topk=topk),
        grid=(bsz,),
        in_specs=[pl.BlockSpec((1, seq, B_DQLAT), lambda b: (b, 0, 0)),
                  pl.BlockSpec(wuq.shape, lambda b: (0, 0)),
                  pl.BlockSpec(wqidx.shape, lambda b: (0, 0)),
                  pl.BlockSpec((1,) + widxt.shape[1:], lambda b: (b, 0, 0, 0)),
                  pl.BlockSpec((1, seq, B_DLAT), lambda b: (b, 0, 0)),
                  pl.BlockSpec((1,) + ckvt.shape[1:], lambda b: (b, 0, 0, 0)),
                  pl.BlockSpec((1, seq, IDX_HEADS * IDX_DH), lambda b: (b, 0, 0)),
                  pl.BlockSpec(pos.shape, lambda b: (0, 0)),
                  pl.BlockSpec(wuv.shape, lambda b: (0, 0))],
        out_specs=pl.BlockSpec((1, seq, B_HEADS * B_DV), lambda b: (b, 0, 0)),
        out_shape=jax.ShapeDtypeStruct((bsz, seq, B_HEADS * B_DV), BF16),
        scratch_shapes=[pltpu.VMEM((seq, tq), F32),
                        pltpu.VMEM((1, tq), F32), pltpu.VMEM((1, tq), I32)],
        compiler_params=_params(("arbitrary",)),
        name="dsa",
    )(cq, wuq, wqidx, widxt, ckv, ckvt, kidx, pos, wuv)


def _swa_block(q, kp, kc, vtp, vtc, pos, sink_ref, mask):
    w = q.shape[0]
    rep = C_HEADS // C_KV_HEADS
    half = LANES // 2
    lane = lax.broadcasted_iota(I32, (w, LANES), 1)
    slopes = _alibi(C_HEADS)
    qs = []
    for h in range(C_HEADS):
        g = h // rep
        x = q[:, (h // 2) * LANES:(h // 2 + 1) * LANES].astype(F32)
        if (h % 2) != g:
            x = pltpu.roll(x, half, axis=1)
        x = jnp.where((lane // half) == g, x, 0.0).astype(BF16)
        qs.append(jnp.concatenate([x, _alibi_coef(w, slopes[h])], axis=1))
    kaug = jnp.concatenate([jnp.concatenate([kp, kc], axis=0), pos], axis=1)
    vt = jnp.concatenate([vtp, vtc], axis=1)
    s_all = _nt_dot(kaug, jnp.concatenate(qs, axis=0))
    ps, rs = [], []
    for h in range(C_HEADS):
        s = s_all[:, h * w:(h + 1) * w] + mask
        sink = LOG2E * (sink_ref[h:h + 1, :] + slopes[h] * (w + lane[:1, :]).astype(F32))
        m = jnp.maximum(jnp.max(s, axis=0, keepdims=True), sink)
        p = jnp.exp2(s - m)
        rs.append(1.0 / (jnp.sum(p, axis=0, keepdims=True) + jnp.exp2(sink - m)))
        ps.append(p.astype(BF16))
    o = _dot(vt, jnp.concatenate(ps, axis=1)) * jnp.concatenate(rs, axis=1)
    ot = [o[(h // rep) * C_DH:(h // rep + 1) * C_DH, h * w:(h + 1) * w] for h in range(C_HEADS)]
    return jnp.concatenate(ot, axis=0).T


def _layer1_kernel(oa_ref, ob_ref, sg_ref, x_ref, gmod0_ref, wout0_ref, shift_ref, scale_ref, ng_ref, win_ref,
                   pos_ref, sink_ref, dww_ref, dwb_ref, lng_ref, lnb_ref, gmod1_ref, wout1_ref, fg_ref,
                   o_ref, mask_ref, kprev_ref, vtprev_ref, buf_ref, shift_buf_ref):
    b = pl.program_id(0)
    i = pl.program_id(1)
    t = x_ref.shape[1]
    w = WINDOW
    halo = CONV_HALO

    @pl.when((b == 0) & (i == 0))
    def _():
        k = lax.broadcasted_iota(I32, (2 * w, w), 0)
        q = lax.broadcasted_iota(I32, (2 * w, w), 1)
        dist = w + q - k
        valid = (dist >= 0) & (dist < w)
        mask_ref[1] = jnp.where(valid, 0.0, NEG_INF)
        mask_ref[0] = jnp.where(valid & (k >= w), 0.0, NEG_INF)
        kprev_ref[...] = jnp.zeros(kprev_ref.shape, kprev_ref.dtype)
        vtprev_ref[...] = jnp.zeros(vtprev_ref.shape, vtprev_ref.dtype)
        buf_ref[:halo, :] = jnp.zeros((halo, buf_ref.shape[1]), F32)

    y = jnp.concatenate([oa_ref[0], ob_ref[0]], axis=1).astype(F32) * sg_ref[0].astype(F32)
    x1 = x_ref[0] + gmod0_ref[0] * _dot(y.astype(BF16), wout0_ref[...])
    h = _rms(x1, ng_ref[...]) * (1.0 + scale_ref[0]) + shift_ref[0]
    p = _dot(h.astype(BF16), win_ref[...])
    qw = C_HEADS * C_DH
    kw = C_KV_HEADS * C_DH
    qc = (p[:, :qw] * (C_DH ** -0.5 * LOG2E)).astype(BF16)
    kc = p[:, qw:qw + kw].astype(BF16)
    vct = p[:, qw + kw:qw + 2 * kw].T.astype(BF16)
    off = qw + 2 * kw
    hglu = p[:, off:off + D_CH] * jax.nn.sigmoid(p[:, off + D_CH:off + 2 * D_CH])
    gate = p[:, off + 2 * D_CH:]
    sg1 = gate * jax.nn.sigmoid(gate)

    oc = []
    for j in range(t // w):
        kp = kprev_ref[...] if j == 0 else kc[(j - 1) * w:j * w]
        vtp = vtprev_ref[...] if j == 0 else vct[:, (j - 1) * w:j * w]
        mask = mask_ref[jnp.minimum(i, 1)] if j == 0 else mask_ref[1]
        oc.append(_swa_block(qc[j * w:(j + 1) * w], kp, kc[j * w:(j + 1) * w], vtp, vct[:, j * w:(j + 1) * w],
                             pos_ref[...], sink_ref, mask))
    kprev_ref[...] = kc[t - w:]
    vtprev_ref[...] = vct[:, t - w:]

    buf_ref[:halo, :] = jnp.where(i > 0, buf_ref[:halo, :], 0.0)
    buf_ref[halo:, :] = hglu
    span = shift_buf_ref.shape[1]
    for ph in range(1, SUBLANES):
        shift_buf_ref[ph - 1] = buf_ref[ph:ph + span, :]
    acc = jnp.zeros((t, D_CH), F32) + dwb_ref[...]
    first = halo - (CONV_W - 1)
    for j in range(CONV_W):
        base, ph = (first + j) // SUBLANES * SUBLANES, (first + j) % SUBLANES
        rows = buf_ref[base:base + t, :] if ph == 0 else shift_buf_ref[ph - 1, base:base + t, :]
        acc = acc + dww_ref[j:j + 1, :] * rows
    buf_ref[:halo, :] = hglu[t - halo:]
    mu = jnp.mean(acc, axis=-1, keepdims=True)
    xc = acc - mu
    var = jnp.mean(xc * xc, axis=-1, keepdims=True)
    yn = xc * lax.rsqrt(var + EPS) * lng_ref[...] + lnb_ref[...]
    od = yn * jax.nn.sigmoid(yn)

    y1 = jnp.concatenate([jnp.concatenate(oc, axis=0), od], axis=1) * sg1
    x2 = x1 + gmod1_ref[0] * _dot(y1.astype(BF16), wout1_ref[...])
    o_ref[0] = _rms(x2, fg_ref[...])


def _layer1(oa, ob, sg, x, gmod0, w_out0, shift, scale, norm_g, w_in, pos, sinks, dw_w, dw_b, ln_g, ln_b,
            gmod1, w_out1, final_g):
    bsz, seq, d = x.shape
    t = min(LAYER1_TILE, seq)
    w = WINDOW
    kw = C_KV_HEADS * C_DH
    wout0 = w_out0.astype(BF16)
    win = w_in.astype(BF16)
    wout1 = w_out1.astype(BF16)
    vec = lambda a: a.reshape(1, -1)
    sink_rows = jnp.broadcast_to(sinks.astype(F32)[:, None], (C_HEADS, w))
    row = lambda width: pl.BlockSpec((1, t, width), lambda b, i: (b, i, 0))
    mod = pl.BlockSpec((1, 1, d), lambda b, i: (b, 0, 0))
    full = lambda a: pl.BlockSpec(a.shape, lambda b, i: (0,) * a.ndim)
    consts = [vec(norm_g), win]
    tail = [sink_rows, dw_w, vec(dw_b), vec(ln_g), vec(ln_b)]
    return pl.pallas_call(
        _layer1_kernel,
        grid=(bsz, seq // t),
        in_specs=[row(oa.shape[2]), row(ob.shape[2]), row(d), row(d), mod, full(wout0), mod, mod]
                 + [full(a) for a in consts]
                 + [pl.BlockSpec((2 * w, LANES), lambda b, i: (0, 0))]
                 + [full(a) for a in tail]
                 + [mod, full(wout1), full(vec(final_g))],
        out_specs=row(d),
        out_shape=jax.ShapeDtypeStruct((bsz, seq, d), F32),
        scratch_shapes=[pltpu.VMEM((2, 2 * w, w), F32),
                        pltpu.VMEM((w, kw), BF16), pltpu.VMEM((kw, w), BF16),
                        pltpu.VMEM((CONV_HALO + t, D_CH), F32),
                        pltpu.VMEM((SUBLANES - 1, CONV_HALO - SUBLANES + t, D_CH), F32)],
        compiler_params=_params(("arbitrary", "arbitrary")),
        name="layer1",
    )(oa, ob, sg, x, gmod0, wout0, shift, scale, *consts, pos, *tail, gmod1, wout1, vec(final_g))


def kernel(x, c, ada_w, ada_b, norm_g, ab_w_in, ab_q_norm_g, ab_kv_norm_g, ab_w_uq, ab_w_qidx, ab_w_uv,
           ab_lam_q1, ab_lam_k1, ab_lam_q2, ab_lam_k2, ab_subln_g, ab_w_out,
           cd_w_in, cd_sinks, cd_dw_w, cd_dw_b, cd_ln_g, cd_ln_b, cd_w_out, final_g):
    bsz, seq, d = x.shape
    mods = _ada_mods(c, ada_w, ada_b)
    mod = lambda l, k: mods[l, :, k * d:(k + 1) * d].reshape(bsz, 1, d)
    pos = _key_positions(seq)
    qa, ka, vat, cq, ckv, ckvt, kidx, widxt, sg0 = _proj0(
        x, mod(0, 0), mod(0, 1), norm_g[0], ab_w_in[0], ab_q_norm_g[0], ab_kv_norm_g[0])
    oa = _diff_attn(qa, ka, vat, pos, ab_lam_q1[0], ab_lam_k1[0], ab_lam_q2[0], ab_lam_k2[0], ab_subln_g[0], 0)
    ob = _dsa(cq, ab_w_uq[0], ab_w_qidx[0], widxt, ckv, ckvt, kidx, pos, ab_w_uv[0])
    return _layer1(oa, ob, sg0, x, mod(0, 2), ab_w_out[0], mod(1, 0), mod(1, 1), norm_g[1], cd_w_in[0], pos,
                   cd_sinks[0], cd_dw_w[0], cd_dw_b[0], cd_ln_g[0], cd_ln_b[0], mod(1, 2), cd_w_out[0], final_g)
```

```python
import functools
import math

import numpy as np
import jax
import jax.numpy as jnp
from jax import lax
from jax.experimental import pallas as pl
from jax.experimental.pallas import tpu as pltpu

F32 = jnp.float32
BF16 = jnp.bfloat16
I32 = jnp.int32

EPS = 1e-6
A_HEADS = 4
A_DH = 64
A_DV = 128
B_HEADS = 8
B_DQLAT = 128
B_DLAT = 128
B_DV = 64
IDX_HEADS = 8
IDX_DH = 32
TOPK_MAX = 256
C_HEADS = 8
C_KV_HEADS = 2
C_DH = 64
WINDOW = 128
D_CH = 512
CONV_W = 31

LANES = 128
SUBLANES = 8
INT_MIN = -2 ** 31
ORDER_OF_NEG_INF = INT_MIN + 0x7FFFFF
NEG_INF = float("-inf")
MIN_NORMAL = float(np.finfo(np.float32).tiny)
VMEM_LIMIT = 48 * 1024 * 1024

ROW_TILE = 1024
LAYER1_TILE = 512
KEY_CHUNK = 256
B_TQ = 128
POS_SPLIT = 16
ONES_ROWS = 16
LOG2E = math.log2(math.e)
COEF_TERMS = 4
CONV_HALO = 32


def _alibi(n):
    return [float(2.0 ** (-8.0 * i / n)) for i in range(1, n + 1)]


def _params(sem):
    return pltpu.CompilerParams(dimension_semantics=sem, vmem_limit_bytes=VMEM_LIMIT)


def _nt_dot(a, b):
    return lax.dot_general(a, b, (((1,), (1,)), ((), ())), preferred_element_type=F32)


def _dot(a, b):
    return jnp.dot(a, b, preferred_element_type=F32)


def _rms(x, g):
    return x * lax.rsqrt(jnp.mean(x * x, axis=-1, keepdims=True) + EPS) * g


def _bf16_terms(x, n):
    terms = []
    for _ in range(n):
        t = float(np.asarray(x, np.float32).astype(jnp.bfloat16).astype(np.float32))
        terms.append(t)
        x -= t
    return terms


def _alibi_coef(rows, slope):
    lane = lax.broadcasted_iota(I32, (rows, LANES), 1)
    out = jnp.zeros((rows, LANES), F32)
    for i, t in enumerate(_bf16_terms(LOG2E * slope, COEF_TERMS)):
        out = jnp.where(lane == 2 * i, POS_SPLIT * t, jnp.where(lane == 2 * i + 1, t, out))
    return out.astype(BF16)


def _key_positions(seq):
    pos = np.zeros((seq, LANES), np.float32)
    for i in range(COEF_TERMS):
        pos[:, 2 * i] = np.arange(seq) // POS_SPLIT
        pos[:, 2 * i + 1] = np.arange(seq) % POS_SPLIT
    return jnp.asarray(pos, BF16)


def _ada_kernel(c_ref, w_ref, b_ref, o_ref):
    c = c_ref[...]
    sc = c * jax.nn.sigmoid(c)
    o_ref[0] = jnp.dot(sc.astype(BF16), w_ref[0].astype(BF16), preferred_element_type=F32) + b_ref[0]


def _ada_mods(c, ada_w, ada_b):
    depth, d, d3 = ada_w.shape
    bsz = c.shape[0]
    nt = d3 // d
    return pl.pallas_call(
        _ada_kernel,
        grid=(depth, nt),
        in_specs=[pl.BlockSpec((bsz, d), lambda l, j: (0, 0)),
                  pl.BlockSpec((1, d, d), lambda l, j: (l, 0, j)),
                  pl.BlockSpec((1, 1, d), lambda l, j: (l, 0, j))],
        out_specs=pl.BlockSpec((1, bsz, d), lambda l, j: (l, 0, j)),
        out_shape=jax.ShapeDtypeStruct((depth, bsz, d3), F32),
        compiler_params=_params(("arbitrary", "arbitrary")),
        name="ada_mods",
    )(c, ada_w, ada_b.reshape(depth, 1, d3))


def _proj0_kernel(x_ref, shift_ref, scale_ref, ng_ref, wmain_ref, wgate_ref, wkidx_ref, wwidx_ref,
                  qng_ref, kvng_ref,
                  qa_ref, ka_ref, vat_ref, cq_ref, ckv_ref, ckvt_ref, kidx_ref, widxt_ref, sg_ref):
    x = x_ref[0]
    h = _rms(x, ng_ref[...]) * (1.0 + scale_ref[0]) + shift_ref[0]
    hb = h.astype(BF16)
    main = jnp.dot(hb, wmain_ref[...], preferred_element_type=F32)
    qw = A_HEADS * 2 * A_DH
    qa_ref[0] = (main[:, :qw] * (A_DH ** -0.5 * LOG2E)).astype(BF16)
    ka_ref[0] = main[:, qw:2 * qw].astype(BF16)
    va = main[:, 2 * qw:3 * qw]
    cq_ref[0] = _rms(main[:, 3 * qw:3 * qw + B_DQLAT], qng_ref[...]).astype(BF16)
    ckv = _rms(main[:, 3 * qw + B_DQLAT:], kvng_ref[...])
    ckv_ref[0] = ckv.astype(BF16)
    ch = vat_ref.shape[3]
    ones = jnp.ones((ONES_ROWS, ch), F32)
    for c in range(vat_ref.shape[1]):
        vt = va[c * ch:(c + 1) * ch].T
        vat_ref[0, c] = jnp.concatenate(
            [blk for h in range(A_HEADS) for blk in (vt[h * A_DV:(h + 1) * A_DV], ones)], axis=0).astype(BF16)
        ckvt_ref[0, c] = jnp.concatenate([ckv[c * ch:(c + 1) * ch].T, ones], axis=0).astype(BF16)
    kidx_ref[0] = jnp.dot(hb, wkidx_ref[...], preferred_element_type=F32).astype(BF16)
    widxt = _nt_dot(wwidx_ref[...], hb)
    for j in range(widxt_ref.shape[1]):
        widxt_ref[0, j] = widxt[:, j * B_TQ:(j + 1) * B_TQ]
    gate = jnp.dot(hb, wgate_ref[...], preferred_element_type=F32)
    sg_ref[0] = (gate * jax.nn.sigmoid(gate)).astype(BF16)


def _proj0(x, shift, scale, norm_g, w_in, q_norm_g, kv_norm_g):
    bsz, seq, d = x.shape
    t = min(ROW_TILE, seq)
    ch = min(KEY_CHUNK, seq)
    qw = A_HEADS * 2 * A_DH
    o_main = 3 * qw + B_DQLAT + B_DLAT
    wmain = w_in[:, :o_main].astype(BF16)
    wkidx = jnp.tile(w_in[:, o_main:o_main + IDX_DH], (1, IDX_HEADS)).astype(BF16)
    wwidx = w_in[:, o_main + IDX_DH:o_main + IDX_DH + IDX_HEADS].T.astype(BF16)
    wgate = w_in[:, o_main + IDX_DH + IDX_HEADS:].astype(BF16)
    row = lambda w: pl.BlockSpec((1, t, w), lambda b, i: (b, i, 0))
    rows = lambda w: (jax.ShapeDtypeStruct((bsz, seq, w), BF16), row(w))
    chunked = lambda w: (jax.ShapeDtypeStruct((bsz, seq // ch, w, ch), BF16),
                         pl.BlockSpec((1, t // ch, w, ch), lambda b, i: (b, i, 0, 0)))
    mod = pl.BlockSpec((1, 1, d), lambda b, i: (b, 0, 0))
    full = lambda a: pl.BlockSpec(a.shape, lambda b, i: (0,) * a.ndim)
    ng = norm_g.reshape(1, d)
    qng = q_norm_g.reshape(1, B_DQLAT)
    kvng = kv_norm_g.reshape(1, B_DLAT)
    iw = IDX_HEADS * IDX_DH
    outs = [rows(qw), rows(qw), chunked(A_HEADS * (A_DV + ONES_ROWS)), rows(B_DQLAT), rows(B_DLAT),
            chunked(B_DLAT + ONES_ROWS), rows(iw),
            (jax.ShapeDtypeStruct((bsz, seq // B_TQ, IDX_HEADS, B_TQ), F32),
             pl.BlockSpec((1, t // B_TQ, IDX_HEADS, B_TQ), lambda b, i: (b, i, 0, 0))),
            rows(d)]
    return pl.pallas_call(
        _proj0_kernel,
        grid=(bsz, seq // t),
        in_specs=[row(d), mod, mod, full(ng), full(wmain), full(wgate), full(wkidx), full(wwidx),
                  full(qng), full(kvng)],
        out_specs=[o[1] for o in outs],
        out_shape=[o[0] for o in outs],
        compiler_params=_params(("arbitrary", "arbitrary")),
        name="proj0",
    )(x, shift, scale, ng, wmain, wgate, wkidx, wwidx, qng, kvng)


def _diff_attn_kernel(q_ref, k_ref, vt_ref, pos_ref, lq1_ref, lk1_ref, lq2_ref, lk2_ref, g_ref, o_ref,
                      *, lam_init):
    tk = vt_ref.shape[3]
    tq = tk
    lam = (jnp.exp(jnp.sum(lq1_ref[...] * lk1_ref[...], keepdims=True))
           - jnp.exp(jnp.sum(lq2_ref[...] * lk2_ref[...], keepdims=True)) + lam_init)
    krow = lax.broadcasted_iota(I32, (tk, 2 * tq), 0)
    qcol = lax.broadcasted_iota(I32, (tk, 2 * tq), 1) % tq
    causal = krow <= qcol
    slopes = _alibi(A_HEADS)
    heads = [slice(h * A_DV, (h + 1) * A_DV) for h in range(A_HEADS)]
    first = lax.broadcasted_iota(I32, (tq, A_DV), 1) < A_DH
    zero = jnp.zeros((tq, A_DV), BF16)
    rows_v = vt_ref.shape[2] // A_HEADS
    vrows = [slice(h * rows_v, (h + 1) * rows_v) for h in range(A_HEADS)]

    def variant(n, i):
        rows = n * tk
        pos = pos_ref[:rows, :]
        qrows = pl.ds(pl.multiple_of(i * tq, tq), tq)
        qs = [jnp.concatenate(
            [jnp.concatenate([jnp.where(first, q_ref[0, qrows, hs], zero), _alibi_coef(tq, slopes[h])], axis=1),
             jnp.concatenate([jnp.where(first, zero, q_ref[0, qrows, hs]), _alibi_coef(tq, slopes[h])], axis=1)],
            axis=0) for h, hs in enumerate(heads)]
        for h, hs in enumerate(heads):
            s = _nt_dot(jnp.concatenate([k_ref[0, :rows, hs], pos], axis=1), qs[h])
            last = jnp.where(causal, s[rows - tk:], NEG_INF)
            s = last if n == 1 else jnp.concatenate([s[:rows - tk], last], axis=0)
            m = jnp.max(s, axis=0, keepdims=True)
            vt = jnp.concatenate([vt_ref[0, c, vrows[h], :] for c in range(n)], axis=1)
            acc = _dot(vt, jnp.exp2(s - m).astype(BF16))
            o = acc[:A_DV] * (1.0 / acc[A_DV:A_DV + 1])
            od = (o[:, :tq] - lam * o[:, tq:]).T
            o_ref[0, qrows, hs] = (_rms(od, g_ref[...]) * (1.0 - lam_init)).astype(o_ref.dtype)

    def tile(i, carry):
        for n in range(1, vt_ref.shape[1] + 1):
            pl.when(i + 1 == n)(functools.partial(variant, n, i))
        return carry

    lax.fori_loop(0, vt_ref.shape[1], tile, 0)


def _diff_attn(qa, ka, vat, pos, lam_q1, lam_k1, lam_q2, lam_k2, subln_g, layer_idx):
    bsz, seq, w = qa.shape
    tq = vat.shape[3]
    lam_init = 0.8 - 0.6 * math.exp(-0.3 * layer_idx)
    vec = lambda a: a.reshape(1, -1)
    full = lambda a: pl.BlockSpec(a.shape, lambda b: (0,) * a.ndim)
    small = [vec(lam_q1), vec(lam_k1), vec(lam_q2), vec(lam_k2), vec(subln_g)]
    return pl.pallas_call(
        functools.partial(_diff_attn_kernel, lam_init=lam_init),
        grid=(bsz,),
        in_specs=[pl.BlockSpec((1, seq, w), lambda b: (b, 0, 0)),
                  pl.BlockSpec((1, seq, w), lambda b: (b, 0, 0)),
                  pl.BlockSpec((1,) + vat.shape[1:], lambda b: (b, 0, 0, 0)),
                  full(pos)] + [full(a) for a in small],
        out_specs=pl.BlockSpec((1, seq, w), lambda b: (b, 0, 0)),
        out_shape=jax.ShapeDtypeStruct((bsz, seq, w), BF16),
        compiler_params=_params(("arbitrary",)),
        name="diff_attn",
    )(qa, ka, vat, pos, *small)


def _dsa_kernel(cq_ref, wuq_ref, wqidx_ref, wt_ref, kv_ref, kvt_ref, kidx_ref, pos_ref, wuv_ref, o_ref,
                key_ref, thr_ref, lim_ref, *, topk):
    tiles, tq = wt_ref.shape[1], wt_ref.shape[3]
    kc_rows = kvt_ref.shape[3]
    slopes = _alibi(B_HEADS)
    group = 2
    ngroups = B_HEADS // group

    def query_operands(j):
        cq = cq_ref[0, pl.ds(pl.multiple_of(j * tq, tq), tq), :]
        qidx = _dot(cq, wqidx_ref[...])
        qlat = (_dot(cq, wuq_ref[...]) * (B_DLAT ** -0.5 * LOG2E)).astype(BF16)
        head_of_lane = lax.broadcasted_iota(I32, qidx.shape, 1) // IDX_DH
        qstack = jnp.concatenate([jnp.where(head_of_lane == h, qidx, 0.0) for h in range(IDX_HEADS)],
                                 axis=0).astype(BF16)
        qs = jnp.concatenate(
            [jnp.concatenate([qlat[:, h * B_DLAT:(h + 1) * B_DLAT], _alibi_coef(tq, slopes[h])], axis=1)
             for h in range(B_HEADS)], axis=0)
        return qstack, qs

    def to_float(u):
        s = jnp.maximum(u ^ INT_MIN, ORDER_OF_NEG_INF)
        return pltpu.bitcast(s ^ ((s >> 31) & 0x7FFFFFFF), F32)

    def variant(n, qi):
        rows = n * kc_rows
        qstack, qs = query_operands(qi)
        wt = wt_ref[0, qi]
        kpos = lax.broadcasted_iota(I32, (rows, tq), 0)
        causal = kpos <= qi * tq + lax.broadcasted_iota(I32, (rows, tq), 1)
        krow = kpos[:SUBLANES]

        rel = _nt_dot(kidx_ref[0, :rows, :], qstack)
        isc = jnp.zeros((rows, tq), F32)
        for h in range(IDX_HEADS):
            isc = isc + wt[h:h + 1, :] * jnp.maximum(rel[:, h * tq:(h + 1) * tq], 0.0)
        key_ref[:rows, :] = jnp.where(causal, isc, NEG_INF)

        def count(pred):
            accs = [jnp.zeros((SUBLANES, tq), I32) for _ in range(4)]
            for r in range(rows // SUBLANES):
                ind = pred(key_ref[r * SUBLANES:(r + 1) * SUBLANES, :], r * SUBLANES + krow)
                accs[r % 4] = accs[r % 4] + ind.astype(I32)
            return jnp.sum((accs[0] + accs[1]) + (accs[2] + accs[3]), axis=0, keepdims=True)

        lim_ref[...] = jnp.full((1, tq), 2 ** 30, I32)
        if rows > topk:
            def bit_step(t, state):
                cand, trial, thr = state
                nxt = lax.shift_right_logical(lax.shift_right_logical(jnp.int32(INT_MIN), t), 1)
                thr_hit, thr_miss = to_float(trial | nxt), to_float(cand | nxt)
                hit = count(lambda x, p: x >= thr) >= topk
                cand = jnp.where(hit, trial, cand)
                return cand, cand | nxt, jnp.where(hit, thr_hit, thr_miss)
            top = jnp.full((1, tq), INT_MIN, I32)
            cand = lax.fori_loop(0, 32, bit_step, (jnp.zeros((1, tq), I32), top, to_float(top)))[0]
            thr = to_float(cand)
            thr_ref[0:1, :] = thr
            thr_ref[1:2, :] = jnp.where(thr == 0.0, MIN_NORMAL, to_float(cand + 1))
            need = topk - count(lambda x, p: x > thr)
            n_eq = count(lambda x, p: x == thr)

            @pl.when(jnp.max((n_eq > need).astype(I32)) > 0)
            def _():
                def pos_step(t, lim):
                    trial = lim | jnp.left_shift(jnp.int32(1), 11 - t)
                    cnt = count(lambda x, p: (x == thr) & (p < trial))
                    return jnp.where(cnt < need, trial, lim)
                lim = lax.fori_loop(0, 12, pos_step, jnp.zeros((1, tq), I32))
                lim_ref[...] = jnp.where(need > 0, lim, 2 ** 30)

        if rows > topk:
            bound = jnp.where(kpos <= lim_ref[...], thr_ref[0:1, :], thr_ref[1:2, :])
            bias = jnp.where(key_ref[:rows, :] >= bound, 0.0, NEG_INF)
        else:
            bias = jnp.where(causal, 0.0, NEG_INF)
        bias = jnp.concatenate([bias] * group, axis=1)
        kaug = jnp.concatenate([kv_ref[0, :rows, :], pos_ref[:rows, :]], axis=1)
        kvt = jnp.concatenate([kvt_ref[0, c] for c in range(n)], axis=1)
        s_all = _nt_dot(kaug, qs)
        ps = []
        for g in range(ngroups):
            s = s_all[:, g * group * tq:(g + 1) * group * tq] + bias
            m = jnp.max(s, axis=0, keepdims=True)
            ps.append(jnp.exp2(s - m).astype(BF16))
        acc = _dot(kvt, jnp.concatenate(ps, axis=1))

        o = acc[:B_DLAT] * (1.0 / acc[B_DLAT:B_DLAT + 1])
        o_all = jnp.concatenate([o[:, h * tq:(h + 1) * tq].T for h in range(B_HEADS)], axis=1).astype(BF16)
        o_ref[0, pl.ds(pl.multiple_of(qi * tq, tq), tq), :] = _dot(o_all, wuv_ref[...]).astype(o_ref.dtype)

    def tile(qi, carry):
        nch = (qi * tq) // kc_rows + 1
        for n in range(1, key_ref.shape[0] // kc_rows + 1):
            pl.when(nch == n)(functools.partial(variant, n, qi))
        return carry

    lax.fori_loop(0, tiles, tile, 0)


def _dsa(cq, w_uq, w_qidx, widxt, ckv, ckvt, kidx, pos, w_uv):
    bsz, seq, _ = cq.shape
    wuq = w_uq.reshape(B_DQLAT, B_HEADS * B_DLAT).astype(BF16)
    wqidx = w_qidx.reshape(B_DQLAT, IDX_HEADS * IDX_DH).astype(BF16)
    tq = min(B_TQ, seq)
    topk = min(TOPK_MAX, seq // 4)
    eye = jnp.eye(B_HEADS, dtype=w_uv.dtype)
    wuv = jnp.einsum('hde,hg->hdge', w_uv, eye).reshape(B_HEADS * B_DLAT, B_HEADS * B_DV).astype(BF16)
    return pl.pallas_call(
        functools.partial(_dsa_kernel, topk=topk),
        grid=(bsz,),
        in_specs=[pl.BlockSpec((1, seq, B_DQLAT), lambda b: (b, 0, 0)),
                  pl.BlockSpec(wuq.shape, lambda b: (0, 0)),
                  pl.BlockSpec(wqidx.shape, lambda b: (0, 0)),
                  pl.BlockSpec((1,) + widxt.shape[1:], lambda b: (b, 0, 0, 0)),
                  pl.BlockSpec((1, seq, B_DLAT), lambda b: (b, 0, 0)),
                  pl.BlockSpec((1,) + ckvt.shape[1:], lambda b: (b, 0, 0, 0)),
                  pl.BlockSpec((1, seq, IDX_HEADS * IDX_DH), lambda b: (b, 0, 0)),
                  pl.BlockSpec(pos.shape, lambda b: (0, 0)),
                  pl.BlockSpec(wuv.shape, lambda b: (0, 0))],
        out_specs=pl.BlockSpec((1, seq, B_HEADS * B_DV), lambda b: (b, 0, 0)),
        out_shape=jax.ShapeDtypeStruct((bsz, seq, B_HEADS * B_DV), BF16),
        scratch_shapes=[pltpu.VMEM((seq, tq), F32),
                        pltpu.VMEM((2, tq), F32), pltpu.VMEM((1, tq), I32)],
        compiler_params=_params(("arbitrary",)),
        name="dsa",
    )(cq, wuq, wqidx, widxt, ckv, ckvt, kidx, pos, wuv)


def _swa_block(q, kp, kc, vtp, vtc, pos, sink_ref, mask):
    w = q.shape[0]
    rep = C_HEADS // C_KV_HEADS
    half = LANES // 2
    lane = lax.broadcasted_iota(I32, (w, LANES), 1)
    slopes = _alibi(C_HEADS)
    qs = []
    for h in range(C_HEADS):
        g = h // rep
        x = q[:, (h // 2) * LANES:(h // 2 + 1) * LANES].astype(F32)
        if (h % 2) != g:
            x = pltpu.roll(x, half, axis=1)
        x = jnp.where((lane // half) == g, x, 0.0).astype(BF16)
        qs.append(jnp.concatenate([x, _alibi_coef(w, slopes[h])], axis=1))
    kaug = jnp.concatenate([jnp.concatenate([kp, kc], axis=0), pos], axis=1)
    vt = jnp.concatenate([vtp, vtc], axis=1)
    s_all = _nt_dot(kaug, jnp.concatenate(qs, axis=0))
    ps, rs = [], []
    for h in range(C_HEADS):
        s = s_all[:, h * w:(h + 1) * w] + mask
        sink = LOG2E * (sink_ref[h:h + 1, :] + slopes[h] * (w + lane[:1, :]).astype(F32))
        m = jnp.maximum(jnp.max(s, axis=0, keepdims=True), sink)
        p = jnp.exp2(s - m)
        rs.append(1.0 / (jnp.sum(p, axis=0, keepdims=True) + jnp.exp2(sink - m)))
        ps.append(p.astype(BF16))
    o = _dot(vt, jnp.concatenate(ps, axis=1)) * jnp.concatenate(rs, axis=1)
    ot = [o[(h // rep) * C_DH:(h // rep + 1) * C_DH, h * w:(h + 1) * w] for h in range(C_HEADS)]
    return jnp.concatenate(ot, axis=0).T


def _layer1_kernel(oa_ref, ob_ref, sg_ref, x_ref, gmod0_ref, wout0_ref, shift_ref, scale_ref, ng_ref, win_ref,
                   pos_ref, sink_ref, dww_ref, dwb_ref, lng_ref, lnb_ref, gmod1_ref, wout1_ref, fg_ref,
                   o_ref, mask_ref, kprev_ref, vtprev_ref, buf_ref, shift_buf_ref):
    b = pl.program_id(0)
    i = pl.program_id(1)
    t = x_ref.shape[1]
    w = WINDOW
    halo = CONV_HALO

    @pl.when((b == 0) & (i == 0))
    def _():
        k = lax.broadcasted_iota(I32, (2 * w, w), 0)
        q = lax.broadcasted_iota(I32, (2 * w, w), 1)
        dist = w + q - k
        valid = (dist >= 0) & (dist < w)
        mask_ref[1] = jnp.where(valid, 0.0, NEG_INF)
        mask_ref[0] = jnp.where(valid & (k >= w), 0.0, NEG_INF)
        kprev_ref[...] = jnp.zeros(kprev_ref.shape, kprev_ref.dtype)
        vtprev_ref[...] = jnp.zeros(vtprev_ref.shape, vtprev_ref.dtype)
        buf_ref[:halo, :] = jnp.zeros((halo, buf_ref.shape[1]), F32)

    y = jnp.concatenate([oa_ref[0], ob_ref[0]], axis=1).astype(F32) * sg_ref[0].astype(F32)
    x1 = x_ref[0] + gmod0_ref[0] * _dot(y.astype(BF16), wout0_ref[...])
    h = _rms(x1, ng_ref[...]) * (1.0 + scale_ref[0]) + shift_ref[0]
    p = _dot(h.astype(BF16), win_ref[...])
    qw = C_HEADS * C_DH
    kw = C_KV_HEADS * C_DH
    qc = (p[:, :qw] * (C_DH ** -0.5 * LOG2E)).astype(BF16)
    kc = p[:, qw:qw + kw].astype(BF16)
    vct = p[:, qw + kw:qw + 2 * kw].T.astype(BF16)
    off = qw + 2 * kw
    hglu = p[:, off:off + D_CH] * jax.nn.sigmoid(p[:, off + D_CH:off + 2 * D_CH])
    gate = p[:, off + 2 * D_CH:]
    sg1 = gate * jax.nn.sigmoid(gate)

    oc = []
    for j in range(t // w):
        kp = kprev_ref[...] if j == 0 else kc[(j - 1) * w:j * w]
        vtp = vtprev_ref[...] if j == 0 else vct[:, (j - 1) * w:j * w]
        mask = mask_ref[jnp.minimum(i, 1)] if j == 0 else mask_ref[1]
        oc.append(_swa_block(qc[j * w:(j + 1) * w], kp, kc[j * w:(j + 1) * w], vtp, vct[:, j * w:(j + 1) * w],
                             pos_ref[...], sink_ref, mask))
    kprev_ref[...] = kc[t - w:]
    vtprev_ref[...] = vct[:, t - w:]

    buf_ref[:halo, :] = jnp.where(i > 0, buf_ref[:halo, :], 0.0)
    buf_ref[halo:, :] = hglu
    span = shift_buf_ref.shape[1]
    for ph in range(1, SUBLANES):
        shift_buf_ref[ph - 1] = buf_ref[ph:ph + span, :]
    acc = jnp.zeros((t, D_CH), F32) + dwb_ref[...]
    first = halo - (CONV_W - 1)
    for j in range(CONV_W):
        base, ph = (first + j) // SUBLANES * SUBLANES, (first + j) % SUBLANES
        rows = buf_ref[base:base + t, :] if ph == 0 else shift_buf_ref[ph - 1, base:base + t, :]
        acc = acc + dww_ref[j:j + 1, :] * rows
    buf_ref[:halo, :] = hglu[t - halo:]
    mu = jnp.mean(acc, axis=-1, keepdims=True)
    xc = acc - mu
    var = jnp.mean(xc * xc, axis=-1, keepdims=True)
    yn = xc * lax.rsqrt(var + EPS) * lng_ref[...] + lnb_ref[...]
    od = yn * jax.nn.sigmoid(yn)

    y1 = jnp.concatenate([jnp.concatenate(oc, axis=0), od], axis=1) * sg1
    x2 = x1 + gmod1_ref[0] * _dot(y1.astype(BF16), wout1_ref[...])
    o_ref[0] = _rms(x2, fg_ref[...])


def _layer1(oa, ob, sg, x, gmod0, w_out0, shift, scale, norm_g, w_in, pos, sinks, dw_w, dw_b, ln_g, ln_b,
            gmod1, w_out1, final_g):
    bsz, seq, d = x.shape
    t = min(LAYER1_TILE, seq)
    w = WINDOW
    kw = C_KV_HEADS * C_DH
    wout0 = w_out0.astype(BF16)
    win = w_in.astype(BF16)
    wout1 = w_out1.astype(BF16)
    vec = lambda a: a.reshape(1, -1)
    sink_rows = jnp.broadcast_to(sinks.astype(F32)[:, None], (C_HEADS, w))
    row = lambda width: pl.BlockSpec((1, t, width), lambda b, i: (b, i, 0))
    mod = pl.BlockSpec((1, 1, d), lambda b, i: (b, 0, 0))
    full = lambda a: pl.BlockSpec(a.shape, lambda b, i: (0,) * a.ndim)
    consts = [vec(norm_g), win]
    tail = [sink_rows, dw_w, vec(dw_b), vec(ln_g), vec(ln_b)]
    return pl.pallas_call(
        _layer1_kernel,
        grid=(bsz, seq // t),
        in_specs=[row(oa.shape[2]), row(ob.shape[2]), row(d), row(d), mod, full(wout0), mod, mod]
                 + [full(a) for a in consts]
                 + [pl.BlockSpec((2 * w, LANES), lambda b, i: (0, 0))]
                 + [full(a) for a in tail]
                 + [mod, full(wout1), full(vec(final_g))],
        out_specs=row(d),
        out_shape=jax.ShapeDtypeStruct((bsz, seq, d), F32),
        scratch_shapes=[pltpu.VMEM((2, 2 * w, w), F32),
                        pltpu.VMEM((w, kw), BF16), pltpu.VMEM((kw, w), BF16),
                        pltpu.VMEM((CONV_HALO + t, D_CH), F32),
                        pltpu.VMEM((SUBLANES - 1, CONV_HALO - SUBLANES + t, D_CH), F32)],
        compiler_params=_params(("arbitrary", "arbitrary")),
        name="layer1",
    )(oa, ob, sg, x, gmod0, wout0, shift, scale, *consts, pos, *tail, gmod1, wout1, vec(final_g))


def kernel(x, c, ada_w, ada_b, norm_g, ab_w_in, ab_q_norm_g, ab_kv_norm_g, ab_w_uq, ab_w_qidx, ab_w_uv,
           ab_lam_q1, ab_lam_k1, ab_lam_q2, ab_lam_k2, ab_subln_g, ab_w_out,
           cd_w_in, cd_sinks, cd_dw_w, cd_dw_b, cd_ln_g, cd_ln_b, cd_w_out, final_g):
    bsz, seq, d = x.shape
    mods = _ada_mods(c, ada_w, ada_b)
    mod = lambda l, k: mods[l, :, k * d:(k + 1) * d].reshape(bsz, 1, d)
    pos = _key_positions(seq)
    qa, ka, vat, cq, ckv, ckvt, kidx, widxt, sg0 = _proj0(
        x, mod(0, 0), mod(0, 1), norm_g[0], ab_w_in[0], ab_q_norm_g[0], ab_kv_norm_g[0])
    oa = _diff_attn(qa, ka, vat, pos, ab_lam_q1[0], ab_lam_k1[0], ab_lam_q2[0], ab_lam_k2[0], ab_subln_g[0], 0)
    ob = _dsa(cq, ab_w_uq[0], ab_w_qidx[0], widxt, ckv, ckvt, kidx, pos, ab_w_uv[0])
    return _layer1(oa, ob, sg0, x, mod(0, 2), ab_w_out[0], mod(1, 0), mod(1, 1), norm_g[1], cd_w_in[0], pos,
                   cd_sinks[0], cd_dw_w[0], cd_dw_b[0], cd_ln_g[0], cd_ln_b[0], mod(1, 2), cd_w_out[0], final_g)
```

```python
import functools
import math

import numpy as np
import jax
import jax.numpy as jnp
from jax import lax
from jax.experimental import pallas as pl
from jax.experimental.pallas import tpu as pltpu

F32 = jnp.float32
BF16 = jnp.bfloat16
I32 = jnp.int32

EPS = 1e-6
A_HEADS = 4
A_DH = 64
A_DV = 128
B_HEADS = 8
B_DQLAT = 128
B_DLAT = 128
B_DV = 64
IDX_HEADS = 8
IDX_DH = 32
TOPK_MAX = 256
C_HEADS = 8
C_KV_HEADS = 2
C_DH = 64
WINDOW = 128
D_CH = 512
CONV_W = 31

LANES = 128
SUBLANES = 8
INT_MIN = -2 ** 31
ORDER_OF_NEG_INF = INT_MIN + 0x7FFFFF
NEG_INF = float("-inf")
VMEM_LIMIT = 48 * 1024 * 1024

ROW_TILE = 1024
LAYER1_TILE = 512
KEY_CHUNK = 256
B_TQ = 128
POS_SPLIT = 16
ONES_ROWS = 16
LOG2E = math.log2(math.e)
COEF_TERMS = 4
CONV_HALO = 32


def _alibi(n):
    return [float(2.0 ** (-8.0 * i / n)) for i in range(1, n + 1)]


def _params(sem):
    return pltpu.CompilerParams(dimension_semantics=sem, vmem_limit_bytes=VMEM_LIMIT)


def _nt_dot(a, b):
    return lax.dot_general(a, b, (((1,), (1,)), ((), ())), preferred_element_type=F32)


def _dot(a, b):
    return jnp.dot(a, b, preferred_element_type=F32)


def _rms(x, g):
    return x * lax.rsqrt(jnp.mean(x * x, axis=-1, keepdims=True) + EPS) * g


def _bf16_terms(x, n):
    terms = []
    for _ in range(n):
        t = float(np.asarray(x, np.float32).astype(jnp.bfloat16).astype(np.float32))
        terms.append(t)
        x -= t
    return terms


def _alibi_coef(rows, slope):
    lane = lax.broadcasted_iota(I32, (rows, LANES), 1)
    out = jnp.zeros((rows, LANES), F32)
    for i, t in enumerate(_bf16_terms(LOG2E * slope, COEF_TERMS)):
        out = jnp.where(lane == 2 * i, POS_SPLIT * t, jnp.where(lane == 2 * i + 1, t, out))
    return out.astype(BF16)


def _key_positions(seq):
    pos = np.zeros((seq, LANES), np.float32)
    for i in range(COEF_TERMS):
        pos[:, 2 * i] = np.arange(seq) // POS_SPLIT
        pos[:, 2 * i + 1] = np.arange(seq) % POS_SPLIT
    return jnp.asarray(pos, BF16)


def _ada_kernel(c_ref, w_ref, b_ref, o_ref):
    c = c_ref[...]
    sc = c * jax.nn.sigmoid(c)
    o_ref[0] = jnp.dot(sc.astype(BF16), w_ref[0].astype(BF16), preferred_element_type=F32) + b_ref[0]


def _ada_mods(c, ada_w, ada_b):
    depth, d, d3 = ada_w.shape
    bsz = c.shape[0]
    nt = d3 // d
    return pl.pallas_call(
        _ada_kernel,
        grid=(depth, nt),
        in_specs=[pl.BlockSpec((bsz, d), lambda l, j: (0, 0)),
                  pl.BlockSpec((1, d, d), lambda l, j: (l, 0, j)),
                  pl.BlockSpec((1, 1, d), lambda l, j: (l, 0, j))],
        out_specs=pl.BlockSpec((1, bsz, d), lambda l, j: (l, 0, j)),
        out_shape=jax.ShapeDtypeStruct((depth, bsz, d3), F32),
        compiler_params=_params(("arbitrary", "arbitrary")),
        name="ada_mods",
    )(c, ada_w, ada_b.reshape(depth, 1, d3))


def _proj0_kernel(x_ref, shift_ref, scale_ref, ng_ref, wmain_ref, wgate_ref, wkidx_ref, wwidx_ref,
                  qng_ref, kvng_ref,
                  qa_ref, ka_ref, vat_ref, cq_ref, ckv_ref, ckvt_ref, kidx_ref, widxt_ref, sg_ref):
    x = x_ref[0]
    h = _rms(x, ng_ref[...]) * (1.0 + scale_ref[0]) + shift_ref[0]
    hb = h.astype(BF16)
    main = jnp.dot(hb, wmain_ref[...], preferred_element_type=F32)
    qw = A_HEADS * 2 * A_DH
    qa_ref[0] = (main[:, :qw] * (A_DH ** -0.5 * LOG2E)).astype(BF16)
    ka_ref[0] = main[:, qw:2 * qw].astype(BF16)
    va = main[:, 2 * qw:3 * qw]
    cq_ref[0] = _rms(main[:, 3 * qw:3 * qw + B_DQLAT], qng_ref[...]).astype(BF16)
    ckv = _rms(main[:, 3 * qw + B_DQLAT:], kvng_ref[...])
    ckv_ref[0] = ckv.astype(BF16)
    ch = vat_ref.shape[3]
    ones = jnp.ones((ONES_ROWS, ch), F32)
    for c in range(vat_ref.shape[1]):
        vt = va[c * ch:(c + 1) * ch].T
        vat_ref[0, c] = jnp.concatenate(
            [blk for h in range(A_HEADS) for blk in (vt[h * A_DV:(h + 1) * A_DV], ones)], axis=0).astype(BF16)
        ckvt_ref[0, c] = jnp.concatenate([ckv[c * ch:(c + 1) * ch].T, ones], axis=0).astype(BF16)
    kidx_ref[0] = jnp.dot(hb, wkidx_ref[...], preferred_element_type=F32).astype(BF16)
    widxt = _nt_dot(wwidx_ref[...], hb)
    for j in range(widxt_ref.shape[1]):
        widxt_ref[0, j] = widxt[:, j * B_TQ:(j + 1) * B_TQ]
    gate = jnp.dot(hb, wgate_ref[...], preferred_element_type=F32)
    sg_ref[0] = (gate * jax.nn.sigmoid(gate)).astype(BF16)


def _proj0(x, shift, scale, norm_g, w_in, q_norm_g, kv_norm_g):
    bsz, seq, d = x.shape
    t = min(ROW_TILE, seq)
    ch = min(KEY_CHUNK, seq)
    qw = A_HEADS * 2 * A_DH
    o_main = 3 * qw + B_DQLAT + B_DLAT
    wmain = w_in[:, :o_main].astype(BF16)
    wkidx = jnp.tile(w_in[:, o_main:o_main + IDX_DH], (1, IDX_HEADS)).astype(BF16)
    wwidx = w_in[:, o_main + IDX_DH:o_main + IDX_DH + IDX_HEADS].T.astype(BF16)
    wgate = w_in[:, o_main + IDX_DH + IDX_HEADS:].astype(BF16)
    row = lambda w: pl.BlockSpec((1, t, w), lambda b, i: (b, i, 0))
    rows = lambda w: (jax.ShapeDtypeStruct((bsz, seq, w), BF16), row(w))
    chunked = lambda w: (jax.ShapeDtypeStruct((bsz, seq // ch, w, ch), BF16),
                         pl.BlockSpec((1, t // ch, w, ch), lambda b, i: (b, i, 0, 0)))
    mod = pl.BlockSpec((1, 1, d), lambda b, i: (b, 0, 0))
    full = lambda a: pl.BlockSpec(a.shape, lambda b, i: (0,) * a.ndim)
    ng = norm_g.reshape(1, d)
    qng = q_norm_g.reshape(1, B_DQLAT)
    kvng = kv_norm_g.reshape(1, B_DLAT)
    iw = IDX_HEADS * IDX_DH
    outs = [rows(qw), rows(qw), chunked(A_HEADS * (A_DV + ONES_ROWS)), rows(B_DQLAT), rows(B_DLAT),
            chunked(B_DLAT + ONES_ROWS), rows(iw),
            (jax.ShapeDtypeStruct((bsz, seq // B_TQ, IDX_HEADS, B_TQ), F32),
             pl.BlockSpec((1, t // B_TQ, IDX_HEADS, B_TQ), lambda b, i: (b, i, 0, 0))),
            rows(d)]
    return pl.pallas_call(
        _proj0_kernel,
        grid=(bsz, seq // t),
        in_specs=[row(d), mod, mod, full(ng), full(wmain), full(wgate), full(wkidx), full(wwidx),
                  full(qng), full(kvng)],
        out_specs=[o[1] for o in outs],
        out_shape=[o[0] for o in outs],
        compiler_params=_params(("arbitrary", "arbitrary")),
        name="proj0",
    )(x, shift, scale, ng, wmain, wgate, wkidx, wwidx, qng, kvng)


def _diff_attn_kernel(q_ref, k_ref, vt_ref, pos_ref, lq1_ref, lk1_ref, lq2_ref, lk2_ref, g_ref, o_ref,
                      *, lam_init):
    tk = vt_ref.shape[3]
    tq = tk
    lam = (jnp.exp(jnp.sum(lq1_ref[...] * lk1_ref[...], keepdims=True))
           - jnp.exp(jnp.sum(lq2_ref[...] * lk2_ref[...], keepdims=True)) + lam_init)
    krow = lax.broadcasted_iota(I32, (tk, 2 * tq), 0)
    qcol = lax.broadcasted_iota(I32, (tk, 2 * tq), 1) % tq
    causal = krow <= qcol
    slopes = _alibi(A_HEADS)
    heads = [slice(h * A_DV, (h + 1) * A_DV) for h in range(A_HEADS)]
    first = lax.broadcasted_iota(I32, (tq, A_DV), 1) < A_DH
    zero = jnp.zeros((tq, A_DV), BF16)
    rows_v = vt_ref.shape[2] // A_HEADS
    vrows = [slice(h * rows_v, (h + 1) * rows_v) for h in range(A_HEADS)]

    def variant(n, i):
        rows = n * tk
        pos = pos_ref[:rows, :]
        qrows = pl.ds(pl.multiple_of(i * tq, tq), tq)
        qs = [jnp.concatenate(
            [jnp.concatenate([jnp.where(first, q_ref[0, qrows, hs], zero), _alibi_coef(tq, slopes[h])], axis=1),
             jnp.concatenate([jnp.where(first, zero, q_ref[0, qrows, hs]), _alibi_coef(tq, slopes[h])], axis=1)],
            axis=0) for h, hs in enumerate(heads)]
        for h, hs in enumerate(heads):
            s = _nt_dot(jnp.concatenate([k_ref[0, :rows, hs], pos], axis=1), qs[h])
            last = jnp.where(causal, s[rows - tk:], NEG_INF)
            s = last if n == 1 else jnp.concatenate([s[:rows - tk], last], axis=0)
            m = jnp.max(s, axis=0, keepdims=True)
            vt = jnp.concatenate([vt_ref[0, c, vrows[h], :] for c in range(n)], axis=1)
            acc = _dot(vt, jnp.exp2(s - m).astype(BF16))
            o = acc[:A_DV] * (1.0 / acc[A_DV:A_DV + 1])
            od = (o[:, :tq] - lam * o[:, tq:]).T
            o_ref[0, qrows, hs] = (_rms(od, g_ref[...]) * (1.0 - lam_init)).astype(o_ref.dtype)

    def tile(i, carry):
        for n in range(1, vt_ref.shape[1] + 1):
            pl.when(i + 1 == n)(functools.partial(variant, n, i))
        return carry

    lax.fori_loop(0, vt_ref.shape[1], tile, 0)


def _diff_attn(qa, ka, vat, pos, lam_q1, lam_k1, lam_q2, lam_k2, subln_g, layer_idx):
    bsz, seq, w = qa.shape
    tq = vat.shape[3]
    lam_init = 0.8 - 0.6 * math.exp(-0.3 * layer_idx)
    vec = lambda a: a.reshape(1, -1)
    full = lambda a: pl.BlockSpec(a.shape, lambda b: (0,) * a.ndim)
    small = [vec(lam_q1), vec(lam_k1), vec(lam_q2), vec(lam_k2), vec(subln_g)]
    return pl.pallas_call(
        functools.partial(_diff_attn_kernel, lam_init=lam_init),
        grid=(bsz,),
        in_specs=[pl.BlockSpec((1, seq, w), lambda b: (b, 0, 0)),
                  pl.BlockSpec((1, seq, w), lambda b: (b, 0, 0)),
                  pl.BlockSpec((1,) + vat.shape[1:], lambda b: (b, 0, 0, 0)),
                  full(pos)] + [full(a) for a in small],
        out_specs=pl.BlockSpec((1, seq, w), lambda b: (b, 0, 0)),
        out_shape=jax.ShapeDtypeStruct((bsz, seq, w), BF16),
        compiler_params=_params(("arbitrary",)),
        name="diff_attn",
    )(qa, ka, vat, pos, *small)


def _dsa_kernel(cq_ref, wuq_ref, wqidx_ref, wt_ref, kv_ref, kvt_ref, kidx_ref, pos_ref, wuv_ref, o_ref,
                key_ref, thr_ref, lim_ref, *, topk):
    tiles, tq = wt_ref.shape[1], wt_ref.shape[3]
    kc_rows = kvt_ref.shape[3]
    slopes = _alibi(B_HEADS)
    group = 2
    ngroups = B_HEADS // group

    def query_operands(j):
        cq = cq_ref[0, pl.ds(pl.multiple_of(j * tq, tq), tq), :]
        qidx = _dot(cq, wqidx_ref[...])
        qlat = (_dot(cq, wuq_ref[...]) * (B_DLAT ** -0.5 * LOG2E)).astype(BF16)
        head_of_lane = lax.broadcasted_iota(I32, qidx.shape, 1) // IDX_DH
        qstack = jnp.concatenate([jnp.where(head_of_lane == h, qidx, 0.0) for h in range(IDX_HEADS)],
                                 axis=0).astype(BF16)
        qs = jnp.concatenate(
            [jnp.concatenate([qlat[:, h * B_DLAT:(h + 1) * B_DLAT], _alibi_coef(tq, slopes[h])], axis=1)
             for h in range(B_HEADS)], axis=0)
        return qstack, qs

    def to_float(u):
        s = jnp.maximum(u ^ INT_MIN, ORDER_OF_NEG_INF)
        return pltpu.bitcast(s ^ ((s >> 31) & 0x7FFFFFFF), F32)

    def variant(n, qi):
        rows = n * kc_rows
        qstack, qs = query_operands(qi)
        wt = wt_ref[0, qi]
        kpos = lax.broadcasted_iota(I32, (rows, tq), 0)
        causal = kpos <= qi * tq + lax.broadcasted_iota(I32, (rows, tq), 1)
        krow = kpos[:SUBLANES]

        rel = _nt_dot(kidx_ref[0, :rows, :], qstack)
        isc = jnp.zeros((rows, tq), F32)
        for h in range(IDX_HEADS):
            isc = isc + wt[h:h + 1, :] * jnp.maximum(rel[:, h * tq:(h + 1) * tq], 0.0)
        key_ref[:rows, :] = jnp.where(causal, isc, NEG_INF)

        def count(pred):
            accs = [jnp.zeros((SUBLANES, tq), I32) for _ in range(4)]
            for r in range(rows // SUBLANES):
                ind = pred(key_ref[r * SUBLANES:(r + 1) * SUBLANES, :], r * SUBLANES + krow)
                accs[r % 4] = accs[r % 4] + ind.astype(I32)
            return jnp.sum((accs[0] + accs[1]) + (accs[2] + accs[3]), axis=0, keepdims=True)

        lim_ref[...] = jnp.full((1, tq), 2 ** 30, I32)
        if rows > topk:
            def bit_step(t, state):
                cand, trial, thr = state
                nxt = lax.shift_right_logical(lax.shift_right_logical(jnp.int32(INT_MIN), t), 1)
                thr_hit, thr_miss = to_float(trial | nxt), to_float(cand | nxt)
                hit = count(lambda x, p: x >= thr) >= topk
                cand = jnp.where(hit, trial, cand)
                return cand, cand | nxt, jnp.where(hit, thr_hit, thr_miss)
            top = jnp.full((1, tq), INT_MIN, I32)
            cand = lax.fori_loop(0, 32, bit_step, (jnp.zeros((1, tq), I32), top, to_float(top)))[0]
            thr = to_float(cand)
            thr_ref[...] = thr
            need = topk - count(lambda x, p: x > thr)
            n_eq = count(lambda x, p: x == thr)

            @pl.when(jnp.max((n_eq > need).astype(I32)) > 0)
            def _():
                def pos_step(t, lim):
                    trial = lim | jnp.left_shift(jnp.int32(1), 11 - t)
                    cnt = count(lambda x, p: (x == thr) & (p < trial))
                    return jnp.where(cnt < need, trial, lim)
                lim_ref[...] = lax.fori_loop(0, 12, pos_step, jnp.zeros((1, tq), I32))

        if rows > topk:
            x = key_ref[:rows, :]
            thr = thr_ref[...]
            at_or_above = jnp.where(x >= thr, 0.0, NEG_INF)
            above = jnp.where(x > thr, 0.0, NEG_INF)
            bias = jnp.where(kpos <= lim_ref[...], at_or_above, above)
        else:
            bias = jnp.where(causal, 0.0, NEG_INF)
        bias = jnp.concatenate([bias] * group, axis=1)
        kaug = jnp.concatenate([kv_ref[0, :rows, :], pos_ref[:rows, :]], axis=1)
        kvt = jnp.concatenate([kvt_ref[0, c] for c in range(n)], axis=1)
        s_all = _nt_dot(kaug, qs)
        ps = []
        for g in range(ngroups):
            s = s_all[:, g * group * tq:(g + 1) * group * tq] + bias
            m = jnp.max(s, axis=0, keepdims=True)
            ps.append(jnp.exp2(s - m).astype(BF16))
        acc = _dot(kvt, jnp.concatenate(ps, axis=1))

        o = acc[:B_DLAT] * (1.0 / acc[B_DLAT:B_DLAT + 1])
        o_all = jnp.concatenate([o[:, h * tq:(h + 1) * tq].T for h in range(B_HEADS)], axis=1).astype(BF16)
        o_ref[0, pl.ds(pl.multiple_of(qi * tq, tq), tq), :] = _dot(o_all, wuv_ref[...]).astype(o_ref.dtype)

    def tile(qi, carry):
        nch = (qi * tq) // kc_rows + 1
        for n in range(1, key_ref.shape[0] // kc_rows + 1):
            pl.when(nch == n)(functools.partial(variant, n, qi))
        return carry

    lax.fori_loop(0, tiles, tile, 0)


def _dsa(cq, w_uq, w_qidx, widxt, ckv, ckvt, kidx, pos, w_uv):
    bsz, seq, _ = cq.shape
    wuq = w_uq.reshape(B_DQLAT, B_HEADS * B_DLAT).astype(BF16)
    wqidx = w_qidx.reshape(B_DQLAT, IDX_HEADS * IDX_DH).astype(BF16)
    tq = min(B_TQ, seq)
    topk = min(TOPK_MAX, seq // 4)
    eye = jnp.eye(B_HEADS, dtype=w_uv.dtype)
    wuv = jnp.einsum('hde,hg->hdge', w_uv, eye).reshape(B_HEADS * B_DLAT, B_HEADS * B_DV).astype(BF16)
    return pl.pallas_call(
        functools.partial(_dsa_kernel, topk=topk),
        grid=(bsz,),
        in_specs=[pl.BlockSpec((1, seq, B_DQLAT), lambda b: (b, 0, 0)),
                  pl.BlockSpec(wuq.shape, lambda b: (0, 0)),
                  pl.BlockSpec(wqidx.shape, lambda b: (0, 0)),
                  pl.BlockSpec((1,) + widxt.shape[1:], lambda b: (b, 0, 0, 0)),
                  pl.BlockSpec((1, seq, B_DLAT), lambda b: (b, 0, 0)),
                  pl.BlockSpec((1,) + ckvt.shape[1:], lambda b: (b, 0, 0, 0)),
                  pl.BlockSpec((1, seq, IDX_HEADS * IDX_DH), lambda b: (b, 0, 0)),
                  pl.BlockSpec(pos.shape, lambda b: (0, 0)),
                  pl.BlockSpec(wuv.shape, lambda b: (0, 0))],
        out_specs=pl.BlockSpec((1, seq, B_HEADS * B_DV), lambda b: (b, 0, 0)),
        out_shape=jax.ShapeDtypeStruct((bsz, seq, B_HEADS * B_DV), BF16),
        scratch_shapes=[pltpu.VMEM((seq, tq), F32),
                        pltpu.VMEM((1, tq), F32), pltpu.VMEM((1, tq), I32)],
        compiler_params=_params(("arbitrary",)),
        name="dsa",
    )(cq, wuq, wqidx, widxt, ckv, ckvt, kidx, pos, wuv)


def _swa_block(q, kp, kc, vtp, vtc, pos, sink_ref, mask):
    w = q.shape[0]
    rep = C_HEADS // C_KV_HEADS
    half = LANES // 2
    lane = lax.broadcasted_iota(I32, (w, LANES), 1)
    slopes = _alibi(C_HEADS)
    qs = []
    for h in range(C_HEADS):
        g = h // rep
        x = q[:, (h // 2) * LANES:(h // 2 + 1) * LANES].astype(F32)
        if (h % 2) != g:
            x = pltpu.roll(x, half, axis=1)
        x = jnp.where((lane // half) == g, x, 0.0).astype(BF16)
        qs.append(jnp.concatenate([x, _alibi_coef(w, slopes[h])], axis=1))
    kaug = jnp.concatenate([jnp.concatenate([kp, kc], axis=0), pos], axis=1)
    vt = jnp.concatenate([vtp, vtc], axis=1)
    s_all = _nt_dot(kaug, jnp.concatenate(qs, axis=0))
    ps, rs = [], []
    for h in range(C_HEADS):
        s = s_all[:, h * w:(h + 1) * w] + mask
        sink = LOG2E * (sink_ref[h:h + 1, :] + slopes[h] * (w + lane[:1, :]).astype(F32))
        m = jnp.maximum(jnp.max(s, axis=0, keepdims=True), sink)
        p = jnp.exp2(s - m)
        rs.append(1.0 / (jnp.sum(p, axis=0, keepdims=True) + jnp.exp2(sink - m)))
        ps.append(p.astype(BF16))
    o = _dot(vt, jnp.concatenate(ps, axis=1)) * jnp.concatenate(rs, axis=1)
    ot = [o[(h // rep) * C_DH:(h // rep + 1) * C_DH, h * w:(h + 1) * w] for h in range(C_HEADS)]
    return jnp.concatenate(ot, axis=0).T


def _layer1_kernel(oa_ref, ob_ref, sg_ref, x_ref, gmod0_ref, wout0_ref, shift_ref, scale_ref, ng_ref, win_ref,
                   pos_ref, sink_ref, dww_ref, dwb_ref, lng_ref, lnb_ref, gmod1_ref, wout1_ref, fg_ref,
                   o_ref, mask_ref, kprev_ref, vtprev_ref, buf_ref, shift_buf_ref):
    b = pl.program_id(0)
    i = pl.program_id(1)
    t = x_ref.shape[1]
    w = WINDOW
    halo = CONV_HALO

    @pl.when((b == 0) & (i == 0))
    def _():
        k = lax.broadcasted_iota(I32, (2 * w, w), 0)
        q = lax.broadcasted_iota(I32, (2 * w, w), 1)
        dist = w + q - k
        valid = (dist >= 0) & (dist < w)
        mask_ref[1] = jnp.where(valid, 0.0, NEG_INF)
        mask_ref[0] = jnp.where(valid & (k >= w), 0.0, NEG_INF)
        kprev_ref[...] = jnp.zeros(kprev_ref.shape, kprev_ref.dtype)
        vtprev_ref[...] = jnp.zeros(vtprev_ref.shape, vtprev_ref.dtype)
        buf_ref[:halo, :] = jnp.zeros((halo, buf_ref.shape[1]), F32)

    y = jnp.concatenate([oa_ref[0], ob_ref[0]], axis=1).astype(F32) * sg_ref[0].astype(F32)
    x1 = x_ref[0] + gmod0_ref[0] * _dot(y.astype(BF16), wout0_ref[...])
    h = _rms(x1, ng_ref[...]) * (1.0 + scale_ref[0]) + shift_ref[0]
    p = _dot(h.astype(BF16), win_ref[...])
    qw = C_HEADS * C_DH
    kw = C_KV_HEADS * C_DH
    qc = (p[:, :qw] * (C_DH ** -0.5 * LOG2E)).astype(BF16)
    kc = p[:, qw:qw + kw].astype(BF16)
    vct = p[:, qw + kw:qw + 2 * kw].T.astype(BF16)
    off = qw + 2 * kw
    hglu = p[:, off:off + D_CH] * jax.nn.sigmoid(p[:, off + D_CH:off + 2 * D_CH])
    gate = p[:, off + 2 * D_CH:]
    sg1 = gate * jax.nn.sigmoid(gate)

    oc = []
    for j in range(t // w):
        kp = kprev_ref[...] if j == 0 else kc[(j - 1) * w:j * w]
        vtp = vtprev_ref[...] if j == 0 else vct[:, (j - 1) * w:j * w]
        mask = mask_ref[jnp.minimum(i, 1)] if j == 0 else mask_ref[1]
        oc.append(_swa_block(qc[j * w:(j + 1) * w], kp, kc[j * w:(j + 1) * w], vtp, vct[:, j * w:(j + 1) * w],
                             pos_ref[...], sink_ref, mask))
    kprev_ref[...] = kc[t - w:]
    vtprev_ref[...] = vct[:, t - w:]

    buf_ref[:halo, :] = jnp.where(i > 0, buf_ref[:halo, :], 0.0)
    buf_ref[halo:, :] = hglu
    span = shift_buf_ref.shape[1]
    for ph in range(1, SUBLANES):
        shift_buf_ref[ph - 1] = buf_ref[ph:ph + span, :]
    acc = jnp.zeros((t, D_CH), F32) + dwb_ref[...]
    first = halo - (CONV_W - 1)
    for j in range(CONV_W):
        base, ph = (first + j) // SUBLANES * SUBLANES, (first + j) % SUBLANES
        rows = buf_ref[base:base + t, :] if ph == 0 else shift_buf_ref[ph - 1, base:base + t, :]
        acc = acc + dww_ref[j:j + 1, :] * rows
    buf_ref[:halo, :] = hglu[t - halo:]
    mu = jnp.mean(acc, axis=-1, keepdims=True)
    xc = acc - mu
    var = jnp.mean(xc * xc, axis=-1, keepdims=True)
    yn = xc * lax.rsqrt(var + EPS) * lng_ref[...] + lnb_ref[...]
    od = yn * jax.nn.sigmoid(yn)

    y1 = jnp.concatenate([jnp.concatenate(oc, axis=0), od], axis=1) * sg1
    x2 = x1 + gmod1_ref[0] * _dot(y1.astype(BF16), wout1_ref[...])
    o_ref[0] = _rms(x2, fg_ref[...])


def _layer1(oa, ob, sg, x, gmod0, w_out0, shift, scale, norm_g, w_in, pos, sinks, dw_w, dw_b, ln_g, ln_b,
            gmod1, w_out1, final_g):
    bsz, seq, d = x.shape
    t = min(LAYER1_TILE, seq)
    w = WINDOW
    kw = C_KV_HEADS * C_DH
    wout0 = w_out0.astype(BF16)
    win = w_in.astype(BF16)
    wout1 = w_out1.astype(BF16)
    vec = lambda a: a.reshape(1, -1)
    sink_rows = jnp.broadcast_to(sinks.astype(F32)[:, None], (C_HEADS, w))
    row = lambda width: pl.BlockSpec((1, t, width), lambda b, i: (b, i, 0))
    mod = pl.BlockSpec((1, 1, d), lambda b, i: (b, 0, 0))
    full = lambda a: pl.BlockSpec(a.shape, lambda b, i: (0,) * a.ndim)
    consts = [vec(norm_g), win]
    tail = [sink_rows, dw_w, vec(dw_b), vec(ln_g), vec(ln_b)]
    return pl.pallas_call(
        _layer1_kernel,
        grid=(bsz, seq // t),
        in_specs=[row(oa.shape[2]), row(ob.shape[2]), row(d), row(d), mod, full(wout0), mod, mod]
                 + [full(a) for a in consts]
                 + [pl.BlockSpec((2 * w, LANES), lambda b, i: (0, 0))]
                 + [full(a) for a in tail]
                 + [mod, full(wout1), full(vec(final_g))],
        out_specs=row(d),
        out_shape=jax.ShapeDtypeStruct((bsz, seq, d), F32),
        scratch_shapes=[pltpu.VMEM((2, 2 * w, w), F32),
                        pltpu.VMEM((w, kw), BF16), pltpu.VMEM((kw, w), BF16),
                        pltpu.VMEM((CONV_HALO + t, D_CH), F32),
                        pltpu.VMEM((SUBLANES - 1, CONV_HALO - SUBLANES + t, D_CH), F32)],
        compiler_params=_params(("arbitrary", "arbitrary")),
        name="layer1",
    )(oa, ob, sg, x, gmod0, wout0, shift, scale, *consts, pos, *tail, gmod1, wout1, vec(final_g))


def kernel(x, c, ada_w, ada_b, norm_g, ab_w_in, ab_q_norm_g, ab_kv_norm_g, ab_w_uq, ab_w_qidx, ab_w_uv,
           ab_lam_q1, ab_lam_k1, ab_lam_q2, ab_lam_k2, ab_subln_g, ab_w_out,
           cd_w_in, cd_sinks, cd_dw_w, cd_dw_b, cd_ln_g, cd_ln_b, cd_w_out, final_g):
    bsz, seq, d = x.shape
    mods = _ada_mods(c, ada_w, ada_b)
    mod = lambda l, k: mods[l, :, k * d:(k + 1) * d].reshape(bsz, 1, d)
    pos = _key_positions(seq)
    qa, ka, vat, cq, ckv, ckvt, kidx, widxt, sg0 = _proj0(
        x, mod(0, 0), mod(0, 1), norm_g[0], ab_w_in[0], ab_q_norm_g[0], ab_kv_norm_g[0])
    oa = _diff_attn(qa, ka, vat, pos, ab_lam_q1[0], ab_lam_k1[0], ab_lam_q2[0], ab_lam_k2[0], ab_subln_g[0], 0)
    ob = _dsa(cq, ab_w_uq[0], ab_w_qidx[0], widxt, ckv, ckvt, kidx, pos, ab_w_uv[0])
    return _layer1(oa, ob, sg0, x, mod(0, 2), ab_w_out[0], mod(1, 0), mod(1, 1), norm_g[1], cd_w_in[0], pos,
                   cd_sinks[0], cd_dw_w[0], cd_dw_b[0], cd_ln_g[0], cd_ln_b[0], mod(1, 2), cd_w_out[0], final_g)
```

```python
import functools
import math

import numpy as np
import jax
import jax.numpy as jnp
from jax import lax
from jax.experimental import pallas as pl
from jax.experimental.pallas import tpu as pltpu

F32 = jnp.float32
BF16 = jnp.bfloat16
I32 = jnp.int32

EPS = 1e-6
A_HEADS = 4
A_DH = 64
A_DV = 128
B_HEADS = 8
B_DQLAT = 128
B_DLAT = 128
B_DV = 64
IDX_HEADS = 8
IDX_DH = 32
TOPK_MAX = 256
C_HEADS = 8
C_KV_HEADS = 2
C_DH = 64
WINDOW = 128
D_CH = 512
CONV_W = 31

LANES = 128
SUBLANES = 8
INT_MIN = -2 ** 31
ORDER_OF_NEG_INF = INT_MIN + 0x7FFFFF
NEG_INF = float("-inf")
VMEM_LIMIT = 48 * 1024 * 1024

ROW_TILE = 1024
LAYER1_TILE = 512
KEY_CHUNK = 256
B_TQ = 128
POS_SPLIT = 16
ONES_ROWS = 16
LOG2E = math.log2(math.e)
COEF_TERMS = 4
CONV_HALO = 32


def _alibi(n):
    return [float(2.0 ** (-8.0 * i / n)) for i in range(1, n + 1)]


def _params(sem, fusible=None):
    return pltpu.CompilerParams(dimension_semantics=sem, vmem_limit_bytes=VMEM_LIMIT, allow_input_fusion=fusible)


def _nt_dot(a, b):
    return lax.dot_general(a, b, (((1,), (1,)), ((), ())), preferred_element_type=F32)


def _dot(a, b):
    return jnp.dot(a, b, preferred_element_type=F32)


def _rms(x, g):
    return x * lax.rsqrt(jnp.mean(x * x, axis=-1, keepdims=True) + EPS) * g


def _bf16_terms(x, n):
    terms = []
    for _ in range(n):
        t = float(np.asarray(x, np.float32).astype(jnp.bfloat16).astype(np.float32))
        terms.append(t)
        x -= t
    return terms


def _alibi_coef(rows, slope):
    lane = lax.broadcasted_iota(I32, (rows, LANES), 1)
    out = jnp.zeros((rows, LANES), F32)
    for i, t in enumerate(_bf16_terms(LOG2E * slope, COEF_TERMS)):
        out = jnp.where(lane == 2 * i, POS_SPLIT * t, jnp.where(lane == 2 * i + 1, t, out))
    return out.astype(BF16)


def _key_positions(seq):
    pos = np.zeros((seq, LANES), np.float32)
    for i in range(COEF_TERMS):
        pos[:, 2 * i] = np.arange(seq) // POS_SPLIT
        pos[:, 2 * i + 1] = np.arange(seq) % POS_SPLIT
    return jnp.asarray(pos, BF16)


def _ada_kernel(c_ref, w_ref, b_ref, o_ref):
    c = c_ref[...]
    sc = c * jax.nn.sigmoid(c)
    o_ref[0] = jnp.dot(sc.astype(BF16), w_ref[0].astype(BF16), preferred_element_type=F32) + b_ref[0]


def _ada_mods(c, ada_w, ada_b):
    depth, d, d3 = ada_w.shape
    bsz = c.shape[0]
    nt = d3 // d
    return pl.pallas_call(
        _ada_kernel,
        grid=(depth, nt),
        in_specs=[pl.BlockSpec((bsz, d), lambda l, j: (0, 0)),
                  pl.BlockSpec((1, d, d), lambda l, j: (l, 0, j)),
                  pl.BlockSpec((1, 1, d), lambda l, j: (l, 0, j))],
        out_specs=pl.BlockSpec((1, bsz, d), lambda l, j: (l, 0, j)),
        out_shape=jax.ShapeDtypeStruct((depth, bsz, d3), F32),
        compiler_params=_params(("arbitrary", "arbitrary")),
        name="ada_mods",
    )(c, ada_w, ada_b.reshape(depth, 1, d3))


def _proj0_kernel(x_ref, shift_ref, scale_ref, ng_ref, wmain_ref, wgate_ref, wkidx_ref, wwidx_ref,
                  qng_ref, kvng_ref,
                  qa_ref, ka_ref, vat_ref, cq_ref, ckv_ref, ckvt_ref, kidx_ref, widxt_ref, sg_ref):
    x = x_ref[0]
    h = _rms(x, ng_ref[...]) * (1.0 + scale_ref[0]) + shift_ref[0]
    hb = h.astype(BF16)
    main = jnp.dot(hb, wmain_ref[...], preferred_element_type=F32)
    qw = A_HEADS * 2 * A_DH
    qa_ref[0] = (main[:, :qw] * (A_DH ** -0.5 * LOG2E)).astype(BF16)
    ka_ref[0] = main[:, qw:2 * qw].astype(BF16)
    va = main[:, 2 * qw:3 * qw]
    cq_ref[0] = _rms(main[:, 3 * qw:3 * qw + B_DQLAT], qng_ref[...]).astype(BF16)
    ckv = _rms(main[:, 3 * qw + B_DQLAT:], kvng_ref[...])
    ckv_ref[0] = ckv.astype(BF16)
    ch = vat_ref.shape[3]
    ones = jnp.ones((ONES_ROWS, ch), F32)
    for c in range(vat_ref.shape[1]):
        vt = va[c * ch:(c + 1) * ch].T
        vat_ref[0, c] = jnp.concatenate(
            [blk for h in range(A_HEADS) for blk in (vt[h * A_DV:(h + 1) * A_DV], ones)], axis=0).astype(BF16)
        ckvt_ref[0, c] = jnp.concatenate([ckv[c * ch:(c + 1) * ch].T, ones], axis=0).astype(BF16)
    kidx_ref[0] = jnp.dot(hb, wkidx_ref[...], preferred_element_type=F32).astype(BF16)
    widxt = _nt_dot(wwidx_ref[...], hb)
    for j in range(widxt_ref.shape[1]):
        widxt_ref[0, j] = widxt[:, j * B_TQ:(j + 1) * B_TQ]
    gate = jnp.dot(hb, wgate_ref[...], preferred_element_type=F32)
    sg_ref[0] = (gate * jax.nn.sigmoid(gate)).astype(BF16)


def _proj0(x, shift, scale, norm_g, w_in, q_norm_g, kv_norm_g):
    bsz, seq, d = x.shape
    t = min(ROW_TILE, seq)
    ch = min(KEY_CHUNK, seq)
    qw = A_HEADS * 2 * A_DH
    o_main = 3 * qw + B_DQLAT + B_DLAT
    wmain = w_in[:, :o_main].astype(BF16)
    wkidx = jnp.tile(w_in[:, o_main:o_main + IDX_DH], (1, IDX_HEADS)).astype(BF16)
    wwidx = w_in[:, o_main + IDX_DH:o_main + IDX_DH + IDX_HEADS].T.astype(BF16)
    wgate = w_in[:, o_main + IDX_DH + IDX_HEADS:].astype(BF16)
    row = lambda w: pl.BlockSpec((1, t, w), lambda b, i: (b, i, 0))
    rows = lambda w: (jax.ShapeDtypeStruct((bsz, seq, w), BF16), row(w))
    chunked = lambda w: (jax.ShapeDtypeStruct((bsz, seq // ch, w, ch), BF16),
                         pl.BlockSpec((1, t // ch, w, ch), lambda b, i: (b, i, 0, 0)))
    mod = pl.BlockSpec((1, 1, d), lambda b, i: (b, 0, 0))
    full = lambda a: pl.BlockSpec(a.shape, lambda b, i: (0,) * a.ndim)
    ng = norm_g.reshape(1, d)
    qng = q_norm_g.reshape(1, B_DQLAT)
    kvng = kv_norm_g.reshape(1, B_DLAT)
    iw = IDX_HEADS * IDX_DH
    outs = [rows(qw), rows(qw), chunked(A_HEADS * (A_DV + ONES_ROWS)), rows(B_DQLAT), rows(B_DLAT),
            chunked(B_DLAT + ONES_ROWS), rows(iw),
            (jax.ShapeDtypeStruct((bsz, seq // B_TQ, IDX_HEADS, B_TQ), F32),
             pl.BlockSpec((1, t // B_TQ, IDX_HEADS, B_TQ), lambda b, i: (b, i, 0, 0))),
            rows(d)]
    return pl.pallas_call(
        _proj0_kernel,
        grid=(bsz, seq // t),
        in_specs=[row(d), mod, mod, full(ng), full(wmain), full(wgate), full(wkidx), full(wwidx),
                  full(qng), full(kvng)],
        out_specs=[o[1] for o in outs],
        out_shape=[o[0] for o in outs],
        compiler_params=_params(("arbitrary", "arbitrary"), [False] * 4 + [True] * 4 + [False] * 2),
        name="proj0",
    )(x, shift, scale, ng, wmain, wgate, wkidx, wwidx, qng, kvng)


def _diff_attn_kernel(q_ref, k_ref, vt_ref, pos_ref, lq1_ref, lk1_ref, lq2_ref, lk2_ref, g_ref, o_ref,
                      *, lam_init):
    tk = vt_ref.shape[3]
    tq = tk
    lam = (jnp.exp(jnp.sum(lq1_ref[...] * lk1_ref[...], keepdims=True))
           - jnp.exp(jnp.sum(lq2_ref[...] * lk2_ref[...], keepdims=True)) + lam_init)
    krow = lax.broadcasted_iota(I32, (tk, 2 * tq), 0)
    qcol = lax.broadcasted_iota(I32, (tk, 2 * tq), 1) % tq
    causal = krow <= qcol
    slopes = _alibi(A_HEADS)
    heads = [slice(h * A_DV, (h + 1) * A_DV) for h in range(A_HEADS)]
    first = lax.broadcasted_iota(I32, (tq, A_DV), 1) < A_DH
    zero = jnp.zeros((tq, A_DV), BF16)
    rows_v = vt_ref.shape[2] // A_HEADS
    vrows = [slice(h * rows_v, (h + 1) * rows_v) for h in range(A_HEADS)]

    def variant(n, i):
        rows = n * tk
        pos = pos_ref[:rows, :]
        qrows = pl.ds(pl.multiple_of(i * tq, tq), tq)
        qs = [jnp.concatenate(
            [jnp.concatenate([jnp.where(first, q_ref[0, qrows, hs], zero), _alibi_coef(tq, slopes[h])], axis=1),
             jnp.concatenate([jnp.where(first, zero, q_ref[0, qrows, hs]), _alibi_coef(tq, slopes[h])], axis=1)],
            axis=0) for h, hs in enumerate(heads)]
        for h, hs in enumerate(heads):
            s = _nt_dot(jnp.concatenate([k_ref[0, :rows, hs], pos], axis=1), qs[h])
            last = jnp.where(causal, s[rows - tk:], NEG_INF)
            s = last if n == 1 else jnp.concatenate([s[:rows - tk], last], axis=0)
            m = jnp.max(s, axis=0, keepdims=True)
            vt = jnp.concatenate([vt_ref[0, c, vrows[h], :] for c in range(n)], axis=1)
            acc = _dot(vt, jnp.exp2(s - m).astype(BF16))
            o = acc[:A_DV] * (1.0 / acc[A_DV:A_DV + 1])
            od = (o[:, :tq] - lam * o[:, tq:]).T
            o_ref[0, qrows, hs] = (_rms(od, g_ref[...]) * (1.0 - lam_init)).astype(o_ref.dtype)

    def tile(i, carry):
        for n in range(1, vt_ref.shape[1] + 1):
            pl.when(i + 1 == n)(functools.partial(variant, n, i))
        return carry

    lax.fori_loop(0, vt_ref.shape[1], tile, 0)


def _diff_attn(qa, ka, vat, pos, lam_q1, lam_k1, lam_q2, lam_k2, subln_g, layer_idx):
    bsz, seq, w = qa.shape
    tq = vat.shape[3]
    lam_init = 0.8 - 0.6 * math.exp(-0.3 * layer_idx)
    vec = lambda a: a.reshape(1, -1)
    full = lambda a: pl.BlockSpec(a.shape, lambda b: (0,) * a.ndim)
    small = [vec(lam_q1), vec(lam_k1), vec(lam_q2), vec(lam_k2), vec(subln_g)]
    return pl.pallas_call(
        functools.partial(_diff_attn_kernel, lam_init=lam_init),
        grid=(bsz,),
        in_specs=[pl.BlockSpec((1, seq, w), lambda b: (b, 0, 0)),
                  pl.BlockSpec((1, seq, w), lambda b: (b, 0, 0)),
                  pl.BlockSpec((1,) + vat.shape[1:], lambda b: (b, 0, 0, 0)),
                  full(pos)] + [full(a) for a in small],
        out_specs=pl.BlockSpec((1, seq, w), lambda b: (b, 0, 0)),
        out_shape=jax.ShapeDtypeStruct((bsz, seq, w), BF16),
        compiler_params=_params(("arbitrary",)),
        name="diff_attn",
    )(qa, ka, vat, pos, *small)


def _dsa_kernel(cq_ref, wuq_ref, wqidx_ref, wt_ref, kv_ref, kvt_ref, kidx_ref, pos_ref, wuv_ref, o_ref,
                key_ref, thr_ref, lim_ref, *, topk):
    tiles, tq = wt_ref.shape[1], wt_ref.shape[3]
    kc_rows = kvt_ref.shape[3]
    slopes = _alibi(B_HEADS)
    group = 2
    ngroups = B_HEADS // group

    def query_operands(j):
        cq = cq_ref[0, pl.ds(pl.multiple_of(j * tq, tq), tq), :]
        qidx = _dot(cq, wqidx_ref[...])
        qlat = (_dot(cq, wuq_ref[...]) * (B_DLAT ** -0.5 * LOG2E)).astype(BF16)
        head_of_lane = lax.broadcasted_iota(I32, qidx.shape, 1) // IDX_DH
        qstack = jnp.concatenate([jnp.where(head_of_lane == h, qidx, 0.0) for h in range(IDX_HEADS)],
                                 axis=0).astype(BF16)
        qs = jnp.concatenate(
            [jnp.concatenate([qlat[:, h * B_DLAT:(h + 1) * B_DLAT], _alibi_coef(tq, slopes[h])], axis=1)
             for h in range(B_HEADS)], axis=0)
        return qstack, qs

    def to_float(u):
        s = jnp.maximum(u ^ INT_MIN, ORDER_OF_NEG_INF)
        return pltpu.bitcast(s ^ ((s >> 31) & 0x7FFFFFFF), F32)

    def variant(n, qi):
        rows = n * kc_rows
        qstack, qs = query_operands(qi)
        wt = wt_ref[0, qi]
        kpos = lax.broadcasted_iota(I32, (rows, tq), 0)
        causal = kpos <= qi * tq + lax.broadcasted_iota(I32, (rows, tq), 1)
        krow = kpos[:SUBLANES]

        rel = _nt_dot(kidx_ref[0, :rows, :], qstack)
        isc = jnp.zeros((rows, tq), F32)
        for h in range(IDX_HEADS):
            isc = isc + wt[h:h + 1, :] * jnp.maximum(rel[:, h * tq:(h + 1) * tq], 0.0)
        key_ref[:rows, :] = jnp.where(causal, isc, NEG_INF)

        def count(pred):
            accs = [jnp.zeros((SUBLANES, tq), I32) for _ in range(4)]
            for r in range(rows // SUBLANES):
                ind = pred(key_ref[r * SUBLANES:(r + 1) * SUBLANES, :], r * SUBLANES + krow)
                accs[r % 4] = accs[r % 4] + ind.astype(I32)
            return jnp.sum((accs[0] + accs[1]) + (accs[2] + accs[3]), axis=0, keepdims=True)

        lim_ref[...] = jnp.full((1, tq), 2 ** 30, I32)
        if rows > topk:
            def bit_step(t, state):
                cand, trial, thr = state
                nxt = lax.shift_right_logical(lax.shift_right_logical(jnp.int32(INT_MIN), t), 1)
                thr_hit, thr_miss = to_float(trial | nxt), to_float(cand | nxt)
                hit = count(lambda x, p: x >= thr) >= topk
                cand = jnp.where(hit, trial, cand)
                return cand, cand | nxt, jnp.where(hit, thr_hit, thr_miss)
            top = jnp.full((1, tq), INT_MIN, I32)
            cand = lax.fori_loop(0, 32, bit_step, (jnp.zeros((1, tq), I32), top, to_float(top)))[0]
            thr = to_float(cand)
            thr_ref[...] = thr
            need = topk - count(lambda x, p: x > thr)
            n_eq = count(lambda x, p: x == thr)

            @pl.when(jnp.max((n_eq > need).astype(I32)) > 0)
            def _():
                def pos_step(t, lim):
                    trial = lim | jnp.left_shift(jnp.int32(1), 11 - t)
                    cnt = count(lambda x, p: (x == thr) & (p < trial))
                    return jnp.where(cnt < need, trial, lim)
                lim_ref[...] = lax.fori_loop(0, 12, pos_step, jnp.zeros((1, tq), I32))

        if rows > topk:
            x = key_ref[:rows, :]
            thr = thr_ref[...]
            at_or_above = jnp.where(x >= thr, 0.0, NEG_INF)
            above = jnp.where(x > thr, 0.0, NEG_INF)
            bias = jnp.where(kpos <= lim_ref[...], at_or_above, above)
        else:
            bias = jnp.where(causal, 0.0, NEG_INF)
        bias = jnp.concatenate([bias] * group, axis=1)
        kaug = jnp.concatenate([kv_ref[0, :rows, :], pos_ref[:rows, :]], axis=1)
        kvt = jnp.concatenate([kvt_ref[0, c] for c in range(n)], axis=1)
        s_all = _nt_dot(kaug, qs)
        ps = []
        for g in range(ngroups):
            s = s_all[:, g * group * tq:(g + 1) * group * tq] + bias
            m = jnp.max(s, axis=0, keepdims=True)
            ps.append(jnp.exp2(s - m).astype(BF16))
        acc = _dot(kvt, jnp.concatenate(ps, axis=1))

        o = acc[:B_DLAT] * (1.0 / acc[B_DLAT:B_DLAT + 1])
        o_all = jnp.concatenate([o[:, h * tq:(h + 1) * tq].T for h in range(B_HEADS)], axis=1).astype(BF16)
        o_ref[0, pl.ds(pl.multiple_of(qi * tq, tq), tq), :] = _dot(o_all, wuv_ref[...]).astype(o_ref.dtype)

    def tile(qi, carry):
        nch = (qi * tq) // kc_rows + 1
        for n in range(1, key_ref.shape[0] // kc_rows + 1):
            pl.when(nch == n)(functools.partial(variant, n, qi))
        return carry

    lax.fori_loop(0, tiles, tile, 0)


def _dsa(cq, w_uq, w_qidx, widxt, ckv, ckvt, kidx, pos, w_uv):
    bsz, seq, _ = cq.shape
    wuq = w_uq.reshape(B_DQLAT, B_HEADS * B_DLAT).astype(BF16)
    wqidx = w_qidx.reshape(B_DQLAT, IDX_HEADS * IDX_DH).astype(BF16)
    tq = min(B_TQ, seq)
    topk = min(TOPK_MAX, seq // 4)
    eye = jnp.eye(B_HEADS, dtype=w_uv.dtype)
    wuv = jnp.einsum('hde,hg->hdge', w_uv, eye).reshape(B_HEADS * B_DLAT, B_HEADS * B_DV).astype(BF16)
    return pl.pallas_call(
        functools.partial(_dsa_kernel, topk=topk),
        grid=(bsz,),
        in_specs=[pl.BlockSpec((1, seq, B_DQLAT), lambda b: (b, 0, 0)),
                  pl.BlockSpec(wuq.shape, lambda b: (0, 0)),
                  pl.BlockSpec(wqidx.shape, lambda b: (0, 0)),
                  pl.BlockSpec((1,) + widxt.shape[1:], lambda b: (b, 0, 0, 0)),
                  pl.BlockSpec((1, seq, B_DLAT), lambda b: (b, 0, 0)),
                  pl.BlockSpec((1,) + ckvt.shape[1:], lambda b: (b, 0, 0, 0)),
                  pl.BlockSpec((1, seq, IDX_HEADS * IDX_DH), lambda b: (b, 0, 0)),
                  pl.BlockSpec(pos.shape, lambda b: (0, 0)),
                  pl.BlockSpec(wuv.shape, lambda b: (0, 0))],
        out_specs=pl.BlockSpec((1, seq, B_HEADS * B_DV), lambda b: (b, 0, 0)),
        out_shape=jax.ShapeDtypeStruct((bsz, seq, B_HEADS * B_DV), BF16),
        scratch_shapes=[pltpu.VMEM((seq, tq), F32),
                        pltpu.VMEM((1, tq), F32), pltpu.VMEM((1, tq), I32)],
        compiler_params=_params(("arbitrary",), [False, True, True] + [False] * 5 + [True]),
        name="dsa",
    )(cq, wuq, wqidx, widxt, ckv, ckvt, kidx, pos, wuv)


def _swa_block(q, kp, kc, vtp, vtc, pos, sink_ref, mask):
    w = q.shape[0]
    rep = C_HEADS // C_KV_HEADS
    half = LANES // 2
    lane = lax.broadcasted_iota(I32, (w, LANES), 1)
    slopes = _alibi(C_HEADS)
    qs = []
    for h in range(C_HEADS):
        g = h // rep
        x = q[:, (h // 2) * LANES:(h // 2 + 1) * LANES].astype(F32)
        if (h % 2) != g:
            x = pltpu.roll(x, half, axis=1)
        x = jnp.where((lane // half) == g, x, 0.0).astype(BF16)
        qs.append(jnp.concatenate([x, _alibi_coef(w, slopes[h])], axis=1))
    kaug = jnp.concatenate([jnp.concatenate([kp, kc], axis=0), pos], axis=1)
    vt = jnp.concatenate([vtp, vtc], axis=1)
    s_all = _nt_dot(kaug, jnp.concatenate(qs, axis=0))
    ps, rs = [], []
    for h in range(C_HEADS):
        s = s_all[:, h * w:(h + 1) * w] + mask
        sink = LOG2E * (sink_ref[h:h + 1, :] + slopes[h] * (w + lane[:1, :]).astype(F32))
        m = jnp.maximum(jnp.max(s, axis=0, keepdims=True), sink)
        p = jnp.exp2(s - m)
        rs.append(1.0 / (jnp.sum(p, axis=0, keepdims=True) + jnp.exp2(sink - m)))
        ps.append(p.astype(BF16))
    o = _dot(vt, jnp.concatenate(ps, axis=1)) * jnp.concatenate(rs, axis=1)
    ot = [o[(h // rep) * C_DH:(h // rep + 1) * C_DH, h * w:(h + 1) * w] for h in range(C_HEADS)]
    return jnp.concatenate(ot, axis=0).T


def _layer1_kernel(oa_ref, ob_ref, sg_ref, x_ref, gmod0_ref, wout0_ref, shift_ref, scale_ref, ng_ref, win_ref,
                   pos_ref, sink_ref, dww_ref, dwb_ref, lng_ref, lnb_ref, gmod1_ref, wout1_ref, fg_ref,
                   o_ref, mask_ref, kprev_ref, vtprev_ref, buf_ref, shift_buf_ref):
    b = pl.program_id(0)
    i = pl.program_id(1)
    t = x_ref.shape[1]
    w = WINDOW
    halo = CONV_HALO

    @pl.when((b == 0) & (i == 0))
    def _():
        k = lax.broadcasted_iota(I32, (2 * w, w), 0)
        q = lax.broadcasted_iota(I32, (2 * w, w), 1)
        dist = w + q - k
        valid = (dist >= 0) & (dist < w)
        mask_ref[1] = jnp.where(valid, 0.0, NEG_INF)
        mask_ref[0] = jnp.where(valid & (k >= w), 0.0, NEG_INF)
        kprev_ref[...] = jnp.zeros(kprev_ref.shape, kprev_ref.dtype)
        vtprev_ref[...] = jnp.zeros(vtprev_ref.shape, vtprev_ref.dtype)
        buf_ref[:halo, :] = jnp.zeros((halo, buf_ref.shape[1]), F32)

    y = jnp.concatenate([oa_ref[0], ob_ref[0]], axis=1).astype(F32) * sg_ref[0].astype(F32)
    x1 = x_ref[0] + gmod0_ref[0] * _dot(y.astype(BF16), wout0_ref[...])
    h = _rms(x1, ng_ref[...]) * (1.0 + scale_ref[0]) + shift_ref[0]
    p = _dot(h.astype(BF16), win_ref[...])
    qw = C_HEADS * C_DH
    kw = C_KV_HEADS * C_DH
    qc = (p[:, :qw] * (C_DH ** -0.5 * LOG2E)).astype(BF16)
    kc = p[:, qw:qw + kw].astype(BF16)
    vct = p[:, qw + kw:qw + 2 * kw].T.astype(BF16)
    off = qw + 2 * kw
    hglu = p[:, off:off + D_CH] * jax.nn.sigmoid(p[:, off + D_CH:off + 2 * D_CH])
    gate = p[:, off + 2 * D_CH:]
    sg1 = gate * jax.nn.sigmoid(gate)

    oc = []
    for j in range(t // w):
        kp = kprev_ref[...] if j == 0 else kc[(j - 1) * w:j * w]
        vtp = vtprev_ref[...] if j == 0 else vct[:, (j - 1) * w:j * w]
        mask = mask_ref[jnp.minimum(i, 1)] if j == 0 else mask_ref[1]
        oc.append(_swa_block(qc[j * w:(j + 1) * w], kp, kc[j * w:(j + 1) * w], vtp, vct[:, j * w:(j + 1) * w],
                             pos_ref[...], sink_ref, mask))
    kprev_ref[...] = kc[t - w:]
    vtprev_ref[...] = vct[:, t - w:]

    buf_ref[:halo, :] = jnp.where(i > 0, buf_ref[:halo, :], 0.0)
    buf_ref[halo:, :] = hglu
    span = shift_buf_ref.shape[1]
    for ph in range(1, SUBLANES):
        shift_buf_ref[ph - 1] = buf_ref[ph:ph + span, :]
    acc = jnp.zeros((t, D_CH), F32) + dwb_ref[...]
    first = halo - (CONV_W - 1)
    for j in range(CONV_W):
        base, ph = (first + j) // SUBLANES * SUBLANES, (first + j) % SUBLANES
        rows = buf_ref[base:base + t, :] if ph == 0 else shift_buf_ref[ph - 1, base:base + t, :]
        acc = acc + dww_ref[j:j + 1, :] * rows
    buf_ref[:halo, :] = hglu[t - halo:]
    mu = jnp.mean(acc, axis=-1, keepdims=True)
    xc = acc - mu
    var = jnp.mean(xc * xc, axis=-1, keepdims=True)
    yn = xc * lax.rsqrt(var + EPS) * lng_ref[...] + lnb_ref[...]
    od = yn * jax.nn.sigmoid(yn)

    y1 = jnp.concatenate([jnp.concatenate(oc, axis=0), od], axis=1) * sg1
    x2 = x1 + gmod1_ref[0] * _dot(y1.astype(BF16), wout1_ref[...])
    o_ref[0] = _rms(x2, fg_ref[...])


def _layer1(oa, ob, sg, x, gmod0, w_out0, shift, scale, norm_g, w_in, pos, sinks, dw_w, dw_b, ln_g, ln_b,
            gmod1, w_out1, final_g):
    bsz, seq, d = x.shape
    t = min(LAYER1_TILE, seq)
    w = WINDOW
    kw = C_KV_HEADS * C_DH
    wout0 = w_out0.astype(BF16)
    win = w_in.astype(BF16)
    wout1 = w_out1.astype(BF16)
    vec = lambda a: a.reshape(1, -1)
    sink_rows = jnp.broadcast_to(sinks.astype(F32)[:, None], (C_HEADS, w))
    row = lambda width: pl.BlockSpec((1, t, width), lambda b, i: (b, i, 0))
    mod = pl.BlockSpec((1, 1, d), lambda b, i: (b, 0, 0))
    full = lambda a: pl.BlockSpec(a.shape, lambda b, i: (0,) * a.ndim)
    consts = [vec(norm_g), win]
    tail = [sink_rows, dw_w, vec(dw_b), vec(ln_g), vec(ln_b)]
    return pl.pallas_call(
        _layer1_kernel,
        grid=(bsz, seq // t),
        in_specs=[row(oa.shape[2]), row(ob.shape[2]), row(d), row(d), mod, full(wout0), mod, mod]
                 + [full(a) for a in consts]
                 + [pl.BlockSpec((2 * w, LANES), lambda b, i: (0, 0))]
                 + [full(a) for a in tail]
                 + [mod, full(wout1), full(vec(final_g))],
        out_specs=row(d),
        out_shape=jax.ShapeDtypeStruct((bsz, seq, d), F32),
        scratch_shapes=[pltpu.VMEM((2, 2 * w, w), F32),
                        pltpu.VMEM((w, kw), BF16), pltpu.VMEM((kw, w), BF16),
                        pltpu.VMEM((CONV_HALO + t, D_CH), F32),
                        pltpu.VMEM((SUBLANES - 1, CONV_HALO - SUBLANES + t, D_CH), F32)],
        compiler_params=_params(("arbitrary", "arbitrary"), [i in (5, 9, 17) for i in range(19)]),
        name="layer1",
    )(oa, ob, sg, x, gmod0, wout0, shift, scale, *consts, pos, *tail, gmod1, wout1, vec(final_g))


def kernel(x, c, ada_w, ada_b, norm_g, ab_w_in, ab_q_norm_g, ab_kv_norm_g, ab_w_uq, ab_w_qidx, ab_w_uv,
           ab_lam_q1, ab_lam_k1, ab_lam_q2, ab_lam_k2, ab_subln_g, ab_w_out,
           cd_w_in, cd_sinks, cd_dw_w, cd_dw_b, cd_ln_g, cd_ln_b, cd_w_out, final_g):
    bsz, seq, d = x.shape
    mods = _ada_mods(c, ada_w, ada_b)
    mod = lambda l, k: mods[l, :, k * d:(k + 1) * d].reshape(bsz, 1, d)
    pos = _key_positions(seq)
    qa, ka, vat, cq, ckv, ckvt, kidx, widxt, sg0 = _proj0(
        x, mod(0, 0), mod(0, 1), norm_g[0], ab_w_in[0], ab_q_norm_g[0], ab_kv_norm_g[0])
    oa = _diff_attn(qa, ka, vat, pos, ab_lam_q1[0], ab_lam_k1[0], ab_lam_q2[0], ab_lam_k2[0], ab_subln_g[0], 0)
    ob = _dsa(cq, ab_w_uq[0], ab_w_qidx[0], widxt, ckv, ckvt, kidx, pos, ab_w_uv[0])
    return _layer1(oa, ob, sg0, x, mod(0, 2), ab_w_out[0], mod(1, 0), mod(1, 1), norm_g[1], cd_w_in[0], pos,
                   cd_sinks[0], cd_dw_w[0], cd_dw_b[0], cd_ln_g[0], cd_ln_b[0], mod(1, 2), cd_w_out[0], final_g)
```
